```python
import math
import jax
import jax.numpy as jnp
from jax import lax
import numpy as np

D_MODEL = 1024
BATCH = 8
SEQ = 2048
DEPTH = 2
DEC_BATCH = 128
DEC_SEQ = 1
PAST_LEN = 16384
PAGE_SIZE = 128

RWKV_HEAD_DIM = 64
RWKV_HEADS = D_MODEL // RWKV_HEAD_DIM
RWKV_WIDTH = RWKV_HEADS * RWKV_HEAD_DIM
LORA_DECAY = max(32, int(round(1.8 * D_MODEL ** 0.5 / 32)) * 32)
LORA_ICLR = max(32, int(round(1.8 * D_MODEL ** 0.5 / 32)) * 32)
LORA_VRES = max(32, int(round(1.3 * D_MODEL ** 0.5 / 32)) * 32)
RWKV_GN_EPS = 64e-5
GLA_HEADS = 4
GLA_KEY_WIDTH = D_MODEL // 2
GLA_VAL_WIDTH = D_MODEL
GLA_DK = GLA_KEY_WIDTH // GLA_HEADS
GLA_DV = GLA_VAL_WIDTH // GLA_HEADS
GLA_GATE_RANK = 16
GLA_GATE_NORMALIZER = 16.0
GLA_CHUNK = 64
GLA_NORM_EPS = 1e-5
FFN_HIDDEN = ((((8 * D_MODEL + 2) // 3) + 255) // 256) * 256
NORM_EPS = 1e-6
SHIFT_SIZES = (RWKV_WIDTH, RWKV_WIDTH, RWKV_WIDTH, LORA_DECAY, LORA_ICLR)
REST_SIZES = (GLA_KEY_WIDTH, GLA_KEY_WIDTH, GLA_VAL_WIDTH, GLA_GATE_RANK, D_MODEL, D_MODEL)
N_SHIFT = sum(SHIFT_SIZES)
N_IN = N_SHIFT + sum(REST_SIZES)

kernel_name = "rwkv7_gla_parallel_hybrid_step"


def rmsnorm(x, g, eps=NORM_EPS):
    xf = x.astype(jnp.float32)
    y = xf * lax.rsqrt(jnp.mean(xf * xf, axis=-1, keepdims=True) + eps)
    return (y * g.astype(jnp.float32)).astype(x.dtype)


def split_cols(y, sizes):
    out, start = [], 0
    for s in sizes:
        out.append(y[..., start:start + s])
        start += s
    return out


def token_shift(y, y_prev, mu):
    shifted = jnp.concatenate([y_prev[:, None, :].astype(y.dtype), y[:, :-1]], axis=1)
    return y + mu.astype(y.dtype) * (shifted - y)


def rwkv7_recurrence(r, decay, k, v, kk, a, s0):
    f32 = jnp.float32
    xs = tuple(jnp.swapaxes(t.astype(f32), 0, 1) for t in (r, decay, k, v, kk, a))

    def step(S, inp):
        r_t, w_t, k_t, v_t, kk_t, a_t = inp
        sa = jnp.einsum('bhvk,bhk->bhv', S, -kk_t)
        S = (S * w_t[:, :, None, :]
             + sa[..., None] * (kk_t * a_t)[:, :, None, :]
             + v_t[..., None] * k_t[:, :, None, :])
        o_t = jnp.einsum('bhvk,bhk->bhv', S, r_t)
        return S, o_t

    s_fin, o = lax.scan(step, s0.astype(f32), xs)
    return jnp.swapaxes(o, 0, 1), s_fin


def gla_chunked(q, k, v, log_a, s0):
    f32 = jnp.float32
    B, T, H, _ = q.shape
    DV = v.shape[-1]
    C = math.gcd(T, GLA_CHUNK)
    n = T // C

    def to_chunks(t):
        return t.astype(f32).reshape(B, n, C, H, t.shape[-1]).transpose(1, 0, 3, 2, 4)

    xs = tuple(to_chunks(t) for t in (q, k, v, log_a))
    causal = jnp.tril(jnp.ones((C, C), dtype=bool))

    def step(S, inp):
        qc, kc, vc, gc = inp
        b = jnp.cumsum(gc, axis=2)
        b_last = b[:, :, -1:, :]
        q_dec = qc * jnp.exp(b)
        k_dec = kc * jnp.exp(-b)
        att = jnp.where(causal, jnp.einsum('bhid,bhjd->bhij', q_dec, k_dec), 0.0)
        o = jnp.einsum('bhij,bhjv->bhiv', att, vc) + jnp.einsum('bhid,bhdv->bhiv', q_dec, S)
        S = (S * jnp.exp(b_last)[:, :, 0, :, None]
             + jnp.einsum('bhjd,bhjv->bhdv', kc * jnp.exp(b_last - b), vc))
        return S, o

    s_fin, o = lax.scan(step, s0.astype(f32), xs)
    o = o.transpose(1, 0, 3, 2, 4).reshape(B, T, H, DV)
    return o, s_fin


def hybrid_mixer(xn, prev, v_first, s_wkv0, s_gla0, lp):
    f32 = jnp.float32
    B, T, _ = xn.shape
    proj = xn @ lp['w_in']
    shifted = token_shift(proj[..., :N_SHIFT], prev @ lp['w_in'][:, :N_SHIFT], lp['mu'])
    r, k, v, w_lo, a_lo = split_cols(shifted, SHIFT_SIZES)
    gq, gk, gv, g_lo, gate_a, gate_b = split_cols(proj[..., N_SHIFT:], REST_SIZES)

    hs = (B, T, RWKV_HEADS, RWKV_HEAD_DIM)
    w_log = -jax.nn.softplus(-(lp['w0'] + jnp.tanh(w_lo) @ lp['w2']).astype(f32)) - 0.5
    decay = jnp.exp(-jnp.exp(w_log))
    if v_first is None:
        v_first = v
    else:
        v_lo = token_shift(xn @ lp['vres_w1'], prev @ lp['vres_w1'], lp['vres_mu'])
        v = v + (v_first - v) * jax.nn.sigmoid(lp['vres_v0'] + v_lo @ lp['vres_w2'])
    a = jax.nn.sigmoid((lp['a0'] + a_lo @ lp['a2']).astype(f32))
    kk = (k * lp['k_k']).astype(f32).reshape(hs)
    kk = kk / jnp.maximum(jnp.sqrt(jnp.sum(kk * kk, axis=-1, keepdims=True)), 1e-12)
    k = k.astype(f32) * (1.0 + (a - 1.0) * lp['k_a'].astype(f32))
    r_h = r.astype(f32).reshape(hs)
    k_h = k.reshape(hs)
    v_h = v.astype(f32).reshape(hs)
    o, s_wkv = rwkv7_recurrence(r_h, decay.reshape(hs), k_h, v_h, kk, a.reshape(hs), s_wkv0)
    mean = jnp.mean(o, axis=-1, keepdims=True)
    var = jnp.mean(jnp.square(o - mean), axis=-1, keepdims=True)
    o = ((o - mean) * lax.rsqrt(var + RWKV_GN_EPS) * lp['lnx_g'].astype(f32).reshape(RWKV_HEADS, RWKV_HEAD_DIM)
         + lp['lnx_b'].astype(f32).reshape(RWKV_HEADS, RWKV_HEAD_DIM))
    bonus = jnp.sum(r_h * k_h * lp['r_k'].astype(f32), axis=-1, keepdims=True) * v_h
    rwkv_out = (o + bonus).reshape(B, T, RWKV_WIDTH)

    ks = (B, T, GLA_HEADS, GLA_DK)
    log_a = jax.nn.log_sigmoid((g_lo @ lp['gla_a2'] + lp['gla_ab']).astype(f32)) / GLA_GATE_NORMALIZER
    o_g, s_gla = gla_chunked(gq.reshape(ks) * (GLA_DK ** -0.5), gk.reshape(ks),
                             gv.reshape(B, T, GLA_HEADS, GLA_DV), log_a.reshape(ks), s_gla0)
    o_g = o_g * lax.rsqrt(jnp.mean(o_g * o_g, axis=-1, keepdims=True) + GLA_NORM_EPS) * lp['gla_g'].astype(f32)
    gla_out = o_g.reshape(B, T, GLA_VAL_WIDTH)

    merged = jax.nn.sigmoid(gate_a.astype(f32)) * rwkv_out + jax.nn.sigmoid(gate_b.astype(f32)) * gla_out
    out = merged.astype(xn.dtype) @ lp['w_o']
    return out, xn[:, -1], s_wkv, s_gla, v_first


LAYER_KEYS = {
    'w_in': 'w_in', 'mu': 'mu_shift', 'w0': 'rwkv_w0', 'w2': 'rwkv_w2', 'a0': 'rwkv_a0', 'a2': 'rwkv_a2',
    'k_k': 'rwkv_k_k', 'k_a': 'rwkv_k_a', 'r_k': 'rwkv_r_k', 'lnx_g': 'rwkv_lnx_g', 'lnx_b': 'rwkv_lnx_b',
    'gla_a2': 'gla_a2', 'gla_ab': 'gla_ab', 'gla_g': 'gla_norm_g', 'w_o': 'w_o',
}
VRES_KEYS = {'vres_w1': 'vres_w1', 'vres_mu': 'vres_mu', 'vres_w2': 'vres_w2', 'vres_v0': 'vres_v0'}


def run_group(x, shift_state, wkv_state, gla_state, params):
    h = x
    new_shift, new_wkv, new_gla = [], [], []
    v_first = None
    for l in range(DEPTH):
        lp = {short: params[name][l] for short, name in LAYER_KEYS.items()}
        if l > 0:
            lp.update({short: params[name][l - 1] for short, name in VRES_KEYS.items()})
        xn = rmsnorm(h, params['norm_mix'][l])
        mix, sh, sw, sg, v_first = hybrid_mixer(xn, shift_state[l], v_first, wkv_state[l], gla_state[l], lp)
        h = h + mix.astype(h.dtype)
        xn = rmsnorm(h, params['norm_ffn'][l])
        gate, up = split_cols(xn @ params['ffn_w_gu'][l], (FFN_HIDDEN, FFN_HIDDEN))
        h = h + ((jax.nn.silu(gate) * up) @ params['ffn_w_down'][l]).astype(h.dtype)
        new_shift.append(sh)
        new_wkv.append(sw)
        new_gla.append(sg)
    y = rmsnorm(h, params['norm_final'])
    return y, jnp.stack(new_shift), jnp.stack(new_wkv), jnp.stack(new_gla)


def setup_inputs(seed: int = 0) -> dict:
    key = jax.random.key(seed)
    ks = iter(jax.random.split(key, 40))
    f32 = jnp.float32

    def nrm(shape, scale):
        return jax.random.normal(next(ks), shape, f32) * scale

    def unif(shape, lo, hi):
        return jax.random.uniform(next(ks), shape, f32, lo, hi)

    L, Lv, D = DEPTH, DEPTH - 1, D_MODEL
    return {
        'x_prompt': nrm((BATCH, SEQ, D), 1.0),
        'x_sample': nrm((DEC_BATCH, DEC_SEQ, D), 1.0),
        'state_shift': nrm((L, DEC_BATCH, D), 1.0),
        'state_wkv': nrm((L, DEC_BATCH, RWKV_HEADS, RWKV_HEAD_DIM, RWKV_HEAD_DIM), 0.05),
        'state_gla': nrm((L, DEC_BATCH, GLA_HEADS, GLA_DK, GLA_DV), 0.05),
        'norm_mix': 1.0 + nrm((L, D), 0.05),
        'w_in': nrm((L, D, N_IN), D ** -0.5),
        'mu_shift': unif((L, N_SHIFT), 0.0, 1.0),
        'rwkv_w0': unif((L, RWKV_WIDTH), -6.0, -1.0),
        'rwkv_w2': nrm((L, LORA_DECAY, RWKV_WIDTH), 0.5 * LORA_DECAY ** -0.5),
        'rwkv_a0': nrm((L, RWKV_WIDTH), 0.1),
        'rwkv_a2': nrm((L, LORA_ICLR, RWKV_WIDTH), 0.5 * LORA_ICLR ** -0.5),
        'rwkv_k_k': 0.85 + nrm((L, RWKV_WIDTH), 0.05),
        'rwkv_k_a': 1.0 + nrm((L, RWKV_WIDTH), 0.05),
        'rwkv_r_k': nrm((L, RWKV_HEADS, RWKV_HEAD_DIM), 0.1),
        'rwkv_lnx_g': 1.0 + nrm((L, RWKV_WIDTH), 0.05),
        'rwkv_lnx_b': nrm((L, RWKV_WIDTH), 0.01),
        'vres_w1': nrm((Lv, D, LORA_VRES), D ** -0.5),
        'vres_mu': unif((Lv, LORA_VRES), 0.0, 1.0),
        'vres_w2': nrm((Lv, LORA_VRES, RWKV_WIDTH), 0.5 * LORA_VRES ** -0.5),
        'vres_v0': nrm((Lv, RWKV_WIDTH), 0.1),
        'gla_a2': nrm((L, GLA_GATE_RANK, GLA_KEY_WIDTH), GLA_GATE_RANK ** -0.5),
        'gla_ab': nrm((L, GLA_KEY_WIDTH), 0.01),
        'gla_norm_g': 1.0 + nrm((L, GLA_DV), 0.05),
        'w_o': nrm((L, D, D), D ** -0.5),
        'norm_ffn': 1.0 + nrm((L, D), 0.05),
        'ffn_w_gu': nrm((L, D, 2 * FFN_HIDDEN), D ** -0.5),
        'ffn_w_down': nrm((L, FFN_HIDDEN, D), FFN_HIDDEN ** -0.5),
        'norm_final': 1.0 + nrm((D,), 0.05),
    }


def reference(x_prompt, x_sample, state_shift, state_wkv, state_gla, norm_mix, w_in, mu_shift,
              rwkv_w0, rwkv_w2, rwkv_a0, rwkv_a2, rwkv_k_k, rwkv_k_a, rwkv_r_k, rwkv_lnx_g, rwkv_lnx_b,
              vres_w1, vres_mu, vres_w2, vres_v0, gla_a2, gla_ab, gla_norm_g, w_o, norm_ffn,
              ffn_w_gu, ffn_w_down, norm_final):
    params = {
        'norm_mix': norm_mix, 'w_in': w_in, 'mu_shift': mu_shift,
        'rwkv_w0': rwkv_w0, 'rwkv_w2': rwkv_w2, 'rwkv_a0': rwkv_a0, 'rwkv_a2': rwkv_a2,
        'rwkv_k_k': rwkv_k_k, 'rwkv_k_a': rwkv_k_a, 'rwkv_r_k': rwkv_r_k,
        'rwkv_lnx_g': rwkv_lnx_g, 'rwkv_lnx_b': rwkv_lnx_b,
        'vres_w1': vres_w1, 'vres_mu': vres_mu, 'vres_w2': vres_w2, 'vres_v0': vres_v0,
        'gla_a2': gla_a2, 'gla_ab': gla_ab, 'gla_norm_g': gla_norm_g, 'w_o': w_o,
        'norm_ffn': norm_ffn, 'ffn_w_gu': ffn_w_gu, 'ffn_w_down': ffn_w_down, 'norm_final': norm_final,
    }
    B = x_prompt.shape[0]
    p_shift0 = jnp.zeros((DEPTH, B, D_MODEL), x_prompt.dtype)
    p_wkv0 = jnp.zeros((DEPTH, B, RWKV_HEADS, RWKV_HEAD_DIM, RWKV_HEAD_DIM), jnp.float32)
    p_gla0 = jnp.zeros((DEPTH, B, GLA_HEADS, GLA_DK, GLA_DV), jnp.float32)
    y_prompt, shift_p, wkv_p, gla_p = run_group(x_prompt, p_shift0, p_wkv0, p_gla0, params)
    y_sample, shift_s, wkv_s, gla_s = run_group(x_sample, state_shift, state_wkv, state_gla, params)
    return (y_prompt, y_sample, shift_p, wkv_p, gla_p, shift_s, wkv_s, gla_s)
```

```python
import functools

import jax
import jax.numpy as jnp
from jax import lax
from jax.experimental import pallas as pl
from jax.experimental.pallas import tpu as pltpu

F32 = jnp.float32
BF16 = jnp.bfloat16

LANES = 128
D_MODEL = 1024
RWKV_HEAD_DIM = 64
RWKV_HEADS = D_MODEL // RWKV_HEAD_DIM
HEADS_PER_PAIR = LANES // RWKV_HEAD_DIM
N_PAIRS = RWKV_HEADS // HEADS_PER_PAIR
LORA_DECAY = 64
LORA_ICLR = 64
LORA_VRES = 32
RWKV_GN_EPS = 64e-5
GLA_HEADS = 4
GLA_DK = 128
GLA_DV = 256
GLA_GATE_RANK = 16
GLA_GATE_NORMALIZER = 16.0
GLA_NORM_EPS = 1e-5
FFN_HIDDEN = 2816
NORM_EPS = 1e-6
CHUNK = 64
SAMPLE_CHUNK = 8

COL_R, COL_K, COL_V = 0, 1024, 2048
COL_GATE_A, COL_GATE_B = 3072, 4096
COL_GV, COL_GQ, COL_GK = 5120, 6144, 6656
COL_LO, COL_VLO, COL_GLO = 7168, 7296, 7424
N_PROJ = 7680
PROJ_TN = 1536
VMEM_LIMIT = 48 * 1024 * 1024


def _cparams(sem):
    return pltpu.CompilerParams(dimension_semantics=sem, vmem_limit_bytes=VMEM_LIMIT)


def _mm(a, b):
    return jnp.dot(a, b, preferred_element_type=F32)


def _mm_nt(a, b):
    return lax.dot_general(a, b, (((1,), (1,)), ((), ())), preferred_element_type=F32)


def _mm_tn(a, b):
    return lax.dot_general(a, b, (((0,), (0,)), ((), ())), preferred_element_type=F32)


def _sigmoid(x):
    return 1.0 / (1.0 + jnp.exp(-x))


def _softplus(x):
    return jnp.maximum(x, 0.0) + jnp.log1p(jnp.exp(-jnp.abs(x)))


def _cumsum_rows(x):
    n = x.shape[0]
    row = lax.broadcasted_iota(jnp.int32, x.shape, 0)
    s = 1
    while s < n:
        x = x + jnp.where(row >= s, pltpu.roll(x, s, 0), 0.0)
        s *= 2
    return x


def _rms_rows(x, g):
    ms = jnp.mean(x * x, axis=-1, keepdims=True)
    return x * lax.rsqrt(ms + NORM_EPS) * g


def _proj_kernel(*refs, sample, tiles_per_seq, tm):
    if sample:
        x_ref, g_ref, w_ref, mu_ref, prev_ref, o_ref, xs_ref, ps_ref = refs
    else:
        x_ref, g_ref, w_ref, mu_ref, o_ref, xs_ref, last_ref = refs
    i = pl.program_id(0)
    j = pl.program_id(1)

    @pl.when(j == 0)
    def _():
        xs_ref[...] = _rms_rows(x_ref[...], g_ref[...]).astype(BF16)
        if sample:
            ps_ref[...] = prev_ref[...].astype(BF16)

    y = _mm(xs_ref[...], w_ref[...])
    if sample:
        shifted = _mm(ps_ref[...], w_ref[...])
    else:
        @pl.when(i == 0)
        def _():
            last_ref[j] = jnp.zeros((8, y.shape[1]), F32)

        carried = last_ref[j][7:8, :]
        prev_row = jnp.where(i % tiles_per_seq == 0, 0.0, carried)
        row = lax.broadcasted_iota(jnp.int32, y.shape, 0)
        shifted = jnp.where(row == 0, prev_row, pltpu.roll(y, 1, 0))
        last_ref[j] = y[tm - 8:, :]
    o_ref[...] = y + mu_ref[...] * (shifted - y)


def _proj_call(x, g, w, mu, prev, seq_len):
    m = x.shape[0]
    sample = prev is not None
    tm = min(512, m if sample else seq_len)
    nj = N_PROJ // PROJ_TN
    in_specs = [
        pl.BlockSpec((tm, D_MODEL), lambda i, j: (i, 0)),
        pl.BlockSpec((1, D_MODEL), lambda i, j: (0, 0)),
        pl.BlockSpec((D_MODEL, PROJ_TN), lambda i, j: (0, j)),
        pl.BlockSpec((1, PROJ_TN), lambda i, j: (0, j)),
    ]
    args = [x, g, w, mu]
    scratch = [pltpu.VMEM((tm, D_MODEL), BF16)]
    if sample:
        in_specs.append(pl.BlockSpec((tm, D_MODEL), lambda i, j: (i, 0)))
        args.append(prev)
        scratch.append(pltpu.VMEM((tm, D_MODEL), BF16))
    else:
        scratch.append(pltpu.VMEM((nj, 8, PROJ_TN), F32))
    return pl.pallas_call(
        functools.partial(_proj_kernel, sample=sample, tiles_per_seq=max(seq_len // tm, 1), tm=tm),
        grid=(m // tm, nj),
        in_specs=in_specs,
        out_specs=pl.BlockSpec((tm, PROJ_TN), lambda i, j: (i, j)),
        out_shape=jax.ShapeDtypeStruct((m, N_PROJ), F32),
        scratch_shapes=scratch,
        compiler_params=_cparams(("arbitrary", "arbitrary")),
        name="proj_sample" if sample else "proj_prompt",
    )(*args)


def _rwkv_kernel(*refs, C, PP, n_valid, has_s0, has_vres):
    it = iter(refs)
    r_ref, k_ref, v_ref, lo_ref = next(it), next(it), next(it), next(it)
    vlo_ref = vf_ref = vw2_ref = s0_ref = None
    if has_vres:
        vlo_ref, vf_ref = next(it), next(it)
    if has_s0:
        s0_ref = next(it)
    w2a_ref = next(it)
    if has_vres:
        vw2_ref = next(it)
    vec_ref = next(it)
    o_ref, so_ref, s_scr = next(it), next(it), next(it)

    c = pl.program_id(2)
    H = RWKV_HEAD_DIM

    @pl.when(c == 0)
    def _():
        s_scr[...] = jnp.zeros_like(s_scr)
        if has_s0:
            for p in range(PP):
                s_scr[p, 0:H, 0:H] = s0_ref[0, 2 * p]
                s_scr[p, H:2 * H, H:2 * H] = s0_ref[0, 2 * p + 1]

    lane = lax.broadcasted_iota(jnp.int32, (C, LANES), 1)
    head0 = lane < H
    trow = lax.broadcasted_iota(jnp.int32, (C, LANES), 0)
    ri = lax.broadcasted_iota(jnp.int32, (2 * C, 2 * C), 0)
    ci = lax.broadcasted_iota(jnp.int32, (2 * C, 2 * C), 1)
    same_head = (ri // C) == (ci // C)
    strict = same_head & ((ri % C) > (ci % C))
    incl = same_head & ((ri % C) >= (ci % C))
    eye = (ri == ci).astype(F32)

    def stack(x):
        return jnp.concatenate([jnp.where(head0, x, 0.0), jnp.where(head0, 0.0, x)], axis=0)

    def head_sum(x):
        s0 = jnp.sum(jnp.where(head0, x, 0.0), axis=-1, keepdims=True)
        s1 = jnp.sum(jnp.where(head0, 0.0, x), axis=-1, keepdims=True)
        return jnp.where(head0, s0, s1)

    lo = lo_ref[0]
    w_in = jnp.where(head0, jnp.tanh(lo), 0.0)
    a_in = jnp.where(head0, 0.0, lo)
    n_sq = C.bit_length() - 2

    for p in range(PP):
        sl = slice(p * LANES, (p + 1) * LANES)
        r = r_ref[0, :, sl]
        k = k_ref[0, :, sl]
        v = v_ref[0, :, sl]
        w0, a0 = vec_ref[0:1, sl], vec_ref[1:2, sl]
        k_k, k_a, r_k = vec_ref[2:3, sl], vec_ref[3:4, sl], vec_ref[4:5, sl]
        lnx_g, lnx_b, v0 = vec_ref[5:6, sl], vec_ref[6:7, sl], vec_ref[7:8, sl]
        w2a = w2a_ref[:, sl]

        w_log = -_softplus(-(w0 + _mm(w_in, w2a))) - 0.5
        ld = -jnp.exp(w_log)
        if n_valid < C:
            ld = jnp.where(trow < n_valid, ld, 0.0)
        a = _sigmoid(a0 + _mm(a_in, w2a))
        if has_vres:
            v = v + (vf_ref[0, :, sl] - v) * _sigmoid(v0 + _mm(vlo_ref[0], vw2_ref[:, sl]))
        kk = k * k_k
        kk = kk / jnp.maximum(jnp.sqrt(head_sum(kk * kk)), 1e-12)
        k = k * (1.0 + (a - 1.0) * k_a)

        b = _cumsum_rows(ld)
        e_b = jnp.exp(b)
        e_nb = jnp.exp(-b)
        al = stack(-kk * jnp.exp(b - ld))
        be = stack(a * kk * e_nb)
        kt = stack(k * e_nb)
        rt = stack(r * e_b)
        vs = stack(v)
        S = s_scr[p]

        a_ab = jnp.where(strict, _mm_nt(al, be), 0.0)
        a_ak = jnp.where(strict, _mm_nt(al, kt), 0.0)
        a_rb = jnp.where(incl, _mm_nt(rt, be), 0.0)
        a_rk = jnp.where(incl, _mm_nt(rt, kt), 0.0)
        tinv = eye + a_ab
        pw = a_ab
        for _ in range(n_sq):
            pw = _mm(pw, pw)
            tinv = tinv + _mm(tinv, pw)
        cm = _mm(tinv, _mm_nt(al, S) + _mm(a_ak, vs))
        o2 = _mm_nt(rt, S) + _mm(a_rb, cm) + _mm(a_rk, vs)
        s_new = (S + _mm_tn(cm, be) + _mm_tn(vs, kt)) * e_b[C - 1:C, :]
        s_scr[p] = s_new

        o = o2[:C] + o2[C:]
        mean = head_sum(o) * (1.0 / H)
        d = o - mean
        var = head_sum(d * d) * (1.0 / H)
        o = d * lax.rsqrt(var + RWKV_GN_EPS) * lnx_g + lnx_b
        o_ref[0, :, sl] = o + head_sum(r * k * r_k) * v

    @pl.when(c == pl.num_programs(2) - 1)
    def _():
        for p in range(PP):
            so_ref[0, 2 * p] = s_scr[p, 0:H, 0:H]
            so_ref[0, 2 * p + 1] = s_scr[p, H:2 * H, H:2 * H]


def _rwkv_call(P, Pfirst, s0, w2a, vw2, vecs, *, C, n_valid, PP):
    B, T, _ = P.shape
    L = LANES * PP
    has_vres = Pfirst is not None
    has_s0 = s0 is not None

    def col(off, width):
        base = off // width
        return lambda b, p, c: (b, c, base + p)

    def fixed(off):
        base = off // LANES
        return lambda b, p, c: (b, c, base)

    in_specs = [pl.BlockSpec((1, C, L), col(COL_R, L)),
                pl.BlockSpec((1, C, L), col(COL_K, L)),
                pl.BlockSpec((1, C, L), col(COL_V, L)),
                pl.BlockSpec((1, C, LANES), fixed(COL_LO))]
    args = [P, P, P, P]
    if has_vres:
        in_specs += [pl.BlockSpec((1, C, LANES), fixed(COL_VLO)),
                     pl.BlockSpec((1, C, L), col(COL_V, L))]
        args += [P, Pfirst]
    if has_s0:
        in_specs.append(pl.BlockSpec((1, HEADS_PER_PAIR * PP, RWKV_HEAD_DIM, RWKV_HEAD_DIM),
                                     lambda b, p, c: (b, p, 0, 0)))
        args.append(s0)
    in_specs.append(pl.BlockSpec((LANES, L), lambda b, p, c: (0, p)))
    args.append(w2a)
    if has_vres:
        in_specs.append(pl.BlockSpec((LANES, L), lambda b, p, c: (0, p)))
        args.append(vw2)
    in_specs.append(pl.BlockSpec((8, L), lambda b, p, c: (0, p)))
    args.append(vecs)
    return pl.pallas_call(
        functools.partial(_rwkv_kernel, C=C, PP=PP, n_valid=n_valid, has_s0=has_s0, has_vres=has_vres),
        grid=(B, N_PAIRS // PP, T // C),
        in_specs=in_specs,
        out_specs=[pl.BlockSpec((1, C, L), lambda b, p, c: (b, c, p)),
                   pl.BlockSpec((1, HEADS_PER_PAIR * PP, RWKV_HEAD_DIM, RWKV_HEAD_DIM),
                                lambda b, p, c: (b, p, 0, 0))],
        out_shape=[jax.ShapeDtypeStruct((B, T, D_MODEL), F32),
                   jax.ShapeDtypeStruct((B, RWKV_HEADS, RWKV_HEAD_DIM, RWKV_HEAD_DIM), F32)],
        scratch_shapes=[pltpu.VMEM((PP, LANES, LANES), F32)],
        compiler_params=_cparams(("parallel", "parallel", "arbitrary")),
        name="rwkv7_chunk",
    )(*args)


def _gla_kernel(*refs, C, n_valid, has_s0):
    if has_s0:
        q_ref, k_ref, v_ref, glo_ref, s0_ref, ga2_ref, gab_ref, g_ref, o_ref, so_ref, s_scr = refs
    else:
        q_ref, k_ref, v_ref, glo_ref, ga2_ref, gab_ref, g_ref, o_ref, so_ref, s_scr = refs
    c = pl.program_id(2)

    @pl.when(c == 0)
    def _():
        if has_s0:
            s_scr[...] = s0_ref[0, 0].T
        else:
            s_scr[...] = jnp.zeros_like(s_scr)

    q = q_ref[0] * (GLA_DK ** -0.5)
    k = k_ref[0]
    v = v_ref[0]
    la = -_softplus(-(_mm(glo_ref[0], ga2_ref[...]) + gab_ref[...])) * (1.0 / GLA_GATE_NORMALIZER)
    if n_valid < C:
        trow = lax.broadcasted_iota(jnp.int32, la.shape, 0)
        la = jnp.where(trow < n_valid, la, 0.0)
    b = _cumsum_rows(la)
    b_last = b[C - 1:C, :]
    qd = q * jnp.exp(b)
    kd = k * jnp.exp(-b)
    ri = lax.broadcasted_iota(jnp.int32, (C, C), 0)
    ci = lax.broadcasted_iota(jnp.int32, (C, C), 1)
    att = jnp.where(ri >= ci, _mm_nt(qd, kd), 0.0)
    St = s_scr[...]
    o = _mm(att, v) + _mm_nt(qd, St)
    s_scr[...] = St * jnp.exp(b_last) + _mm_tn(v, k * jnp.exp(b_last - b))
    o_ref[0] = o * lax.rsqrt(jnp.mean(o * o, axis=-1, keepdims=True) + GLA_NORM_EPS) * g_ref[...]

    @pl.when(c == pl.num_programs(2) - 1)
    def _():
        so_ref[0, 0] = s_scr[...].T


def _gla_call(P, s0, ga2, gab, g, *, C, n_valid):
    B, T, _ = P.shape
    has_s0 = s0 is not None
    in_specs = [pl.BlockSpec((1, C, GLA_DK), lambda b, h, c: (b, c, COL_GQ // GLA_DK + h)),
                pl.BlockSpec((1, C, GLA_DK), lambda b, h, c: (b, c, COL_GK // GLA_DK + h)),
                pl.BlockSpec((1, C, GLA_DV), lambda b, h, c: (b, c, COL_GV // GLA_DV + h)),
                pl.BlockSpec((1, C, LANES), lambda b, h, c: (b, c, COL_GLO // LANES))]
    args = [P, P, P, P]
    if has_s0:
        in_specs.append(pl.BlockSpec((1, 1, GLA_DK, GLA_DV), lambda b, h, c: (b, h, 0, 0)))
        args.append(s0)
    in_specs += [pl.BlockSpec((LANES, GLA_DK), lambda b, h, c: (0, h)),
                 pl.BlockSpec((1, GLA_DK), lambda b, h, c: (0, h)),
                 pl.BlockSpec((1, GLA_DV), lambda b, h, c: (0, 0))]
    args += [ga2, gab, g]
    return pl.pallas_call(
        functools.partial(_gla_kernel, C=C, n_valid=n_valid, has_s0=has_s0),
        grid=(B, GLA_HEADS, T // C),
        in_specs=in_specs,
        out_specs=[pl.BlockSpec((1, C, GLA_DV), lambda b, h, c: (b, c, h)),
                   pl.BlockSpec((1, 1, GLA_DK, GLA_DV), lambda b, h, c: (b, h, 0, 0))],
        out_shape=[jax.ShapeDtypeStruct((B, T, GLA_HEADS * GLA_DV), F32),
                   jax.ShapeDtypeStruct((B, GLA_HEADS, GLA_DK, GLA_DV), F32)],
        scratch_shapes=[pltpu.VMEM((GLA_DV, GLA_DK), F32)],
        compiler_params=_cparams(("parallel", "parallel", "arbitrary")),
        name="gla_chunk",
    )(*args)


def _wo_kernel(ga_ref, gb_ref, ro_ref, go_ref, h_ref, w_ref, o_ref):
    merged = _sigmoid(ga_ref[...]) * ro_ref[...] + _sigmoid(gb_ref[...]) * go_ref[...]
    o_ref[...] = h_ref[...] + _mm(merged.astype(BF16), w_ref[...])


def _wo_call(P, ro, go, h, w):
    m = h.shape[0]
    tm = min(512, m)
    row = lambda i: (i, 0)
    return pl.pallas_call(
        _wo_kernel,
        grid=(m // tm,),
        in_specs=[pl.BlockSpec((tm, D_MODEL), lambda i: (i, COL_GATE_A // D_MODEL)),
                  pl.BlockSpec((tm, D_MODEL), lambda i: (i, COL_GATE_B // D_MODEL)),
                  pl.BlockSpec((tm, D_MODEL), row),
                  pl.BlockSpec((tm, D_MODEL), row),
                  pl.BlockSpec((tm, D_MODEL), row),
                  pl.BlockSpec((D_MODEL, D_MODEL), lambda i: (0, 0))],
        out_specs=pl.BlockSpec((tm, D_MODEL), row),
        out_shape=jax.ShapeDtypeStruct((m, D_MODEL), F32),
        compiler_params=_cparams(("parallel",)),
        name="merge_wo",
    )(P, P, ro, go, h, w)


FFN_TN = FFN_HIDDEN // 2


def _ffn_gu_kernel(h_ref, g_ref, wg_ref, wu_ref, o_ref, xs_ref):
    @pl.when(pl.program_id(1) == 0)
    def _():
        xs_ref[...] = _rms_rows(h_ref[...], g_ref[...]).astype(BF16)

    xs = xs_ref[...]
    gate = _mm(xs, wg_ref[...])
    up = _mm(xs, wu_ref[...])
    o_ref[...] = (gate * _sigmoid(gate) * up).astype(BF16)


def _ffn_gu_call(h, g, w_gu):
    m = h.shape[0]
    tm = min(512, m)
    nj = FFN_HIDDEN // FFN_TN
    return pl.pallas_call(
        _ffn_gu_kernel,
        grid=(m // tm, nj),
        in_specs=[pl.BlockSpec((tm, D_MODEL), lambda i, j: (i, 0)),
                  pl.BlockSpec((1, D_MODEL), lambda i, j: (0, 0)),
                  pl.BlockSpec((D_MODEL, FFN_TN), lambda i, j: (0, j)),
                  pl.BlockSpec((D_MODEL, FFN_TN), lambda i, j: (0, j + nj))],
        out_specs=pl.BlockSpec((tm, FFN_TN), lambda i, j: (i, j)),
        out_shape=jax.ShapeDtypeStruct((m, FFN_HIDDEN), BF16),
        scratch_shapes=[pltpu.VMEM((tm, D_MODEL), BF16)],
        compiler_params=_cparams(("parallel", "arbitrary")),
        name="ffn_gate_up",
    )(h, g, w_gu, w_gu)


def _ffn_down_kernel(a_ref, h_ref, w_ref, gf_ref, o_ref, *, final_norm):
    out = h_ref[...] + _mm(a_ref[...], w_ref[...])
    if final_norm:
        out = _rms_rows(out, gf_ref[...])
    o_ref[...] = out


def _ffn_down_call(act, h, w, gf, final_norm):
    m = h.shape[0]
    tm = min(512, m)
    row = lambda i: (i, 0)
    return pl.pallas_call(
        functools.partial(_ffn_down_kernel, final_norm=final_norm),
        grid=(m // tm,),
        in_specs=[pl.BlockSpec((tm, FFN_HIDDEN), row),
                  pl.BlockSpec((tm, D_MODEL), row),
                  pl.BlockSpec((FFN_HIDDEN, D_MODEL), lambda i: (0, 0)),
                  pl.BlockSpec((1, D_MODEL), lambda i: (0, 0))],
        out_specs=pl.BlockSpec((tm, D_MODEL), row),
        out_shape=jax.ShapeDtypeStruct((m, D_MODEL), F32),
        compiler_params=_cparams(("parallel",)),
        name="ffn_down",
    )(act, h, w, gf)


def _rms_kernel(x_ref, g_ref, o_ref):
    o_ref[...] = _rms_rows(x_ref[...], g_ref[...])


def _rms_call(x, g):
    return pl.pallas_call(
        _rms_kernel,
        out_shape=jax.ShapeDtypeStruct(x.shape, F32),
        name="rmsnorm_rows",
    )(x, g)


def _pad_cols(x, n):
    return jnp.pad(x, ((0, 0), (0, n - x.shape[1])))


def _pad_rows(x, n):
    return jnp.pad(x, ((0, n - x.shape[0]), (0, 0)))


def _layer_weights(l, w_in, mu_shift, vres_w1, vres_mu, p):
    W = w_in[l]
    mu = mu_shift[l]
    sizes = (1024, 1024, 1024, 64, 64, 512, 512, 1024, 16, 1024, 1024)
    offs = [0]
    for s in sizes:
        offs.append(offs[-1] + s)
    (r, k, v, wlo, alo, gq, gk, gv, glo, ga, gb) = [W[:, offs[i]:offs[i + 1]] for i in range(len(sizes))]
    zeros_mu = lambda n: jnp.zeros((n,), F32)
    if l > 0:
        vlo_w = _pad_cols(vres_w1[l - 1], LANES)
        vlo_mu = jnp.pad(vres_mu[l - 1], (0, LANES - LORA_VRES))
    else:
        vlo_w = jnp.zeros((D_MODEL, LANES), F32)
        vlo_mu = zeros_mu(LANES)
    w_cat = jnp.concatenate([r, k, v, ga, gb, gv, gq, gk, wlo, alo, vlo_w, _pad_cols(glo, LANES),
                             jnp.zeros((D_MODEL, LANES), F32)], axis=1).astype(BF16)
    mu_cat = jnp.concatenate([mu[:3072], zeros_mu(2048 + 2048), mu[3072:3200], vlo_mu,
                              zeros_mu(2 * LANES)])[None, :]
    lw = dict(w_cat=w_cat, mu_cat=mu_cat)
    lw['w2a'] = jnp.concatenate([p['rwkv_w2'][l], p['rwkv_a2'][l]], axis=0)
    lw['vecs'] = jnp.stack([p['rwkv_w0'][l], p['rwkv_a0'][l], p['rwkv_k_k'][l], p['rwkv_k_a'][l],
                            p['rwkv_r_k'][l].reshape(-1), p['rwkv_lnx_g'][l], p['rwkv_lnx_b'][l],
                            p['vres_v0'][l - 1] if l > 0 else zeros_mu(D_MODEL)], axis=0)
    lw['vw2'] = _pad_rows(p['vres_w2'][l - 1], LANES) if l > 0 else None
    lw['ga2'] = _pad_rows(p['gla_a2'][l], LANES)
    lw['gab'] = p['gla_ab'][l][None, :]
    lw['gla_g'] = p['gla_norm_g'][l][None, :]
    lw['w_o'] = p['w_o'][l].astype(BF16)
    lw['w_gu'] = p['ffn_w_gu'][l].astype(BF16)
    lw['w_down'] = p['ffn_w_down'][l].astype(BF16)
    lw['g_mix'] = p['norm_mix'][l][None, :]
    lw['g_ffn'] = p['norm_ffn'][l][None, :]
    return lw


def _run_group(x, shift_state, wkv_state, gla_state, weights, g_final, pairs_per_step):
    B, T, _ = x.shape
    fresh = shift_state is None
    C = CHUNK if fresh else SAMPLE_CHUNK
    h = x.reshape(B * T, D_MODEL)
    new_shift, new_wkv, new_gla = [], [], []
    P_first = None
    depth = len(weights)
    for l, lw in enumerate(weights):
        new_shift.append(_rms_call(h.reshape(B, T, D_MODEL)[:, -1], lw['g_mix']))
        if fresh:
            P = _proj_call(h, lw['g_mix'], lw['w_cat'], lw['mu_cat'], None, T).reshape(B, T, N_PROJ)
        else:
            P = _proj_call(h, lw['g_mix'], lw['w_cat'], lw['mu_cat'], shift_state[l], T)
            P = jnp.pad(P[:, None, :], ((0, 0), (0, C - T), (0, 0)))
        Tp = P.shape[1]
        ro, s_wkv = _rwkv_call(P, P_first if l > 0 else None, None if fresh else wkv_state[l],
                               lw['w2a'], lw['vw2'], lw['vecs'], C=C, n_valid=min(T, C), PP=pairs_per_step)
        go, s_gla = _gla_call(P, None if fresh else gla_state[l], lw['ga2'], lw['gab'], lw['gla_g'],
                              C=C, n_valid=min(T, C))
        if l == 0:
            P_first = P
        if Tp != T:
            ro, go = ro[:, :T], go[:, :T]
            P2 = P[:, :T].reshape(B * T, N_PROJ)
        else:
            P2 = P.reshape(B * T, N_PROJ)
        h = _wo_call(P2, ro.reshape(B * T, D_MODEL), go.reshape(B * T, D_MODEL), h, lw['w_o'])
        act = _ffn_gu_call(h, lw['g_ffn'], lw['w_gu'])
        h = _ffn_down_call(act, h, lw['w_down'], g_final, final_norm=(l == depth - 1))
        new_wkv.append(s_wkv)
        new_gla.append(s_gla)
    return (h.reshape(B, T, D_MODEL), jnp.stack(new_shift), jnp.stack(new_wkv), jnp.stack(new_gla))


def kernel(x_prompt, x_sample, state_shift, state_wkv, state_gla, norm_mix, w_in, mu_shift, rwkv_w0, rwkv_w2, rwkv_a0, rwkv_a2, rwkv_k_k, rwkv_k_a, rwkv_r_k, rwkv_lnx_g, rwkv_lnx_b, vres_w1, vres_mu, vres_w2, vres_v0, gla_a2, gla_ab, gla_norm_g, w_o, norm_ffn, ffn_w_gu, ffn_w_down, norm_final):
    p = dict(norm_mix=norm_mix, rwkv_w0=rwkv_w0, rwkv_w2=rwkv_w2, rwkv_a0=rwkv_a0, rwkv_a2=rwkv_a2,
             rwkv_k_k=rwkv_k_k, rwkv_k_a=rwkv_k_a, rwkv_r_k=rwkv_r_k, rwkv_lnx_g=rwkv_lnx_g,
             rwkv_lnx_b=rwkv_lnx_b, vres_w2=vres_w2, vres_v0=vres_v0, gla_a2=gla_a2, gla_ab=gla_ab,
             gla_norm_g=gla_norm_g, w_o=w_o, norm_ffn=norm_ffn, ffn_w_gu=ffn_w_gu, ffn_w_down=ffn_w_down)
    depth = w_in.shape[0]
    weights = [_layer_weights(l, w_in, mu_shift, vres_w1, vres_mu, p) for l in range(depth)]
    g_final = norm_final[None, :]
    y_p, shift_p, wkv_p, gla_p = _run_group(x_prompt, None, None, None, weights, g_final, 2)
    y_s, shift_s, wkv_s, gla_s = _run_group(x_sample, state_shift, state_wkv, state_gla, weights, g_final, 2)
    return (y_p, y_s, shift_p, wkv_p, gla_p, shift_s, wkv_s, gla_s)
```

```python
import functools

import jax
import jax.numpy as jnp
from jax import lax
from jax.experimental import pallas as pl
from jax.experimental.pallas import tpu as pltpu

F32 = jnp.float32
BF16 = jnp.bfloat16

LANES = 128
D_MODEL = 1024
RWKV_HEAD_DIM = 64
RWKV_HEADS = D_MODEL // RWKV_HEAD_DIM
HEADS_PER_PAIR = LANES // RWKV_HEAD_DIM
N_PAIRS = RWKV_HEADS // HEADS_PER_PAIR
LORA_DECAY = 64
LORA_ICLR = 64
LORA_VRES = 32
RWKV_GN_EPS = 64e-5
GLA_HEADS = 4
GLA_DK = 128
GLA_DV = 256
GLA_GATE_RANK = 16
GLA_GATE_NORMALIZER = 16.0
GLA_NORM_EPS = 1e-5
FFN_HIDDEN = 2816
NORM_EPS = 1e-6
CHUNK = 64
SAMPLE_CHUNK = 8

COL_R, COL_K, COL_V = 0, 1024, 2048
COL_GATE_A, COL_GATE_B = 3072, 4096
COL_GV, COL_GQ, COL_GK = 5120, 6144, 6656
COL_LO, COL_VLO, COL_GLO = 7168, 7296, 7424
N_PROJ = 7680
PROJ_TN = 1536
VMEM_LIMIT = 48 * 1024 * 1024


def _cparams(sem):
    return pltpu.CompilerParams(dimension_semantics=sem, vmem_limit_bytes=VMEM_LIMIT)


def _mm(a, b):
    return jnp.dot(a, b, preferred_element_type=F32)


def _mm_nt(a, b):
    return lax.dot_general(a, b, (((1,), (1,)), ((), ())), preferred_element_type=F32)


def _mm_tn(a, b):
    return lax.dot_general(a, b, (((0,), (0,)), ((), ())), preferred_element_type=F32)


def _sigmoid(x):
    return 1.0 / (1.0 + jnp.exp(-x))


def _softplus(x):
    return jnp.maximum(x, 0.0) + jnp.log1p(jnp.exp(-jnp.abs(x)))


def _cumsum_rows(x):
    n = x.shape[0]
    row = lax.broadcasted_iota(jnp.int32, x.shape, 0)
    s = 1
    while s < n:
        x = x + jnp.where(row >= s, pltpu.roll(x, s, 0), 0.0)
        s *= 2
    return x


def _rms_rows(x, g):
    ms = jnp.mean(x * x, axis=-1, keepdims=True)
    return x * lax.rsqrt(ms + NORM_EPS) * g


def _proj_kernel(*refs, sample, tiles_per_seq, tm):
    if sample:
        x_ref, g_ref, w_ref, mu_ref, prev_ref, o_ref, xs_ref, ps_ref = refs
    else:
        x_ref, g_ref, w_ref, mu_ref, o_ref, xs_ref, last_ref = refs
    i = pl.program_id(0)
    j = pl.program_id(1)

    @pl.when(j == 0)
    def _():
        xs_ref[...] = _rms_rows(x_ref[...], g_ref[...]).astype(BF16)
        if sample:
            ps_ref[...] = prev_ref[...].astype(BF16)

    y = _mm(xs_ref[...], w_ref[...])
    if sample:
        shifted = _mm(ps_ref[...], w_ref[...])
    else:
        @pl.when(i == 0)
        def _():
            last_ref[j] = jnp.zeros((8, y.shape[1]), F32)

        carried = last_ref[j][7:8, :]
        prev_row = jnp.where(i % tiles_per_seq == 0, 0.0, carried)
        row = lax.broadcasted_iota(jnp.int32, y.shape, 0)
        shifted = jnp.where(row == 0, prev_row, pltpu.roll(y, 1, 0))
        last_ref[j] = y[tm - 8:, :]
    o_ref[...] = y + mu_ref[...] * (shifted - y)


def _proj_call(x, g, w, mu, prev, seq_len):
    m = x.shape[0]
    sample = prev is not None
    tm = min(512, m if sample else seq_len)
    nj = N_PROJ // PROJ_TN
    in_specs = [
        pl.BlockSpec((tm, D_MODEL), lambda i, j: (i, 0)),
        pl.BlockSpec((1, D_MODEL), lambda i, j: (0, 0)),
        pl.BlockSpec((D_MODEL, PROJ_TN), lambda i, j: (0, j)),
        pl.BlockSpec((1, PROJ_TN), lambda i, j: (0, j)),
    ]
    args = [x, g, w, mu]
    scratch = [pltpu.VMEM((tm, D_MODEL), BF16)]
    if sample:
        in_specs.append(pl.BlockSpec((tm, D_MODEL), lambda i, j: (i, 0)))
        args.append(prev)
        scratch.append(pltpu.VMEM((tm, D_MODEL), BF16))
    else:
        scratch.append(pltpu.VMEM((nj, 8, PROJ_TN), F32))
    return pl.pallas_call(
        functools.partial(_proj_kernel, sample=sample, tiles_per_seq=max(seq_len // tm, 1), tm=tm),
        grid=(m // tm, nj),
        in_specs=in_specs,
        out_specs=pl.BlockSpec((tm, PROJ_TN), lambda i, j: (i, j)),
        out_shape=jax.ShapeDtypeStruct((m, N_PROJ), F32),
        scratch_shapes=scratch,
        compiler_params=_cparams(("arbitrary", "arbitrary")),
        name="proj_sample" if sample else "proj_prompt",
    )(*args)


def _rwkv_kernel(*refs, C, PP, n_valid, has_s0, has_vres):
    it = iter(refs)
    r_ref, k_ref, v_ref, lo_ref = next(it), next(it), next(it), next(it)
    vlo_ref = vf_ref = vw2_ref = s0_ref = None
    if has_vres:
        vlo_ref, vf_ref = next(it), next(it)
    if has_s0:
        s0_ref = next(it)
    w2a_ref = next(it)
    if has_vres:
        vw2_ref = next(it)
    vec_ref = next(it)
    o_ref, so_ref, s_scr = next(it), next(it), next(it)

    c = pl.program_id(2)
    H = RWKV_HEAD_DIM

    @pl.when(c == 0)
    def _():
        s_scr[...] = jnp.zeros_like(s_scr)
        if has_s0:
            for p in range(PP):
                s_scr[p, 0:H, 0:H] = s0_ref[0, 2 * p]
                s_scr[p, H:2 * H, H:2 * H] = s0_ref[0, 2 * p + 1]

    lane = lax.broadcasted_iota(jnp.int32, (C, LANES), 1)
    head0 = lane < H
    trow = lax.broadcasted_iota(jnp.int32, (C, LANES), 0)
    ri = lax.broadcasted_iota(jnp.int32, (2 * C, 2 * C), 0)
    ci = lax.broadcasted_iota(jnp.int32, (2 * C, 2 * C), 1)
    same_head = (ri // C) == (ci // C)
    strict = same_head & ((ri % C) > (ci % C))
    incl = same_head & ((ri % C) >= (ci % C))
    eye = (ri == ci).astype(F32)

    def stack(x):
        return jnp.concatenate([jnp.where(head0, x, 0.0), jnp.where(head0, 0.0, x)], axis=0)

    def head_sum(x):
        s0 = jnp.sum(jnp.where(head0, x, 0.0), axis=-1, keepdims=True)
        s1 = jnp.sum(jnp.where(head0, 0.0, x), axis=-1, keepdims=True)
        return jnp.where(head0, s0, s1)

    lo = lo_ref[0]
    w_in = jnp.where(head0, jnp.tanh(lo), 0.0)
    a_in = jnp.where(head0, 0.0, lo)
    n_sq = C.bit_length() - 2

    pairs = range(PP)
    sls = [slice(p * LANES, (p + 1) * LANES) for p in pairs]

    def each(f, *lists):
        return [f(*xs) for xs in zip(*lists)]

    vec = lambda row: [vec_ref[row:row + 1, sl] for sl in sls]
    w0, a0, k_k, k_a, r_k, lnx_g, lnx_b, v0 = (vec(i) for i in range(8))
    r = [r_ref[0, :, sl] for sl in sls]
    k = [k_ref[0, :, sl] for sl in sls]
    v = [v_ref[0, :, sl] for sl in sls]
    w2a = [w2a_ref[:, sl] for sl in sls]

    wl = each(lambda w: _mm(w_in, w), w2a)
    al_ = each(lambda w: _mm(a_in, w), w2a)
    if has_vres:
        vlo = vlo_ref[0]
        vg = [_mm(vlo, vw2_ref[:, sl]) for sl in sls]
        v = [vp + (vf_ref[0, :, sl] - vp) * _sigmoid(z + g)
             for vp, sl, z, g in zip(v, sls, v0, vg)]

    def decay_log(w0p, wlp):
        ld = -jnp.exp(-_softplus(-(w0p + wlp)) - 0.5)
        if n_valid < C:
            ld = jnp.where(trow < n_valid, ld, 0.0)
        return ld

    ld = each(decay_log, w0, wl)
    a = each(lambda z, y: _sigmoid(z + y), a0, al_)

    def unit_kk(kp, kkp):
        kk = kp * kkp
        return kk / jnp.maximum(jnp.sqrt(head_sum(kk * kk)), 1e-12)

    kk = each(unit_kk, k, k_k)
    k = each(lambda kp, ap, kap: kp * (1.0 + (ap - 1.0) * kap), k, a, k_a)
    b = each(_cumsum_rows, ld)
    e_b = each(jnp.exp, b)
    e_nb = each(lambda x: jnp.exp(-x), b)
    al = each(lambda kkp, bp, ldp: stack(-kkp * jnp.exp(bp - ldp)), kk, b, ld)
    be = each(lambda ap, kkp, e: stack(ap * kkp * e), a, kk, e_nb)
    kt = each(lambda kp, e: stack(kp * e), k, e_nb)
    rt = each(lambda rp, e: stack(rp * e), r, e_b)
    vs = each(stack, v)
    S = [s_scr[p] for p in pairs]

    a_ak = each(lambda x, y: jnp.where(strict, _mm_nt(x, y), 0.0), al, kt)
    a_rb = each(lambda x, y: jnp.where(incl, _mm_nt(x, y), 0.0), rt, be)
    a_rk = each(lambda x, y: jnp.where(incl, _mm_nt(x, y), 0.0), rt, kt)
    cm = each(lambda x, s, m, y: _mm_nt(x, s) + _mm(m, y), al, S, a_ak, vs)
    if n_valid > 1:
        pw = each(lambda x, y: jnp.where(strict, _mm_nt(x, y), 0.0), al, be)
        tinv = each(lambda x: eye + x, pw)
        for _ in range(n_sq):
            pw = each(lambda x: _mm(x, x), pw)
            tinv = each(lambda t, x: t + _mm(t, x), tinv, pw)
        cm = each(_mm, tinv, cm)
    o2 = each(lambda x, s, m1, c1, m2, y: _mm_nt(x, s) + _mm(m1, c1) + _mm(m2, y),
              rt, S, a_rb, cm, a_rk, vs)
    s_new = each(lambda s, c1, x, y, z, e: (s + _mm_tn(c1, x) + _mm_tn(y, z)) * e[C - 1:C, :],
                 S, cm, be, vs, kt, e_b)
    for p in pairs:
        s_scr[p] = s_new[p]

    for p in pairs:
        o = o2[p][:C] + o2[p][C:]
        mean = head_sum(o) * (1.0 / H)
        d = o - mean
        var = head_sum(d * d) * (1.0 / H)
        o = d * lax.rsqrt(var + RWKV_GN_EPS) * lnx_g[p] + lnx_b[p]
        o_ref[0, :, sls[p]] = o + head_sum(r[p] * k[p] * r_k[p]) * v[p]

    @pl.when(c == pl.num_programs(2) - 1)
    def _():
        for p in range(PP):
            so_ref[0, 2 * p] = s_scr[p, 0:H, 0:H]
            so_ref[0, 2 * p + 1] = s_scr[p, H:2 * H, H:2 * H]


def _rwkv_call(P, Pfirst, s0, w2a, vw2, vecs, *, C, n_valid, PP):
    B, T, _ = P.shape
    L = LANES * PP
    has_vres = Pfirst is not None
    has_s0 = s0 is not None

    def col(off, width):
        base = off // width
        return lambda b, p, c: (b, c, base + p)

    def fixed(off):
        base = off // LANES
        return lambda b, p, c: (b, c, base)

    in_specs = [pl.BlockSpec((1, C, L), col(COL_R, L)),
                pl.BlockSpec((1, C, L), col(COL_K, L)),
                pl.BlockSpec((1, C, L), col(COL_V, L)),
                pl.BlockSpec((1, C, LANES), fixed(COL_LO))]
    args = [P, P, P, P]
    if has_vres:
        in_specs += [pl.BlockSpec((1, C, LANES), fixed(COL_VLO)),
                     pl.BlockSpec((1, C, L), col(COL_V, L))]
        args += [P, Pfirst]
    if has_s0:
        in_specs.append(pl.BlockSpec((1, HEADS_PER_PAIR * PP, RWKV_HEAD_DIM, RWKV_HEAD_DIM),
                                     lambda b, p, c: (b, p, 0, 0)))
        args.append(s0)
    in_specs.append(pl.BlockSpec((LANES, L), lambda b, p, c: (0, p)))
    args.append(w2a)
    if has_vres:
        in_specs.append(pl.BlockSpec((LANES, L), lambda b, p, c: (0, p)))
        args.append(vw2)
    in_specs.append(pl.BlockSpec((8, L), lambda b, p, c: (0, p)))
    args.append(vecs)
    return pl.pallas_call(
        functools.partial(_rwkv_kernel, C=C, PP=PP, n_valid=n_valid, has_s0=has_s0, has_vres=has_vres),
        grid=(B, N_PAIRS // PP, T // C),
        in_specs=in_specs,
        out_specs=[pl.BlockSpec((1, C, L), lambda b, p, c: (b, c, p)),
                   pl.BlockSpec((1, HEADS_PER_PAIR * PP, RWKV_HEAD_DIM, RWKV_HEAD_DIM),
                                lambda b, p, c: (b, p, 0, 0))],
        out_shape=[jax.ShapeDtypeStruct((B, T, D_MODEL), F32),
                   jax.ShapeDtypeStruct((B, RWKV_HEADS, RWKV_HEAD_DIM, RWKV_HEAD_DIM), F32)],
        scratch_shapes=[pltpu.VMEM((PP, LANES, LANES), F32)],
        compiler_params=_cparams(("parallel", "parallel", "arbitrary")),
        name="rwkv7_chunk",
    )(*args)


def _gla_kernel(*refs, C, n_valid, has_s0):
    if has_s0:
        q_ref, k_ref, v_ref, glo_ref, s0_ref, ga2_ref, gab_ref, g_ref, o_ref, so_ref, s_scr = refs
    else:
        q_ref, k_ref, v_ref, glo_ref, ga2_ref, gab_ref, g_ref, o_ref, so_ref, s_scr = refs
    c = pl.program_id(1)

    @pl.when(c == 0)
    def _():
        if has_s0:
            for h in range(GLA_HEADS):
                s_scr[h] = s0_ref[0, h].T
        else:
            s_scr[...] = jnp.zeros_like(s_scr)

    glo = glo_ref[0]
    ri = lax.broadcasted_iota(jnp.int32, (C, C), 0)
    ci = lax.broadcasted_iota(jnp.int32, (C, C), 1)
    causal = ri >= ci
    trow = lax.broadcasted_iota(jnp.int32, (C, GLA_DK), 0)
    heads = range(GLA_HEADS)
    ks = [slice(h * GLA_DK, (h + 1) * GLA_DK) for h in heads]
    vsl = [slice(h * GLA_DV, (h + 1) * GLA_DV) for h in heads]
    v = [v_ref[0, :, s] for s in vsl]
    k = [k_ref[0, :, s] for s in ks]
    gate = [_mm(glo, ga2_ref[:, s]) for s in ks]

    def cum_log_decay(x, s):
        la = -_softplus(-(x + gab_ref[:, s])) * (1.0 / GLA_GATE_NORMALIZER)
        if n_valid < C:
            la = jnp.where(trow < n_valid, la, 0.0)
        return _cumsum_rows(la)

    b = [cum_log_decay(x, s) for x, s in zip(gate, ks)]
    qd = [q_ref[0, :, s] * (GLA_DK ** -0.5) * jnp.exp(bh) for s, bh in zip(ks, b)]
    kd = [kh * jnp.exp(-bh) for kh, bh in zip(k, b)]
    kl = [kh * jnp.exp(bh[C - 1:C, :] - bh) for kh, bh in zip(k, b)]
    St = [s_scr[h] for h in heads]
    att = [jnp.where(causal, _mm_nt(x, y), 0.0) for x, y in zip(qd, kd)]
    o_s = [_mm_nt(x, s) for x, s in zip(qd, St)]
    upd = [_mm_tn(x, y) for x, y in zip(v, kl)]
    o = [_mm(x, y) + z for x, y, z in zip(att, v, o_s)]
    for h in heads:
        s_scr[h] = St[h] * jnp.exp(b[h][C - 1:C, :]) + upd[h]
        o_ref[0, :, vsl[h]] = (o[h] * lax.rsqrt(jnp.mean(o[h] * o[h], axis=-1, keepdims=True)
                                                + GLA_NORM_EPS) * g_ref[...])

    @pl.when(c == pl.num_programs(1) - 1)
    def _():
        for h in range(GLA_HEADS):
            so_ref[0, h] = s_scr[h].T


def _gla_call(P, s0, ga2, gab, g, *, C, n_valid):
    B, T, _ = P.shape
    has_s0 = s0 is not None
    kw = GLA_HEADS * GLA_DK
    vw = GLA_HEADS * GLA_DV
    state_spec = pl.BlockSpec((1, GLA_HEADS, GLA_DK, GLA_DV), lambda b, c: (b, 0, 0, 0))
    in_specs = [pl.BlockSpec((1, C, kw), lambda b, c: (b, c, COL_GQ // kw)),
                pl.BlockSpec((1, C, kw), lambda b, c: (b, c, COL_GK // kw)),
                pl.BlockSpec((1, C, vw), lambda b, c: (b, c, COL_GV // vw)),
                pl.BlockSpec((1, C, LANES), lambda b, c: (b, c, COL_GLO // LANES))]
    args = [P, P, P, P]
    if has_s0:
        in_specs.append(state_spec)
        args.append(s0)
    in_specs += [pl.BlockSpec((LANES, kw), lambda b, c: (0, 0)),
                 pl.BlockSpec((1, kw), lambda b, c: (0, 0)),
                 pl.BlockSpec((1, GLA_DV), lambda b, c: (0, 0))]
    args += [ga2, gab, g]
    return pl.pallas_call(
        functools.partial(_gla_kernel, C=C, n_valid=n_valid, has_s0=has_s0),
        grid=(B, T // C),
        in_specs=in_specs,
        out_specs=[pl.BlockSpec((1, C, vw), lambda b, c: (b, c, 0)), state_spec],
        out_shape=[jax.ShapeDtypeStruct((B, T, vw), F32),
                   jax.ShapeDtypeStruct((B, GLA_HEADS, GLA_DK, GLA_DV), F32)],
        scratch_shapes=[pltpu.VMEM((GLA_HEADS, GLA_DV, GLA_DK), F32)],
        compiler_params=_cparams(("parallel", "arbitrary")),
        name="gla_chunk",
    )(*args)


def _wo_kernel(ga_ref, gb_ref, ro_ref, go_ref, h_ref, w_ref, o_ref):
    merged = _sigmoid(ga_ref[...]) * ro_ref[...] + _sigmoid(gb_ref[...]) * go_ref[...]
    o_ref[...] = h_ref[...] + _mm(merged.astype(BF16), w_ref[...])


def _wo_call(P, ro, go, h, w):
    m = h.shape[0]
    tm = min(512, m)
    row = lambda i: (i, 0)
    return pl.pallas_call(
        _wo_kernel,
        grid=(m // tm,),
        in_specs=[pl.BlockSpec((tm, D_MODEL), lambda i: (i, COL_GATE_A // D_MODEL)),
                  pl.BlockSpec((tm, D_MODEL), lambda i: (i, COL_GATE_B // D_MODEL)),
                  pl.BlockSpec((tm, D_MODEL), row),
                  pl.BlockSpec((tm, D_MODEL), row),
                  pl.BlockSpec((tm, D_MODEL), row),
                  pl.BlockSpec((D_MODEL, D_MODEL), lambda i: (0, 0))],
        out_specs=pl.BlockSpec((tm, D_MODEL), row),
        out_shape=jax.ShapeDtypeStruct((m, D_MODEL), F32),
        compiler_params=_cparams(("parallel",)),
        name="merge_wo",
    )(P, P, ro, go, h, w)


FFN_TN = FFN_HIDDEN // 2


def _ffn_gu_kernel(h_ref, g_ref, wg_ref, wu_ref, o_ref, xs_ref):
    @pl.when(pl.program_id(1) == 0)
    def _():
        xs_ref[...] = _rms_rows(h_ref[...], g_ref[...]).astype(BF16)

    xs = xs_ref[...]
    gate = _mm(xs, wg_ref[...])
    up = _mm(xs, wu_ref[...])
    o_ref[...] = (gate * _sigmoid(gate) * up).astype(BF16)


def _ffn_gu_call(h, g, w_gu):
    m = h.shape[0]
    tm = min(512, m)
    nj = FFN_HIDDEN // FFN_TN
    return pl.pallas_call(
        _ffn_gu_kernel,
        grid=(m // tm, nj),
        in_specs=[pl.BlockSpec((tm, D_MODEL), lambda i, j: (i, 0)),
                  pl.BlockSpec((1, D_MODEL), lambda i, j: (0, 0)),
                  pl.BlockSpec((D_MODEL, FFN_TN), lambda i, j: (0, j)),
                  pl.BlockSpec((D_MODEL, FFN_TN), lambda i, j: (0, j + nj))],
        out_specs=pl.BlockSpec((tm, FFN_TN), lambda i, j: (i, j)),
        out_shape=jax.ShapeDtypeStruct((m, FFN_HIDDEN), BF16),
        scratch_shapes=[pltpu.VMEM((tm, D_MODEL), BF16)],
        compiler_params=_cparams(("parallel", "arbitrary")),
        name="ffn_gate_up",
    )(h, g, w_gu, w_gu)


def _ffn_down_kernel(a_ref, h_ref, w_ref, gf_ref, o_ref, *, final_norm):
    out = h_ref[...] + _mm(a_ref[...], w_ref[...])
    if final_norm:
        out = _rms_rows(out, gf_ref[...])
    o_ref[...] = out


def _ffn_down_call(act, h, w, gf, final_norm):
    m = h.shape[0]
    tm = min(512, m)
    row = lambda i: (i, 0)
    return pl.pallas_call(
        functools.partial(_ffn_down_kernel, final_norm=final_norm),
        grid=(m // tm,),
        in_specs=[pl.BlockSpec((tm, FFN_HIDDEN), row),
                  pl.BlockSpec((tm, D_MODEL), row),
                  pl.BlockSpec((FFN_HIDDEN, D_MODEL), lambda i: (0, 0)),
                  pl.BlockSpec((1, D_MODEL), lambda i: (0, 0))],
        out_specs=pl.BlockSpec((tm, D_MODEL), row),
        out_shape=jax.ShapeDtypeStruct((m, D_MODEL), F32),
        compiler_params=_cparams(("parallel",)),
        name="ffn_down",
    )(act, h, w, gf)


def _rms_kernel(x_ref, g_ref, o_ref):
    o_ref[...] = _rms_rows(x_ref[...], g_ref[...])


def _rms_call(x, g):
    return pl.pallas_call(
        _rms_kernel,
        out_shape=jax.ShapeDtypeStruct(x.shape, F32),
        name="rmsnorm_rows",
    )(x, g)


def _pad_cols(x, n):
    return jnp.pad(x, ((0, 0), (0, n - x.shape[1])))


def _pad_rows(x, n):
    return jnp.pad(x, ((0, n - x.shape[0]), (0, 0)))


def _layer_weights(l, w_in, mu_shift, vres_w1, vres_mu, p):
    W = w_in[l]
    mu = mu_shift[l]
    sizes = (1024, 1024, 1024, 64, 64, 512, 512, 1024, 16, 1024, 1024)
    offs = [0]
    for s in sizes:
        offs.append(offs[-1] + s)
    (r, k, v, wlo, alo, gq, gk, gv, glo, ga, gb) = [W[:, offs[i]:offs[i + 1]] for i in range(len(sizes))]
    zeros_mu = lambda n: jnp.zeros((n,), F32)
    if l > 0:
        vlo_w = _pad_cols(vres_w1[l - 1], LANES)
        vlo_mu = jnp.pad(vres_mu[l - 1], (0, LANES - LORA_VRES))
    else:
        vlo_w = jnp.zeros((D_MODEL, LANES), F32)
        vlo_mu = zeros_mu(LANES)
    w_cat = jnp.concatenate([r, k, v, ga, gb, gv, gq, gk, wlo, alo, vlo_w, _pad_cols(glo, LANES),
                             jnp.zeros((D_MODEL, LANES), F32)], axis=1).astype(BF16)
    mu_cat = jnp.concatenate([mu[:3072], zeros_mu(2048 + 2048), mu[3072:3200], vlo_mu,
                              zeros_mu(2 * LANES)])[None, :]
    lw = dict(w_cat=w_cat, mu_cat=mu_cat)
    lw['w2a'] = jnp.concatenate([p['rwkv_w2'][l], p['rwkv_a2'][l]], axis=0)
    lw['vecs'] = jnp.stack([p['rwkv_w0'][l], p['rwkv_a0'][l], p['rwkv_k_k'][l], p['rwkv_k_a'][l],
                            p['rwkv_r_k'][l].reshape(-1), p['rwkv_lnx_g'][l], p['rwkv_lnx_b'][l],
                            p['vres_v0'][l - 1] if l > 0 else zeros_mu(D_MODEL)], axis=0)
    lw['vw2'] = _pad_rows(p['vres_w2'][l - 1], LANES) if l > 0 else None
    lw['ga2'] = _pad_rows(p['gla_a2'][l], LANES)
    lw['gab'] = p['gla_ab'][l][None, :]
    lw['gla_g'] = p['gla_norm_g'][l][None, :]
    lw['w_o'] = p['w_o'][l].astype(BF16)
    lw['w_gu'] = p['ffn_w_gu'][l].astype(BF16)
    lw['w_down'] = p['ffn_w_down'][l].astype(BF16)
    lw['g_mix'] = p['norm_mix'][l][None, :]
    lw['g_ffn'] = p['norm_ffn'][l][None, :]
    return lw


def _run_group(x, shift_state, wkv_state, gla_state, weights, g_final, pairs_per_step):
    B, T, _ = x.shape
    fresh = shift_state is None
    C = CHUNK if fresh else SAMPLE_CHUNK
    h = x.reshape(B * T, D_MODEL)
    new_shift, new_wkv, new_gla = [], [], []
    P_first = None
    depth = len(weights)
    for l, lw in enumerate(weights):
        new_shift.append(_rms_call(h.reshape(B, T, D_MODEL)[:, -1], lw['g_mix']))
        if fresh:
            P = _proj_call(h, lw['g_mix'], lw['w_cat'], lw['mu_cat'], None, T).reshape(B, T, N_PROJ)
        else:
            P = _proj_call(h, lw['g_mix'], lw['w_cat'], lw['mu_cat'], shift_state[l], T)
            P = jnp.pad(P[:, None, :], ((0, 0), (0, C - T), (0, 0)))
        Tp = P.shape[1]
        ro, s_wkv = _rwkv_call(P, P_first if l > 0 else None, None if fresh else wkv_state[l],
                               lw['w2a'], lw['vw2'], lw['vecs'], C=C, n_valid=min(T, C), PP=pairs_per_step)
        go, s_gla = _gla_call(P, None if fresh else gla_state[l], lw['ga2'], lw['gab'], lw['gla_g'],
                              C=C, n_valid=min(T, C))
        if l == 0:
            P_first = P
        if Tp != T:
            ro, go = ro[:, :T], go[:, :T]
            P2 = P[:, :T].reshape(B * T, N_PROJ)
        else:
            P2 = P.reshape(B * T, N_PROJ)
        h = _wo_call(P2, ro.reshape(B * T, D_MODEL), go.reshape(B * T, D_MODEL), h, lw['w_o'])
        act = _ffn_gu_call(h, lw['g_ffn'], lw['w_gu'])
        h = _ffn_down_call(act, h, lw['w_down'], g_final, final_norm=(l == depth - 1))
        new_wkv.append(s_wkv)
        new_gla.append(s_gla)
    return (h.reshape(B, T, D_MODEL), jnp.stack(new_shift), jnp.stack(new_wkv), jnp.stack(new_gla))


def kernel(x_prompt, x_sample, state_shift, state_wkv, state_gla, norm_mix, w_in, mu_shift, rwkv_w0, rwkv_w2, rwkv_a0, rwkv_a2, rwkv_k_k, rwkv_k_a, rwkv_r_k, rwkv_lnx_g, rwkv_lnx_b, vres_w1, vres_mu, vres_w2, vres_v0, gla_a2, gla_ab, gla_norm_g, w_o, norm_ffn, ffn_w_gu, ffn_w_down, norm_final):
    p = dict(norm_mix=norm_mix, rwkv_w0=rwkv_w0, rwkv_w2=rwkv_w2, rwkv_a0=rwkv_a0, rwkv_a2=rwkv_a2,
             rwkv_k_k=rwkv_k_k, rwkv_k_a=rwkv_k_a, rwkv_r_k=rwkv_r_k, rwkv_lnx_g=rwkv_lnx_g,
             rwkv_lnx_b=rwkv_lnx_b, vres_w2=vres_w2, vres_v0=vres_v0, gla_a2=gla_a2, gla_ab=gla_ab,
             gla_norm_g=gla_norm_g, w_o=w_o, norm_ffn=norm_ffn, ffn_w_gu=ffn_w_gu, ffn_w_down=ffn_w_down)
    depth = w_in.shape[0]
    weights = [_layer_weights(l, w_in, mu_shift, vres_w1, vres_mu, p) for l in range(depth)]
    g_final = norm_final[None, :]
    y_p, shift_p, wkv_p, gla_p = _run_group(x_prompt, None, None, None, weights, g_final, N_PAIRS)
    y_s, shift_s, wkv_s, gla_s = _run_group(x_sample, state_shift, state_wkv, state_gla, weights, g_final, N_PAIRS)
    return (y_p, y_s, shift_p, wkv_p, gla_p, shift_s, wkv_s, gla_s)
```

```python
import functools

import jax
import jax.numpy as jnp
from jax import lax
from jax.experimental import pallas as pl
from jax.experimental.pallas import tpu as pltpu

F32 = jnp.float32
BF16 = jnp.bfloat16

LANES = 128
D_MODEL = 1024
RWKV_HEAD_DIM = 64
RWKV_HEADS = D_MODEL // RWKV_HEAD_DIM
HEADS_PER_PAIR = LANES // RWKV_HEAD_DIM
N_PAIRS = RWKV_HEADS // HEADS_PER_PAIR
LORA_DECAY = 64
LORA_ICLR = 64
LORA_VRES = 32
RWKV_GN_EPS = 64e-5
GLA_HEADS = 4
GLA_DK = 128
GLA_DV = 256
GLA_GATE_RANK = 16
GLA_GATE_NORMALIZER = 16.0
GLA_NORM_EPS = 1e-5
FFN_HIDDEN = 2816
NORM_EPS = 1e-6
CHUNK = 64
SAMPLE_CHUNK = 8

COL_R, COL_K, COL_V = 0, 1024, 2048
COL_GATE_A, COL_GATE_B = 3072, 4096
COL_GV, COL_GQ, COL_GK = 5120, 6144, 6656
COL_LO, COL_VLO, COL_GLO = 7168, 7296, 7424
N_PROJ = 7680
PROJ_TN = 1536
VMEM_LIMIT = 48 * 1024 * 1024
RWKV_ROWS_PER_STEP = 2
GLA_ROWS_PER_STEP = 4


def _cparams(sem):
    return pltpu.CompilerParams(dimension_semantics=sem, vmem_limit_bytes=VMEM_LIMIT)


def _mm(a, b):
    return jnp.dot(a, b, preferred_element_type=F32)


def _mm_nt(a, b):
    return lax.dot_general(a, b, (((1,), (1,)), ((), ())), preferred_element_type=F32)


def _mm_tn(a, b):
    return lax.dot_general(a, b, (((0,), (0,)), ((), ())), preferred_element_type=F32)


def _sigmoid(x):
    return 1.0 / (1.0 + jnp.exp(-x))


def _softplus(x):
    return jnp.maximum(x, 0.0) + jnp.log1p(jnp.exp(-jnp.abs(x)))


def _cumsum_rows(x):
    n = x.shape[0]
    row = lax.broadcasted_iota(jnp.int32, x.shape, 0)
    s = 1
    while s < n:
        x = x + jnp.where(row >= s, pltpu.roll(x, s, 0), 0.0)
        s *= 2
    return x


def _rms_rows(x, g):
    ms = jnp.mean(x * x, axis=-1, keepdims=True)
    return x * lax.rsqrt(ms + NORM_EPS) * g


def _proj_kernel(*refs, sample, tiles_per_seq, tm):
    if sample:
        x_ref, g_ref, w_ref, mu_ref, prev_ref, o_ref, xs_ref, ps_ref = refs
    else:
        x_ref, g_ref, w_ref, mu_ref, o_ref, xs_ref, last_ref = refs
    i = pl.program_id(0)
    j = pl.program_id(1)

    @pl.when(j == 0)
    def _():
        xs_ref[...] = _rms_rows(x_ref[...], g_ref[...]).astype(BF16)
        if sample:
            ps_ref[...] = prev_ref[...].astype(BF16)

    y = _mm(xs_ref[...], w_ref[...])
    if sample:
        shifted = _mm(ps_ref[...], w_ref[...])
    else:
        @pl.when(i == 0)
        def _():
            last_ref[j] = jnp.zeros((8, y.shape[1]), F32)

        carried = last_ref[j][7:8, :]
        prev_row = jnp.where(i % tiles_per_seq == 0, 0.0, carried)
        row = lax.broadcasted_iota(jnp.int32, y.shape, 0)
        shifted = jnp.where(row == 0, prev_row, pltpu.roll(y, 1, 0))
        last_ref[j] = y[tm - 8:, :]
    o_ref[...] = y + mu_ref[...] * (shifted - y)


def _proj_call(x, g, w, mu, prev, seq_len):
    m = x.shape[0]
    sample = prev is not None
    tm = min(512, m if sample else seq_len)
    nj = N_PROJ // PROJ_TN
    in_specs = [
        pl.BlockSpec((tm, D_MODEL), lambda i, j: (i, 0)),
        pl.BlockSpec((1, D_MODEL), lambda i, j: (0, 0)),
        pl.BlockSpec((D_MODEL, PROJ_TN), lambda i, j: (0, j)),
        pl.BlockSpec((1, PROJ_TN), lambda i, j: (0, j)),
    ]
    args = [x, g, w, mu]
    scratch = [pltpu.VMEM((tm, D_MODEL), BF16)]
    if sample:
        in_specs.append(pl.BlockSpec((tm, D_MODEL), lambda i, j: (i, 0)))
        args.append(prev)
        scratch.append(pltpu.VMEM((tm, D_MODEL), BF16))
    else:
        scratch.append(pltpu.VMEM((nj, 8, PROJ_TN), F32))
    return pl.pallas_call(
        functools.partial(_proj_kernel, sample=sample, tiles_per_seq=max(seq_len // tm, 1), tm=tm),
        grid=(m // tm, nj),
        in_specs=in_specs,
        out_specs=pl.BlockSpec((tm, PROJ_TN), lambda i, j: (i, j)),
        out_shape=jax.ShapeDtypeStruct((m, N_PROJ), F32),
        scratch_shapes=scratch,
        compiler_params=_cparams(("arbitrary", "arbitrary")),
        name="proj_sample" if sample else "proj_prompt",
    )(*args)


def _rwkv_kernel(*refs, C, BB, n_valid, has_s0, has_vres):
    it = iter(refs)
    r_ref, k_ref, v_ref, lo_ref = next(it), next(it), next(it), next(it)
    vlo_ref = vf_ref = vw2_ref = s0_ref = None
    if has_vres:
        vlo_ref, vf_ref = next(it), next(it)
    if has_s0:
        s0_ref = next(it)
    w2a_ref = next(it)
    if has_vres:
        vw2_ref = next(it)
    vec_ref = next(it)
    o_ref, so_ref, s_scr = next(it), next(it), next(it)

    c = pl.program_id(1)
    H = RWKV_HEAD_DIM
    chains = [(bi, p) for bi in range(BB) for p in range(N_PAIRS)]
    sls = [slice(p * LANES, (p + 1) * LANES) for _, p in chains]
    bis = [bi for bi, _ in chains]

    @pl.when(c == 0)
    def _():
        s_scr[...] = jnp.zeros_like(s_scr)
        if has_s0:
            for i, (bi, p) in enumerate(chains):
                s_scr[i, 0:H, 0:H] = s0_ref[bi, 2 * p]
                s_scr[i, H:2 * H, H:2 * H] = s0_ref[bi, 2 * p + 1]

    lane = lax.broadcasted_iota(jnp.int32, (C, LANES), 1)
    head0 = lane < H
    trow = lax.broadcasted_iota(jnp.int32, (C, LANES), 0)
    n_sq = C.bit_length() - 2
    fused = 2 * C == LANES

    def stack(x):
        return jnp.concatenate([jnp.where(head0, x, 0.0), jnp.where(head0, 0.0, x)], axis=0)

    def head_sum(x):
        s0 = jnp.sum(jnp.where(head0, x, 0.0), axis=-1, keepdims=True)
        s1 = jnp.sum(jnp.where(head0, 0.0, x), axis=-1, keepdims=True)
        return jnp.where(head0, s0, s1)

    def each(f, *lists):
        return [f(*xs) for xs in zip(*lists)]

    bf = lambda x: x.astype(BF16)
    vec = lambda row: [vec_ref[row:row + 1, sl] for sl in sls]
    w0, a0, k_k, k_a, r_k, lnx_g, lnx_b, v0 = (vec(i) for i in range(8))
    r = [r_ref[bi, :, sl] for bi, sl in zip(bis, sls)]
    k = [k_ref[bi, :, sl] for bi, sl in zip(bis, sls)]
    v = [v_ref[bi, :, sl] for bi, sl in zip(bis, sls)]

    los = [lo_ref[bi] for bi in range(BB)]
    w2a = w2a_ref[...]
    wl_rows = [_mm(jnp.where(head0, jnp.tanh(lo), 0.0), w2a) for lo in los]
    al_rows = [_mm(jnp.where(head0, 0.0, lo), w2a) for lo in los]
    wl = [wl_rows[bi][:, sl] for bi, sl in zip(bis, sls)]
    al_ = [al_rows[bi][:, sl] for bi, sl in zip(bis, sls)]
    if has_vres:
        vw2 = vw2_ref[...]
        vg_rows = [_mm(vlo_ref[bi], vw2) for bi in range(BB)]
        v = [vp + (vf_ref[bi, :, sl] - vp) * _sigmoid(z + vg_rows[bi][:, sl])
             for vp, bi, sl, z in zip(v, bis, sls, v0)]

    def decay_log(w0p, wlp):
        ld = -jnp.exp(-_softplus(-(w0p + wlp)) - 0.5)
        if n_valid < C:
            ld = jnp.where(trow < n_valid, ld, 0.0)
        return ld

    ld = each(decay_log, w0, wl)
    a = each(lambda z, y: _sigmoid(z + y), a0, al_)

    def unit_kk(kp, kkp):
        kk = kp * kkp
        return kk / jnp.maximum(jnp.sqrt(head_sum(kk * kk)), 1e-12)

    kk = each(unit_kk, k, k_k)
    k = each(lambda kp, ap, kap: kp * (1.0 + (ap - 1.0) * kap), k, a, k_a)
    b = each(_cumsum_rows, ld)
    e_b = each(jnp.exp, b)
    e_nb = each(lambda x: jnp.exp(-x), b)
    al = each(lambda kkp, bp, ldp: bf(stack(-kkp * jnp.exp(bp - ldp))), kk, b, ld)
    be = each(lambda ap, kkp, e: bf(stack(ap * kkp * e)), a, kk, e_nb)
    kt = each(lambda kp, e: bf(stack(kp * e)), k, e_nb)
    rt = each(lambda rp, e: bf(stack(rp * e)), r, e_b)
    vs = each(lambda x: bf(stack(x)), v)
    S32 = [s_scr[i] for i in range(len(chains))]

    if fused:
        ri = lax.broadcasted_iota(jnp.int32, (2 * LANES, 2 * LANES), 0)
        ci = lax.broadcasted_iota(jnp.int32, (2 * LANES, 2 * LANES), 1)
        rr, cc = ri % LANES, ci % LANES
        same_head = (rr // C) == (cc // C)
        tri = (rr % C) >= (cc % C) + (ri < LANES).astype(jnp.int32)
        mask4 = same_head & tri
        r1 = lax.broadcasted_iota(jnp.int32, (LANES, LANES), 0)
        c1 = lax.broadcasted_iota(jnp.int32, (LANES, LANES), 1)
        eye = (r1 == c1).astype(F32)
        cat0 = lambda x, y: jnp.concatenate([x, y], axis=0)
        cat1 = lambda x, y: jnp.concatenate([x, y], axis=1)

        ar = each(cat0, al, rt)
        bk = each(cat0, be, kt)
        amat = each(lambda x, y: jnp.where(mask4, _mm_nt(x, y), 0.0), ar, bk)
        sv = each(lambda s, y: cat0(bf(s.T), y), S32, vs)
        x = each(lambda l, m, rhs: _mm(cat1(l, bf(m[:, LANES:])), rhs), ar, amat, sv)
        tinv = each(lambda m: eye + m[:LANES, :LANES], amat)
        q = each(lambda m: bf(m[:LANES, :LANES]), amat)
        q = each(lambda z: bf(_mm(z, z)), q)
        for _ in range(n_sq - 1):
            y = each(lambda z, t: _mm(z, cat1(bf(t), z)), q, tinv)
            tinv = each(lambda t, z: t + z[:, :LANES], tinv, y)
            q = each(lambda z: bf(z[:, LANES:]), y)
        tinv = each(lambda t, z: t + _mm(z, bf(t)), tinv, q)
        cm = each(lambda t, z: bf(_mm(bf(t), bf(z[:LANES]))), tinv, x)
        o2 = each(lambda z, m, cc_: z[LANES:] + _mm(bf(m[LANES:, :LANES]), cc_), x, amat, cm)
        s_new = each(lambda s, cc_, y, rhs, e: (s + _mm_tn(cat0(cc_, y), rhs)) * e[C - 1:C, :],
                     S32, cm, vs, bk, e_b)
    else:
        ri = lax.broadcasted_iota(jnp.int32, (2 * C, 2 * C), 0)
        ci = lax.broadcasted_iota(jnp.int32, (2 * C, 2 * C), 1)
        same_head = (ri // C) == (ci // C)
        strict = same_head & ((ri % C) > (ci % C))
        incl = same_head & ((ri % C) >= (ci % C))
        eye = (ri == ci).astype(F32)
        S = each(bf, S32)
        a_ak = each(lambda x, y: bf(jnp.where(strict, _mm_nt(x, y), 0.0)), al, kt)
        a_rb = each(lambda x, y: bf(jnp.where(incl, _mm_nt(x, y), 0.0)), rt, be)
        a_rk = each(lambda x, y: bf(jnp.where(incl, _mm_nt(x, y), 0.0)), rt, kt)
        cm = each(lambda x, s, m, y: _mm_nt(x, s) + _mm(m, y), al, S, a_ak, vs)
        if n_valid > 1:
            pw = each(lambda x, y: jnp.where(strict, _mm_nt(x, y), 0.0), al, be)
            tinv = each(lambda x: eye + x, pw)
            pwb = each(bf, pw)
            for _ in range(n_sq):
                pwb = each(lambda x: bf(_mm(x, x)), pwb)
                tinv = each(lambda t, x: t + _mm(bf(t), x), tinv, pwb)
            cm = each(lambda t, x: _mm(bf(t), bf(x)), tinv, cm)
        cm = each(bf, cm)
        o2 = each(lambda x, s, m1, c1, m2, y: _mm_nt(x, s) + _mm(m1, c1) + _mm(m2, y),
                  rt, S, a_rb, cm, a_rk, vs)
        s_new = each(lambda s, c1, x, y, z, e: (s + _mm_tn(c1, x) + _mm_tn(y, z)) * e[C - 1:C, :],
                     S32, cm, be, vs, kt, e_b)

    for i in range(len(chains)):
        s_scr[i] = s_new[i]

    for i, (bi, sl) in enumerate(zip(bis, sls)):
        o = o2[i][:C] + o2[i][C:]
        mean = head_sum(o) * (1.0 / H)
        d = o - mean
        var = head_sum(d * d) * (1.0 / H)
        o = d * lax.rsqrt(var + RWKV_GN_EPS) * lnx_g[i] + lnx_b[i]
        o_ref[bi, :, sl] = o + head_sum(r[i] * k[i] * r_k[i]) * v[i]

    @pl.when(c == pl.num_programs(1) - 1)
    def _():
        for i, (bi, p) in enumerate(chains):
            so_ref[bi, 2 * p] = s_scr[i, 0:H, 0:H]
            so_ref[bi, 2 * p + 1] = s_scr[i, H:2 * H, H:2 * H]


def _rwkv_call(P, Pfirst, s0, w2a, vw2, vecs, *, C, n_valid, BB):
    B, T, _ = P.shape
    has_vres = Pfirst is not None
    has_s0 = s0 is not None

    def cols(off, width):
        idx = off // width
        return pl.BlockSpec((BB, C, width), lambda b, c: (b, c, idx))

    whole = lambda shape: pl.BlockSpec(shape, lambda b, c: (0,) * len(shape))
    state_spec = pl.BlockSpec((BB, RWKV_HEADS, RWKV_HEAD_DIM, RWKV_HEAD_DIM), lambda b, c: (b, 0, 0, 0))
    in_specs = [cols(COL_R, D_MODEL), cols(COL_K, D_MODEL), cols(COL_V, D_MODEL), cols(COL_LO, LANES)]
    args = [P, P, P, P]
    if has_vres:
        in_specs += [cols(COL_VLO, LANES), cols(COL_V, D_MODEL)]
        args += [P, Pfirst]
    if has_s0:
        in_specs.append(state_spec)
        args.append(s0)
    in_specs.append(whole((LANES, D_MODEL)))
    args.append(w2a)
    if has_vres:
        in_specs.append(whole((LANES, D_MODEL)))
        args.append(vw2)
    in_specs.append(whole((8, D_MODEL)))
    args.append(vecs)
    return pl.pallas_call(
        functools.partial(_rwkv_kernel, C=C, BB=BB, n_valid=n_valid, has_s0=has_s0, has_vres=has_vres),
        grid=(B // BB, T // C),
        in_specs=in_specs,
        out_specs=[pl.BlockSpec((BB, C, D_MODEL), lambda b, c: (b, c, 0)), state_spec],
        out_shape=[jax.ShapeDtypeStruct((B, T, D_MODEL), F32),
                   jax.ShapeDtypeStruct((B, RWKV_HEADS, RWKV_HEAD_DIM, RWKV_HEAD_DIM), F32)],
        scratch_shapes=[pltpu.VMEM((BB * N_PAIRS, LANES, LANES), F32)],
        compiler_params=_cparams(("parallel", "arbitrary")),
        name="rwkv7_chunk",
    )(*args)


def _gla_kernel(*refs, C, BB, n_valid, has_s0):
    if has_s0:
        q_ref, k_ref, v_ref, glo_ref, s0_ref, ga2_ref, gab_ref, g_ref, o_ref, so_ref, s_scr = refs
    else:
        q_ref, k_ref, v_ref, glo_ref, ga2_ref, gab_ref, g_ref, o_ref, so_ref, s_scr = refs
    c = pl.program_id(1)
    chains = [(bi, h) for bi in range(BB) for h in range(GLA_HEADS)]

    @pl.when(c == 0)
    def _():
        if has_s0:
            for i, (bi, h) in enumerate(chains):
                s_scr[i] = s0_ref[bi, h].T
        else:
            s_scr[...] = jnp.zeros_like(s_scr)

    ri = lax.broadcasted_iota(jnp.int32, (C, C), 0)
    ci = lax.broadcasted_iota(jnp.int32, (C, C), 1)
    causal = ri >= ci
    trow = lax.broadcasted_iota(jnp.int32, (C, GLA_DK), 0)
    bis = [bi for bi, _ in chains]
    ks = [slice(h * GLA_DK, (h + 1) * GLA_DK) for _, h in chains]
    vsl = [slice(h * GLA_DV, (h + 1) * GLA_DV) for _, h in chains]
    v = [v_ref[bi, :, s] for bi, s in zip(bis, vsl)]
    k = [k_ref[bi, :, s] for bi, s in zip(bis, ks)]
    ga2 = ga2_ref[...]
    gate_rows = [_mm(glo_ref[bi], ga2) for bi in range(BB)]

    def cum_log_decay(bi, s):
        la = -_softplus(-(gate_rows[bi][:, s] + gab_ref[:, s])) * (1.0 / GLA_GATE_NORMALIZER)
        if n_valid < C:
            la = jnp.where(trow < n_valid, la, 0.0)
        return _cumsum_rows(la)

    b = [cum_log_decay(bi, s) for bi, s in zip(bis, ks)]
    qd = [q_ref[bi, :, s] * (GLA_DK ** -0.5) * jnp.exp(bh) for bi, s, bh in zip(bis, ks, b)]
    kd = [kh * jnp.exp(-bh) for kh, bh in zip(k, b)]
    kl = [kh * jnp.exp(bh[C - 1:C, :] - bh) for kh, bh in zip(k, b)]
    St = [s_scr[i] for i in range(len(chains))]
    att = [jnp.where(causal, _mm_nt(x, y), 0.0) for x, y in zip(qd, kd)]
    o_s = [_mm_nt(x, s) for x, s in zip(qd, St)]
    upd = [_mm_tn(x, y) for x, y in zip(v, kl)]
    o = [_mm(x, y) + z for x, y, z in zip(att, v, o_s)]
    for i, (bi, s) in enumerate(zip(bis, vsl)):
        s_scr[i] = St[i] * jnp.exp(b[i][C - 1:C, :]) + upd[i]
        o_ref[bi, :, s] = (o[i] * lax.rsqrt(jnp.mean(o[i] * o[i], axis=-1, keepdims=True)
                                            + GLA_NORM_EPS) * g_ref[...])

    @pl.when(c == pl.num_programs(1) - 1)
    def _():
        for i, (bi, h) in enumerate(chains):
            so_ref[bi, h] = s_scr[i].T


def _gla_call(P, s0, ga2, gab, g, *, C, n_valid, BB):
    B, T, _ = P.shape
    has_s0 = s0 is not None
    kw = GLA_HEADS * GLA_DK
    vw = GLA_HEADS * GLA_DV
    state_spec = pl.BlockSpec((BB, GLA_HEADS, GLA_DK, GLA_DV), lambda b, c: (b, 0, 0, 0))
    in_specs = [pl.BlockSpec((BB, C, kw), lambda b, c: (b, c, COL_GQ // kw)),
                pl.BlockSpec((BB, C, kw), lambda b, c: (b, c, COL_GK // kw)),
                pl.BlockSpec((BB, C, vw), lambda b, c: (b, c, COL_GV // vw)),
                pl.BlockSpec((BB, C, LANES), lambda b, c: (b, c, COL_GLO // LANES))]
    args = [P, P, P, P]
    if has_s0:
        in_specs.append(state_spec)
        args.append(s0)
    in_specs += [pl.BlockSpec((LANES, kw), lambda b, c: (0, 0)),
                 pl.BlockSpec((1, kw), lambda b, c: (0, 0)),
                 pl.BlockSpec((1, GLA_DV), lambda b, c: (0, 0))]
    args += [ga2, gab, g]
    return pl.pallas_call(
        functools.partial(_gla_kernel, C=C, BB=BB, n_valid=n_valid, has_s0=has_s0),
        grid=(B // BB, T // C),
        in_specs=in_specs,
        out_specs=[pl.BlockSpec((BB, C, vw), lambda b, c: (b, c, 0)), state_spec],
        out_shape=[jax.ShapeDtypeStruct((B, T, vw), F32),
                   jax.ShapeDtypeStruct((B, GLA_HEADS, GLA_DK, GLA_DV), F32)],
        scratch_shapes=[pltpu.VMEM((BB * GLA_HEADS, GLA_DV, GLA_DK), F32)],
        compiler_params=_cparams(("parallel", "arbitrary")),
        name="gla_chunk",
    )(*args)


def _wo_kernel(ga_ref, gb_ref, ro_ref, go_ref, h_ref, w_ref, o_ref):
    merged = _sigmoid(ga_ref[...]) * ro_ref[...] + _sigmoid(gb_ref[...]) * go_ref[...]
    o_ref[...] = h_ref[...] + _mm(merged.astype(BF16), w_ref[...])


def _wo_call(P, ro, go, h, w):
    m = h.shape[0]
    tm = min(512, m)
    row = lambda i: (i, 0)
    return pl.pallas_call(
        _wo_kernel,
        grid=(m // tm,),
        in_specs=[pl.BlockSpec((tm, D_MODEL), lambda i: (i, COL_GATE_A // D_MODEL)),
                  pl.BlockSpec((tm, D_MODEL), lambda i: (i, COL_GATE_B // D_MODEL)),
                  pl.BlockSpec((tm, D_MODEL), row),
                  pl.BlockSpec((tm, D_MODEL), row),
                  pl.BlockSpec((tm, D_MODEL), row),
                  pl.BlockSpec((D_MODEL, D_MODEL), lambda i: (0, 0))],
        out_specs=pl.BlockSpec((tm, D_MODEL), row),
        out_shape=jax.ShapeDtypeStruct((m, D_MODEL), F32),
        compiler_params=_cparams(("parallel",)),
        name="merge_wo",
    )(P, P, ro, go, h, w)


FFN_TN = FFN_HIDDEN // 2


def _ffn_gu_kernel(h_ref, g_ref, wg_ref, wu_ref, o_ref, xs_ref):
    @pl.when(pl.program_id(1) == 0)
    def _():
        xs_ref[...] = _rms_rows(h_ref[...], g_ref[...]).astype(BF16)

    xs = xs_ref[...]
    gate = _mm(xs, wg_ref[...])
    up = _mm(xs, wu_ref[...])
    o_ref[...] = (gate * _sigmoid(gate) * up).astype(BF16)


def _ffn_gu_call(h, g, w_gu):
    m = h.shape[0]
    tm = min(512, m)
    nj = FFN_HIDDEN // FFN_TN
    return pl.pallas_call(
        _ffn_gu_kernel,
        grid=(m // tm, nj),
        in_specs=[pl.BlockSpec((tm, D_MODEL), lambda i, j: (i, 0)),
                  pl.BlockSpec((1, D_MODEL), lambda i, j: (0, 0)),
                  pl.BlockSpec((D_MODEL, FFN_TN), lambda i, j: (0, j)),
                  pl.BlockSpec((D_MODEL, FFN_TN), lambda i, j: (0, j + nj))],
        out_specs=pl.BlockSpec((tm, FFN_TN), lambda i, j: (i, j)),
        out_shape=jax.ShapeDtypeStruct((m, FFN_HIDDEN), BF16),
        scratch_shapes=[pltpu.VMEM((tm, D_MODEL), BF16)],
        compiler_params=_cparams(("parallel", "arbitrary")),
        name="ffn_gate_up",
    )(h, g, w_gu, w_gu)


def _ffn_down_kernel(a_ref, h_ref, w_ref, gf_ref, o_ref, *, final_norm):
    out = h_ref[...] + _mm(a_ref[...], w_ref[...])
    if final_norm:
        out = _rms_rows(out, gf_ref[...])
    o_ref[...] = out


def _ffn_down_call(act, h, w, gf, final_norm):
    m = h.shape[0]
    tm = min(512, m)
    row = lambda i: (i, 0)
    return pl.pallas_call(
        functools.partial(_ffn_down_kernel, final_norm=final_norm),
        grid=(m // tm,),
        in_specs=[pl.BlockSpec((tm, FFN_HIDDEN), row),
                  pl.BlockSpec((tm, D_MODEL), row),
                  pl.BlockSpec((FFN_HIDDEN, D_MODEL), lambda i: (0, 0)),
                  pl.BlockSpec((1, D_MODEL), lambda i: (0, 0))],
        out_specs=pl.BlockSpec((tm, D_MODEL), row),
        out_shape=jax.ShapeDtypeStruct((m, D_MODEL), F32),
        compiler_params=_cparams(("parallel",)),
        name="ffn_down",
    )(act, h, w, gf)


def _rms_kernel(x_ref, g_ref, o_ref):
    o_ref[...] = _rms_rows(x_ref[...], g_ref[...])


def _rms_call(x, g):
    return pl.pallas_call(
        _rms_kernel,
        out_shape=jax.ShapeDtypeStruct(x.shape, F32),
        name="rmsnorm_rows",
    )(x, g)


def _pad_cols(x, n):
    return jnp.pad(x, ((0, 0), (0, n - x.shape[1])))


def _pad_rows(x, n):
    return jnp.pad(x, ((0, n - x.shape[0]), (0, 0)))


def _layer_weights(l, w_in, mu_shift, vres_w1, vres_mu, p):
    W = w_in[l]
    mu = mu_shift[l]
    sizes = (1024, 1024, 1024, 64, 64, 512, 512, 1024, 16, 1024, 1024)
    offs = [0]
    for s in sizes:
        offs.append(offs[-1] + s)
    (r, k, v, wlo, alo, gq, gk, gv, glo, ga, gb) = [W[:, offs[i]:offs[i + 1]] for i in range(len(sizes))]
    zeros_mu = lambda n: jnp.zeros((n,), F32)
    if l > 0:
        vlo_w = _pad_cols(vres_w1[l - 1], LANES)
        vlo_mu = jnp.pad(vres_mu[l - 1], (0, LANES - LORA_VRES))
    else:
        vlo_w = jnp.zeros((D_MODEL, LANES), F32)
        vlo_mu = zeros_mu(LANES)
    w_cat = jnp.concatenate([r, k, v, ga, gb, gv, gq, gk, wlo, alo, vlo_w, _pad_cols(glo, LANES),
                             jnp.zeros((D_MODEL, LANES), F32)], axis=1).astype(BF16)
    mu_cat = jnp.concatenate([mu[:3072], zeros_mu(2048 + 2048), mu[3072:3200], vlo_mu,
                              zeros_mu(2 * LANES)])[None, :]
    lw = dict(w_cat=w_cat, mu_cat=mu_cat)
    lw['w2a'] = jnp.concatenate([p['rwkv_w2'][l], p['rwkv_a2'][l]], axis=0)
    lw['vecs'] = jnp.stack([p['rwkv_w0'][l], p['rwkv_a0'][l], p['rwkv_k_k'][l], p['rwkv_k_a'][l],
                            p['rwkv_r_k'][l].reshape(-1), p['rwkv_lnx_g'][l], p['rwkv_lnx_b'][l],
                            p['vres_v0'][l - 1] if l > 0 else zeros_mu(D_MODEL)], axis=0)
    lw['vw2'] = _pad_rows(p['vres_w2'][l - 1], LANES) if l > 0 else None
    lw['ga2'] = _pad_rows(p['gla_a2'][l], LANES)
    lw['gab'] = p['gla_ab'][l][None, :]
    lw['gla_g'] = p['gla_norm_g'][l][None, :]
    lw['w_o'] = p['w_o'][l].astype(BF16)
    lw['w_gu'] = p['ffn_w_gu'][l].astype(BF16)
    lw['w_down'] = p['ffn_w_down'][l].astype(BF16)
    lw['g_mix'] = p['norm_mix'][l][None, :]
    lw['g_ffn'] = p['norm_ffn'][l][None, :]
    return lw


def _rows_per_step(batch, want):
    while batch % want:
        want //= 2
    return want


def _run_group(x, shift_state, wkv_state, gla_state, weights, g_final):
    B, T, _ = x.shape
    fresh = shift_state is None
    C = CHUNK if fresh else SAMPLE_CHUNK
    h = x.reshape(B * T, D_MODEL)
    new_shift, new_wkv, new_gla = [], [], []
    P_first = None
    depth = len(weights)
    for l, lw in enumerate(weights):
        new_shift.append(_rms_call(h.reshape(B, T, D_MODEL)[:, -1], lw['g_mix']))
        if fresh:
            P = _proj_call(h, lw['g_mix'], lw['w_cat'], lw['mu_cat'], None, T).reshape(B, T, N_PROJ)
        else:
            P = _proj_call(h, lw['g_mix'], lw['w_cat'], lw['mu_cat'], shift_state[l], T)
            P = jnp.pad(P[:, None, :], ((0, 0), (0, C - T), (0, 0)))
        Tp = P.shape[1]
        ro, s_wkv = _rwkv_call(P, P_first if l > 0 else None, None if fresh else wkv_state[l],
                               lw['w2a'], lw['vw2'], lw['vecs'], C=C, n_valid=min(T, C),
                               BB=_rows_per_step(B, RWKV_ROWS_PER_STEP))
        go, s_gla = _gla_call(P, None if fresh else gla_state[l], lw['ga2'], lw['gab'], lw['gla_g'],
                              C=C, n_valid=min(T, C), BB=_rows_per_step(B, GLA_ROWS_PER_STEP))
        if l == 0:
            P_first = P
        if Tp != T:
            ro, go = ro[:, :T], go[:, :T]
            P2 = P[:, :T].reshape(B * T, N_PROJ)
        else:
            P2 = P.reshape(B * T, N_PROJ)
        h = _wo_call(P2, ro.reshape(B * T, D_MODEL), go.reshape(B * T, D_MODEL), h, lw['w_o'])
        act = _ffn_gu_call(h, lw['g_ffn'], lw['w_gu'])
        h = _ffn_down_call(act, h, lw['w_down'], g_final, final_norm=(l == depth - 1))
        new_wkv.append(s_wkv)
        new_gla.append(s_gla)
    return (h.reshape(B, T, D_MODEL), jnp.stack(new_shift), jnp.stack(new_wkv), jnp.stack(new_gla))


def kernel(x_prompt, x_sample, state_shift, state_wkv, state_gla, norm_mix, w_in, mu_shift, rwkv_w0, rwkv_w2, rwkv_a0, rwkv_a2, rwkv_k_k, rwkv_k_a, rwkv_r_k, rwkv_lnx_g, rwkv_lnx_b, vres_w1, vres_mu, vres_w2, vres_v0, gla_a2, gla_ab, gla_norm_g, w_o, norm_ffn, ffn_w_gu, ffn_w_down, norm_final):
    p = dict(norm_mix=norm_mix, rwkv_w0=rwkv_w0, rwkv_w2=rwkv_w2, rwkv_a0=rwkv_a0, rwkv_a2=rwkv_a2,
             rwkv_k_k=rwkv_k_k, rwkv_k_a=rwkv_k_a, rwkv_r_k=rwkv_r_k, rwkv_lnx_g=rwkv_lnx_g,
             rwkv_lnx_b=rwkv_lnx_b, vres_w2=vres_w2, vres_v0=vres_v0, gla_a2=gla_a2, gla_ab=gla_ab,
             gla_norm_g=gla_norm_g, w_o=w_o, norm_ffn=norm_ffn, ffn_w_gu=ffn_w_gu, ffn_w_down=ffn_w_down)
    depth = w_in.shape[0]
    weights = [_layer_weights(l, w_in, mu_shift, vres_w1, vres_mu, p) for l in range(depth)]
    g_final = norm_final[None, :]
    y_p, shift_p, wkv_p, gla_p = _run_group(x_prompt, None, None, None, weights, g_final)
    y_s, shift_s, wkv_s, gla_s = _run_group(x_sample, state_shift, state_wkv, state_gla, weights, g_final)
    return (y_p, y_s, shift_p, wkv_p, gla_p, shift_s, wkv_s, gla_s)
```

```python
import functools

import jax
import jax.numpy as jnp
from jax import lax
from jax.experimental import pallas as pl
from jax.experimental.pallas import tpu as pltpu

F32 = jnp.float32
BF16 = jnp.bfloat16

LANES = 128
D_MODEL = 1024
RWKV_HEAD_DIM = 64
RWKV_HEADS = D_MODEL // RWKV_HEAD_DIM
HEADS_PER_PAIR = LANES // RWKV_HEAD_DIM
N_PAIRS = RWKV_HEADS // HEADS_PER_PAIR
LORA_DECAY = 64
LORA_ICLR = 64
LORA_VRES = 32
RWKV_GN_EPS = 64e-5
GLA_HEADS = 4
GLA_DK = 128
GLA_DV = 256
GLA_GATE_RANK = 16
GLA_GATE_NORMALIZER = 16.0
GLA_NORM_EPS = 1e-5
FFN_HIDDEN = 2816
NORM_EPS = 1e-6
CHUNK = 64
SAMPLE_CHUNK = 8

COL_R, COL_K, COL_V = 0, 1024, 2048
COL_LO, COL_VLO, COL_GLO = 3072, 3200, 3328
COL_GQ = 3584
COL_GATE_A, COL_GATE_B = 4096, 5120
COL_GV, COL_GK = 6144, 7168
N_SHIFTED = 3328
N_PROJ = 7680
PROJ_TN = 1536
PROJ_SUB = 512
VMEM_LIMIT = 48 * 1024 * 1024
RWKV_ROWS_PER_STEP = 2
GLA_ROWS_PER_STEP = 4


def _cparams(sem):
    return pltpu.CompilerParams(dimension_semantics=sem, vmem_limit_bytes=VMEM_LIMIT)


def _mm(a, b):
    return jnp.dot(a, b, preferred_element_type=F32)


def _mm_nt(a, b):
    return lax.dot_general(a, b, (((1,), (1,)), ((), ())), preferred_element_type=F32)


def _mm_tn(a, b):
    return lax.dot_general(a, b, (((0,), (0,)), ((), ())), preferred_element_type=F32)


def _sigmoid(x):
    return 1.0 / (1.0 + jnp.exp(-x))


def _softplus(x):
    return jnp.maximum(x, 0.0) + jnp.log1p(jnp.exp(-jnp.abs(x)))


def _cumsum_rows(x):
    n = x.shape[0]
    row = lax.broadcasted_iota(jnp.int32, x.shape, 0)
    s = 1
    while s < n:
        x = x + jnp.where(row >= s, pltpu.roll(x, s, 0), 0.0)
        s *= 2
    return x


def _rms_rows(x, g):
    ms = jnp.mean(x * x, axis=-1, keepdims=True)
    return x * lax.rsqrt(ms + NORM_EPS) * g


def _proj_kernel(*refs, sample, tiles_per_seq, tm, n_shift_tiles):
    if sample:
        x_ref, g_ref, w_ref, mu_ref, prev_ref, o_ref, xs_ref, ps_ref = refs
    else:
        x_ref, g_ref, w_ref, mu_ref, o_ref, xs_ref, last_ref = refs
    i = pl.program_id(0)
    j = pl.program_id(1)

    @pl.when(j == 0)
    def _():
        xs_ref[...] = _rms_rows(x_ref[...], g_ref[...]).astype(BF16)
        if sample:
            ps_ref[...] = prev_ref[...].astype(BF16)

    xs = xs_ref[...]
    subs = [slice(n * PROJ_SUB, (n + 1) * PROJ_SUB) for n in range(PROJ_TN // PROJ_SUB)]
    if sample:
        ps = ps_ref[...]
        for cs in subs:
            y = _mm(xs, w_ref[:, cs])
            o_ref[:, cs] = y + mu_ref[:, cs] * (_mm(ps, w_ref[:, cs]) - y)
        return

    @pl.when(j < n_shift_tiles)
    def _():
        @pl.when(i == 0)
        def _():
            last_ref[j] = jnp.zeros((8, PROJ_TN), F32)

        row = lax.broadcasted_iota(jnp.int32, (tm, PROJ_SUB), 0)
        for cs in subs:
            y = _mm(xs, w_ref[:, cs])
            prev_row = jnp.where(i % tiles_per_seq == 0, 0.0, last_ref[j, 7:8, cs])
            shifted = jnp.where(row == 0, prev_row, pltpu.roll(y, 1, 0))
            last_ref[j, :, cs] = y[tm - 8:, :]
            o_ref[:, cs] = y + mu_ref[:, cs] * (shifted - y)

    @pl.when(j >= n_shift_tiles)
    def _():
        for cs in subs:
            o_ref[:, cs] = _mm(xs, w_ref[:, cs])


def _proj_call(x, g, w, mu, prev, seq_len):
    m = x.shape[0]
    sample = prev is not None
    tm = min(512, m if sample else seq_len)
    nj = N_PROJ // PROJ_TN
    n_shift_tiles = -(-N_SHIFTED // PROJ_TN)
    in_specs = [
        pl.BlockSpec((tm, D_MODEL), lambda i, j: (i, 0)),
        pl.BlockSpec((1, D_MODEL), lambda i, j: (0, 0)),
        pl.BlockSpec((D_MODEL, PROJ_TN), lambda i, j: (0, j)),
        pl.BlockSpec((1, PROJ_TN), lambda i, j: (0, j)),
    ]
    args = [x, g, w, mu]
    scratch = [pltpu.VMEM((tm, D_MODEL), BF16)]
    if sample:
        in_specs.append(pl.BlockSpec((tm, D_MODEL), lambda i, j: (i, 0)))
        args.append(prev)
        scratch.append(pltpu.VMEM((tm, D_MODEL), BF16))
    else:
        scratch.append(pltpu.VMEM((n_shift_tiles, 8, PROJ_TN), F32))
    return pl.pallas_call(
        functools.partial(_proj_kernel, sample=sample, tiles_per_seq=max(seq_len // tm, 1), tm=tm,
                          n_shift_tiles=n_shift_tiles),
        grid=(m // tm, nj),
        in_specs=in_specs,
        out_specs=pl.BlockSpec((tm, PROJ_TN), lambda i, j: (i, j)),
        out_shape=jax.ShapeDtypeStruct((m, N_PROJ), F32),
        scratch_shapes=scratch,
        compiler_params=_cparams(("arbitrary", "arbitrary")),
        name="proj_sample" if sample else "proj_prompt",
    )(*args)


def _rwkv_kernel(*refs, C, BB, n_valid, has_s0, has_vres):
    it = iter(refs)
    r_ref, k_ref, v_ref, lo_ref = next(it), next(it), next(it), next(it)
    vlo_ref = vf_ref = vw2_ref = s0_ref = None
    if has_vres:
        vlo_ref, vf_ref = next(it), next(it)
    if has_s0:
        s0_ref = next(it)
    w2a_ref = next(it)
    if has_vres:
        vw2_ref = next(it)
    vec_ref = next(it)
    o_ref, so_ref, s_scr = next(it), next(it), next(it)

    c = pl.program_id(1)
    H = RWKV_HEAD_DIM
    chains = [(bi, p) for bi in range(BB) for p in range(N_PAIRS)]
    sls = [slice(p * LANES, (p + 1) * LANES) for _, p in chains]
    bis = [bi for bi, _ in chains]

    @pl.when(c == 0)
    def _():
        s_scr[...] = jnp.zeros_like(s_scr)
        if has_s0:
            for i, (bi, p) in enumerate(chains):
                s_scr[i, 0:H, 0:H] = s0_ref[bi, 2 * p]
                s_scr[i, H:2 * H, H:2 * H] = s0_ref[bi, 2 * p + 1]

    lane = lax.broadcasted_iota(jnp.int32, (C, LANES), 1)
    head0 = lane < H
    trow = lax.broadcasted_iota(jnp.int32, (C, LANES), 0)
    n_sq = C.bit_length() - 2
    fused = 2 * C == LANES

    def stack(x):
        return jnp.concatenate([jnp.where(head0, x, 0.0), jnp.where(head0, 0.0, x)], axis=0)

    def head_sum(x):
        s0 = jnp.sum(jnp.where(head0, x, 0.0), axis=-1, keepdims=True)
        s1 = jnp.sum(jnp.where(head0, 0.0, x), axis=-1, keepdims=True)
        return jnp.where(head0, s0, s1)

    def each(f, *lists):
        return [f(*xs) for xs in zip(*lists)]

    bf = lambda x: x.astype(BF16)
    vec = lambda row: [vec_ref[row:row + 1, sl] for sl in sls]
    w0, a0, k_k, k_a, r_k, lnx_g, lnx_b, v0 = (vec(i) for i in range(8))
    r = [r_ref[bi, :, sl] for bi, sl in zip(bis, sls)]
    k = [k_ref[bi, :, sl] for bi, sl in zip(bis, sls)]
    v = [v_ref[bi, :, sl] for bi, sl in zip(bis, sls)]

    los = [lo_ref[bi] for bi in range(BB)]
    w2a = w2a_ref[...]
    wl_rows = [_mm(jnp.where(head0, jnp.tanh(lo), 0.0), w2a) for lo in los]
    al_rows = [_mm(jnp.where(head0, 0.0, lo), w2a) for lo in los]
    wl = [wl_rows[bi][:, sl] for bi, sl in zip(bis, sls)]
    al_ = [al_rows[bi][:, sl] for bi, sl in zip(bis, sls)]
    if has_vres:
        vw2 = vw2_ref[...]
        vg_rows = [_mm(vlo_ref[bi], vw2) for bi in range(BB)]
        v = [vp + (vf_ref[bi, :, sl] - vp) * _sigmoid(z + vg_rows[bi][:, sl])
             for vp, bi, sl, z in zip(v, bis, sls, v0)]

    def decay_log(w0p, wlp):
        ld = -jnp.exp(-_softplus(-(w0p + wlp)) - 0.5)
        if n_valid < C:
            ld = jnp.where(trow < n_valid, ld, 0.0)
        return ld

    ld = each(decay_log, w0, wl)
    a = each(lambda z, y: _sigmoid(z + y), a0, al_)

    def unit_kk(kp, kkp):
        kk = kp * kkp
        return kk / jnp.maximum(jnp.sqrt(head_sum(kk * kk)), 1e-12)

    kk = each(unit_kk, k, k_k)
    k = each(lambda kp, ap, kap: kp * (1.0 + (ap - 1.0) * kap), k, a, k_a)
    b = each(_cumsum_rows, ld)
    e_b = each(jnp.exp, b)
    e_nb = each(lambda x: jnp.exp(-x), b)
    al = each(lambda kkp, bp, ldp: bf(stack(-kkp * jnp.exp(bp - ldp))), kk, b, ld)
    be = each(lambda ap, kkp, e: bf(stack(ap * kkp * e)), a, kk, e_nb)
    kt = each(lambda kp, e: bf(stack(kp * e)), k, e_nb)
    rt = each(lambda rp, e: bf(stack(rp * e)), r, e_b)
    vs = each(lambda x: bf(stack(x)), v)
    S32 = [s_scr[i] for i in range(len(chains))]

    if fused:
        ri = lax.broadcasted_iota(jnp.int32, (2 * LANES, 2 * LANES), 0)
        ci = lax.broadcasted_iota(jnp.int32, (2 * LANES, 2 * LANES), 1)
        rr, cc = ri % LANES, ci % LANES
        same_head = (rr // C) == (cc // C)
        tri = (rr % C) >= (cc % C) + (ri < LANES).astype(jnp.int32)
        mask4 = same_head & tri
        r1 = lax.broadcasted_iota(jnp.int32, (LANES, LANES), 0)
        c1 = lax.broadcasted_iota(jnp.int32, (LANES, LANES), 1)
        eye = (r1 == c1).astype(F32)
        cat0 = lambda x, y: jnp.concatenate([x, y], axis=0)
        cat1 = lambda x, y: jnp.concatenate([x, y], axis=1)

        ar = each(cat0, al, rt)
        bk = each(cat0, be, kt)
        amat = each(lambda x, y: jnp.where(mask4, _mm_nt(x, y), 0.0), ar, bk)
        sv = each(lambda s, y: cat0(bf(s.T), y), S32, vs)
        x = each(lambda l, m, rhs: _mm(cat1(l, bf(m[:, LANES:])), rhs), ar, amat, sv)
        tinv = each(lambda m: eye + m[:LANES, :LANES], amat)
        q = each(lambda m: bf(m[:LANES, :LANES]), amat)
        q = each(lambda z: bf(_mm(z, z)), q)
        for _ in range(n_sq - 1):
            y = each(lambda z, t: _mm(z, cat1(bf(t), z)), q, tinv)
            tinv = each(lambda t, z: t + z[:, :LANES], tinv, y)
            q = each(lambda z: bf(z[:, LANES:]), y)
        tinv = each(lambda t, z: t + _mm(z, bf(t)), tinv, q)
        cm = each(lambda t, z: bf(_mm(bf(t), bf(z[:LANES]))), tinv, x)
        o2 = each(lambda z, m, cc_: z[LANES:] + _mm(bf(m[LANES:, :LANES]), cc_), x, amat, cm)
        s_new = each(lambda s, cc_, y, rhs, e: (s + _mm_tn(cat0(cc_, y), rhs)) * e[C - 1:C, :],
                     S32, cm, vs, bk, e_b)
    else:
        ri = lax.broadcasted_iota(jnp.int32, (2 * C, 2 * C), 0)
        ci = lax.broadcasted_iota(jnp.int32, (2 * C, 2 * C), 1)
        same_head = (ri // C) == (ci // C)
        strict = same_head & ((ri % C) > (ci % C))
        incl = same_head & ((ri % C) >= (ci % C))
        eye = (ri == ci).astype(F32)
        S = each(bf, S32)
        a_ak = each(lambda x, y: bf(jnp.where(strict, _mm_nt(x, y), 0.0)), al, kt)
        a_rb = each(lambda x, y: bf(jnp.where(incl, _mm_nt(x, y), 0.0)), rt, be)
        a_rk = each(lambda x, y: bf(jnp.where(incl, _mm_nt(x, y), 0.0)), rt, kt)
        cm = each(lambda x, s, m, y: _mm_nt(x, s) + _mm(m, y), al, S, a_ak, vs)
        if n_valid > 1:
            pw = each(lambda x, y: jnp.where(strict, _mm_nt(x, y), 0.0), al, be)
            tinv = each(lambda x: eye + x, pw)
            pwb = each(bf, pw)
            for _ in range(n_sq):
                pwb = each(lambda x: bf(_mm(x, x)), pwb)
                tinv = each(lambda t, x: t + _mm(bf(t), x), tinv, pwb)
            cm = each(lambda t, x: _mm(bf(t), bf(x)), tinv, cm)
        cm = each(bf, cm)
        o2 = each(lambda x, s, m1, c1, m2, y: _mm_nt(x, s) + _mm(m1, c1) + _mm(m2, y),
                  rt, S, a_rb, cm, a_rk, vs)
        s_new = each(lambda s, c1, x, y, z, e: (s + _mm_tn(c1, x) + _mm_tn(y, z)) * e[C - 1:C, :],
                     S32, cm, be, vs, kt, e_b)

    for i in range(len(chains)):
        s_scr[i] = s_new[i]

    for i, (bi, sl) in enumerate(zip(bis, sls)):
        o = o2[i][:C] + o2[i][C:]
        mean = head_sum(o) * (1.0 / H)
        d = o - mean
        var = head_sum(d * d) * (1.0 / H)
        o = d * lax.rsqrt(var + RWKV_GN_EPS) * lnx_g[i] + lnx_b[i]
        o_ref[bi, :, sl] = o + head_sum(r[i] * k[i] * r_k[i]) * v[i]

    @pl.when(c == pl.num_programs(1) - 1)
    def _():
        for i, (bi, p) in enumerate(chains):
            so_ref[bi, 2 * p] = s_scr[i, 0:H, 0:H]
            so_ref[bi, 2 * p + 1] = s_scr[i, H:2 * H, H:2 * H]


def _rwkv_call(P, Pfirst, s0, w2a, vw2, vecs, *, C, n_valid, BB):
    B, T, _ = P.shape
    has_vres = Pfirst is not None
    has_s0 = s0 is not None

    def cols(off, width):
        idx = off // width
        return pl.BlockSpec((BB, C, width), lambda b, c: (b, c, idx))

    whole = lambda shape: pl.BlockSpec(shape, lambda b, c: (0,) * len(shape))
    state_spec = pl.BlockSpec((BB, RWKV_HEADS, RWKV_HEAD_DIM, RWKV_HEAD_DIM), lambda b, c: (b, 0, 0, 0))
    in_specs = [cols(COL_R, D_MODEL), cols(COL_K, D_MODEL), cols(COL_V, D_MODEL), cols(COL_LO, LANES)]
    args = [P, P, P, P]
    if has_vres:
        in_specs += [cols(COL_VLO, LANES), cols(COL_V, D_MODEL)]
        args += [P, Pfirst]
    if has_s0:
        in_specs.append(state_spec)
        args.append(s0)
    in_specs.append(whole((LANES, D_MODEL)))
    args.append(w2a)
    if has_vres:
        in_specs.append(whole((LANES, D_MODEL)))
        args.append(vw2)
    in_specs.append(whole((8, D_MODEL)))
    args.append(vecs)
    return pl.pallas_call(
        functools.partial(_rwkv_kernel, C=C, BB=BB, n_valid=n_valid, has_s0=has_s0, has_vres=has_vres),
        grid=(B // BB, T // C),
        in_specs=in_specs,
        out_specs=[pl.BlockSpec((BB, C, D_MODEL), lambda b, c: (b, c, 0)), state_spec],
        out_shape=[jax.ShapeDtypeStruct((B, T, D_MODEL), F32),
                   jax.ShapeDtypeStruct((B, RWKV_HEADS, RWKV_HEAD_DIM, RWKV_HEAD_DIM), F32)],
        scratch_shapes=[pltpu.VMEM((BB * N_PAIRS, LANES, LANES), F32)],
        compiler_params=_cparams(("parallel", "arbitrary")),
        name="rwkv7_chunk",
    )(*args)


def _gla_kernel(*refs, C, BB, n_valid, has_s0):
    if has_s0:
        q_ref, k_ref, v_ref, glo_ref, s0_ref, ga2_ref, gab_ref, g_ref, o_ref, so_ref, s_scr = refs
    else:
        q_ref, k_ref, v_ref, glo_ref, ga2_ref, gab_ref, g_ref, o_ref, so_ref, s_scr = refs
    c = pl.program_id(1)
    chains = [(bi, h) for bi in range(BB) for h in range(GLA_HEADS)]

    @pl.when(c == 0)
    def _():
        if has_s0:
            for i, (bi, h) in enumerate(chains):
                s_scr[i] = s0_ref[bi, h].T
        else:
            s_scr[...] = jnp.zeros_like(s_scr)

    ri = lax.broadcasted_iota(jnp.int32, (C, C), 0)
    ci = lax.broadcasted_iota(jnp.int32, (C, C), 1)
    causal = ri >= ci
    trow = lax.broadcasted_iota(jnp.int32, (C, GLA_DK), 0)
    bis = [bi for bi, _ in chains]
    ks = [slice(h * GLA_DK, (h + 1) * GLA_DK) for _, h in chains]
    vsl = [slice(h * GLA_DV, (h + 1) * GLA_DV) for _, h in chains]
    v = [v_ref[bi, :, s] for bi, s in zip(bis, vsl)]
    k = [k_ref[bi, :, s] for bi, s in zip(bis, ks)]
    ga2 = ga2_ref[...]
    gate_rows = [_mm(glo_ref[bi], ga2) for bi in range(BB)]

    def cum_log_decay(bi, s):
        la = -_softplus(-(gate_rows[bi][:, s] + gab_ref[:, s])) * (1.0 / GLA_GATE_NORMALIZER)
        if n_valid < C:
            la = jnp.where(trow < n_valid, la, 0.0)
        return _cumsum_rows(la)

    b = [cum_log_decay(bi, s) for bi, s in zip(bis, ks)]
    qd = [q_ref[bi, :, s] * (GLA_DK ** -0.5) * jnp.exp(bh) for bi, s, bh in zip(bis, ks, b)]
    kd = [kh * jnp.exp(-bh) for kh, bh in zip(k, b)]
    kl = [kh * jnp.exp(bh[C - 1:C, :] - bh) for kh, bh in zip(k, b)]
    St = [s_scr[i] for i in range(len(chains))]
    att = [jnp.where(causal, _mm_nt(x, y), 0.0) for x, y in zip(qd, kd)]
    o_s = [_mm_nt(x, s) for x, s in zip(qd, St)]
    upd = [_mm_tn(x, y) for x, y in zip(v, kl)]
    o = [_mm(x, y) + z for x, y, z in zip(att, v, o_s)]
    for i, (bi, s) in enumerate(zip(bis, vsl)):
        s_scr[i] = St[i] * jnp.exp(b[i][C - 1:C, :]) + upd[i]
        o_ref[bi, :, s] = (o[i] * lax.rsqrt(jnp.mean(o[i] * o[i], axis=-1, keepdims=True)
                                            + GLA_NORM_EPS) * g_ref[...])

    @pl.when(c == pl.num_programs(1) - 1)
    def _():
        for i, (bi, h) in enumerate(chains):
            so_ref[bi, h] = s_scr[i].T


def _gla_call(P, s0, ga2, gab, g, *, C, n_valid, BB):
    B, T, _ = P.shape
    has_s0 = s0 is not None
    kw = GLA_HEADS * GLA_DK
    vw = GLA_HEADS * GLA_DV
    state_spec = pl.BlockSpec((BB, GLA_HEADS, GLA_DK, GLA_DV), lambda b, c: (b, 0, 0, 0))
    in_specs = [pl.BlockSpec((BB, C, kw), lambda b, c: (b, c, COL_GQ // kw)),
                pl.BlockSpec((BB, C, kw), lambda b, c: (b, c, COL_GK // kw)),
                pl.BlockSpec((BB, C, vw), lambda b, c: (b, c, COL_GV // vw)),
                pl.BlockSpec((BB, C, LANES), lambda b, c: (b, c, COL_GLO // LANES))]
    args = [P, P, P, P]
    if has_s0:
        in_specs.append(state_spec)
        args.append(s0)
    in_specs += [pl.BlockSpec((LANES, kw), lambda b, c: (0, 0)),
                 pl.BlockSpec((1, kw), lambda b, c: (0, 0)),
                 pl.BlockSpec((1, GLA_DV), lambda b, c: (0, 0))]
    args += [ga2, gab, g]
    return pl.pallas_call(
        functools.partial(_gla_kernel, C=C, BB=BB, n_valid=n_valid, has_s0=has_s0),
        grid=(B // BB, T // C),
        in_specs=in_specs,
        out_specs=[pl.BlockSpec((BB, C, vw), lambda b, c: (b, c, 0)), state_spec],
        out_shape=[jax.ShapeDtypeStruct((B, T, vw), F32),
                   jax.ShapeDtypeStruct((B, GLA_HEADS, GLA_DK, GLA_DV), F32)],
        scratch_shapes=[pltpu.VMEM((BB * GLA_HEADS, GLA_DV, GLA_DK), F32)],
        compiler_params=_cparams(("parallel", "arbitrary")),
        name="gla_chunk",
    )(*args)


def _wo_kernel(ga_ref, gb_ref, ro_ref, go_ref, h_ref, w_ref, o_ref):
    merged = _sigmoid(ga_ref[...]) * ro_ref[...] + _sigmoid(gb_ref[...]) * go_ref[...]
    o_ref[...] = h_ref[...] + _mm(merged.astype(BF16), w_ref[...])


def _wo_call(P, ro, go, h, w):
    m = h.shape[0]
    tm = min(512, m)
    row = lambda i: (i, 0)
    return pl.pallas_call(
        _wo_kernel,
        grid=(m // tm,),
        in_specs=[pl.BlockSpec((tm, D_MODEL), lambda i: (i, COL_GATE_A // D_MODEL)),
                  pl.BlockSpec((tm, D_MODEL), lambda i: (i, COL_GATE_B // D_MODEL)),
                  pl.BlockSpec((tm, D_MODEL), row),
                  pl.BlockSpec((tm, D_MODEL), row),
                  pl.BlockSpec((tm, D_MODEL), row),
                  pl.BlockSpec((D_MODEL, D_MODEL), lambda i: (0, 0))],
        out_specs=pl.BlockSpec((tm, D_MODEL), row),
        out_shape=jax.ShapeDtypeStruct((m, D_MODEL), F32),
        compiler_params=_cparams(("parallel",)),
        name="merge_wo",
    )(P, P, ro, go, h, w)


FFN_SUB = 256


def _ffn_kernel(h_ref, g_ref, wg_ref, wu_ref, wd_ref, gf_ref, o_ref, *, final_norm):
    h = h_ref[...]
    xs = _rms_rows(h, g_ref[...]).astype(BF16)
    out = h
    for n in range(FFN_HIDDEN // FFN_SUB):
        cs = slice(n * FFN_SUB, (n + 1) * FFN_SUB)
        gate = _mm(xs, wg_ref[:, cs])
        up = _mm(xs, wu_ref[:, cs])
        act = (gate * _sigmoid(gate) * up).astype(BF16)
        out = out + _mm(act, wd_ref[cs, :])
    if final_norm:
        out = _rms_rows(out, gf_ref[...])
    o_ref[...] = out


def _ffn_call(h, g, w_gu, w_down, gf, final_norm):
    m = h.shape[0]
    tm = min(512, m)
    row = lambda i: (i, 0)
    return pl.pallas_call(
        functools.partial(_ffn_kernel, final_norm=final_norm),
        grid=(m // tm,),
        in_specs=[pl.BlockSpec((tm, D_MODEL), row),
                  pl.BlockSpec((1, D_MODEL), lambda i: (0, 0)),
                  pl.BlockSpec((D_MODEL, FFN_HIDDEN), lambda i: (0, 0)),
                  pl.BlockSpec((D_MODEL, FFN_HIDDEN), lambda i: (0, 1)),
                  pl.BlockSpec((FFN_HIDDEN, D_MODEL), lambda i: (0, 0)),
                  pl.BlockSpec((1, D_MODEL), lambda i: (0, 0))],
        out_specs=pl.BlockSpec((tm, D_MODEL), row),
        out_shape=jax.ShapeDtypeStruct((m, D_MODEL), F32),
        compiler_params=_cparams(("parallel",)),
        name="ffn",
    )(h, g, w_gu, w_gu, w_down, gf)


def _rms_kernel(x_ref, g_ref, o_ref):
    o_ref[...] = _rms_rows(x_ref[...], g_ref[...])


def _rms_call(x, g):
    return pl.pallas_call(
        _rms_kernel,
        out_shape=jax.ShapeDtypeStruct(x.shape, F32),
        name="rmsnorm_rows",
    )(x, g)


def _pad_cols(x, n):
    return jnp.pad(x, ((0, 0), (0, n - x.shape[1])))


def _pad_rows(x, n):
    return jnp.pad(x, ((0, n - x.shape[0]), (0, 0)))


def _layer_weights(l, w_in, mu_shift, vres_w1, vres_mu, p):
    W = w_in[l]
    mu = mu_shift[l]
    sizes = (1024, 1024, 1024, 64, 64, 512, 512, 1024, 16, 1024, 1024)
    offs = [0]
    for s in sizes:
        offs.append(offs[-1] + s)
    (r, k, v, wlo, alo, gq, gk, gv, glo, ga, gb) = [W[:, offs[i]:offs[i + 1]] for i in range(len(sizes))]
    zeros_mu = lambda n: jnp.zeros((n,), F32)
    if l > 0:
        vlo_w = _pad_cols(vres_w1[l - 1], LANES)
        vlo_mu = jnp.pad(vres_mu[l - 1], (0, LANES - LORA_VRES))
    else:
        vlo_w = jnp.zeros((D_MODEL, LANES), F32)
        vlo_mu = zeros_mu(LANES)
    w_cat = jnp.concatenate([r, k, v, wlo, alo, vlo_w, _pad_cols(glo, LANES),
                             jnp.zeros((D_MODEL, LANES), F32), gq, ga, gb, gv, gk], axis=1).astype(BF16)
    mu_cat = jnp.concatenate([mu[:3200], vlo_mu, zeros_mu(N_PROJ - N_SHIFTED)])[None, :]
    lw = dict(w_cat=w_cat, mu_cat=mu_cat)
    lw['w2a'] = jnp.concatenate([p['rwkv_w2'][l], p['rwkv_a2'][l]], axis=0)
    lw['vecs'] = jnp.stack([p['rwkv_w0'][l], p['rwkv_a0'][l], p['rwkv_k_k'][l], p['rwkv_k_a'][l],
                            p['rwkv_r_k'][l].reshape(-1), p['rwkv_lnx_g'][l], p['rwkv_lnx_b'][l],
                            p['vres_v0'][l - 1] if l > 0 else zeros_mu(D_MODEL)], axis=0)
    lw['vw2'] = _pad_rows(p['vres_w2'][l - 1], LANES) if l > 0 else None
    lw['ga2'] = _pad_rows(p['gla_a2'][l], LANES)
    lw['gab'] = p['gla_ab'][l][None, :]
    lw['gla_g'] = p['gla_norm_g'][l][None, :]
    lw['w_o'] = p['w_o'][l].astype(BF16)
    lw['w_gu'] = p['ffn_w_gu'][l].astype(BF16)
    lw['w_down'] = p['ffn_w_down'][l].astype(BF16)
    lw['g_mix'] = p['norm_mix'][l][None, :]
    lw['g_ffn'] = p['norm_ffn'][l][None, :]
    return lw


def _rows_per_step(batch, want):
    while batch % want:
        want //= 2
    return want


def _run_group(x, shift_state, wkv_state, gla_state, weights, g_final):
    B, T, _ = x.shape
    fresh = shift_state is None
    C = CHUNK if fresh else SAMPLE_CHUNK
    h = x.reshape(B * T, D_MODEL)
    new_shift, new_wkv, new_gla = [], [], []
    P_first = None
    depth = len(weights)
    for l, lw in enumerate(weights):
        new_shift.append(_rms_call(h.reshape(B, T, D_MODEL)[:, -1], lw['g_mix']))
        if fresh:
            P = _proj_call(h, lw['g_mix'], lw['w_cat'], lw['mu_cat'], None, T).reshape(B, T, N_PROJ)
        else:
            P = _proj_call(h, lw['g_mix'], lw['w_cat'], lw['mu_cat'], shift_state[l], T)
            P = jnp.pad(P[:, None, :], ((0, 0), (0, C - T), (0, 0)))
        Tp = P.shape[1]
        ro, s_wkv = _rwkv_call(P, P_first if l > 0 else None, None if fresh else wkv_state[l],
                               lw['w2a'], lw['vw2'], lw['vecs'], C=C, n_valid=min(T, C),
                               BB=_rows_per_step(B, RWKV_ROWS_PER_STEP))
        go, s_gla = _gla_call(P, None if fresh else gla_state[l], lw['ga2'], lw['gab'], lw['gla_g'],
                              C=C, n_valid=min(T, C), BB=_rows_per_step(B, GLA_ROWS_PER_STEP))
        if l == 0:
            P_first = P
        if Tp != T:
            ro, go = ro[:, :T], go[:, :T]
            P2 = P[:, :T].reshape(B * T, N_PROJ)
        else:
            P2 = P.reshape(B * T, N_PROJ)
        h = _wo_call(P2, ro.reshape(B * T, D_MODEL), go.reshape(B * T, D_MODEL), h, lw['w_o'])
        h = _ffn_call(h, lw['g_ffn'], lw['w_gu'], lw['w_down'], g_final, final_norm=(l == depth - 1))
        new_wkv.append(s_wkv)
        new_gla.append(s_gla)
    return (h.reshape(B, T, D_MODEL), jnp.stack(new_shift), jnp.stack(new_wkv), jnp.stack(new_gla))


def kernel(x_prompt, x_sample, state_shift, state_wkv, state_gla, norm_mix, w_in, mu_shift, rwkv_w0, rwkv_w2, rwkv_a0, rwkv_a2, rwkv_k_k, rwkv_k_a, rwkv_r_k, rwkv_lnx_g, rwkv_lnx_b, vres_w1, vres_mu, vres_w2, vres_v0, gla_a2, gla_ab, gla_norm_g, w_o, norm_ffn, ffn_w_gu, ffn_w_down, norm_final):
    p = dict(norm_mix=norm_mix, rwkv_w0=rwkv_w0, rwkv_w2=rwkv_w2, rwkv_a0=rwkv_a0, rwkv_a2=rwkv_a2,
             rwkv_k_k=rwkv_k_k, rwkv_k_a=rwkv_k_a, rwkv_r_k=rwkv_r_k, rwkv_lnx_g=rwkv_lnx_g,
             rwkv_lnx_b=rwkv_lnx_b, vres_w2=vres_w2, vres_v0=vres_v0, gla_a2=gla_a2, gla_ab=gla_ab,
             gla_norm_g=gla_norm_g, w_o=w_o, norm_ffn=norm_ffn, ffn_w_gu=ffn_w_gu, ffn_w_down=ffn_w_down)
    depth = w_in.shape[0]
    weights = [_layer_weights(l, w_in, mu_shift, vres_w1, vres_mu, p) for l in range(depth)]
    g_final = norm_final[None, :]
    y_p, shift_p, wkv_p, gla_p = _run_group(x_prompt, None, None, None, weights, g_final)
    y_s, shift_s, wkv_s, gla_s = _run_group(x_sample, state_shift, state_wkv, state_gla, weights, g_final)
    return (y_p, y_s, shift_p, wkv_p, gla_p, shift_s, wkv_s, gla_s)
```

```python
import functools

import jax
import jax.numpy as jnp
from jax import lax
from jax.experimental import pallas as pl
from jax.experimental.pallas import tpu as pltpu

F32 = jnp.float32
BF16 = jnp.bfloat16

LANES = 128
D_MODEL = 1024
RWKV_HEAD_DIM = 64
RWKV_HEADS = D_MODEL // RWKV_HEAD_DIM
HEADS_PER_PAIR = LANES // RWKV_HEAD_DIM
N_PAIRS = RWKV_HEADS // HEADS_PER_PAIR
LORA_DECAY = 64
LORA_ICLR = 64
LORA_VRES = 32
RWKV_GN_EPS = 64e-5
GLA_HEADS = 4
GLA_DK = 128
GLA_DV = 256
GLA_GATE_RANK = 16
GLA_GATE_NORMALIZER = 16.0
GLA_NORM_EPS = 1e-5
FFN_HIDDEN = 2816
NORM_EPS = 1e-6
CHUNK = 64
SAMPLE_CHUNK = 8

COL_R, COL_K, COL_V = 0, 1024, 2048
COL_LO, COL_VLO, COL_GLO = 3072, 3200, 3328
COL_GQ = 3584
COL_GATE_A, COL_GATE_B = 4096, 5120
COL_GV, COL_GK = 6144, 7168
N_SHIFTED = 3328
N_PROJ = 7680
PROJ_TN = 1536
PROJ_SUB = 512
VMEM_LIMIT = 48 * 1024 * 1024
RWKV_ROWS_PER_STEP = 4
GLA_ROWS_PER_STEP = 4


def _cparams(sem):
    return pltpu.CompilerParams(dimension_semantics=sem, vmem_limit_bytes=VMEM_LIMIT)


def _mm(a, b):
    return jnp.dot(a, b, preferred_element_type=F32)


def _mm_nt(a, b):
    return lax.dot_general(a, b, (((1,), (1,)), ((), ())), preferred_element_type=F32)


def _mm_tn(a, b):
    return lax.dot_general(a, b, (((0,), (0,)), ((), ())), preferred_element_type=F32)


def _sigmoid(x):
    return 0.5 * jnp.tanh(0.5 * x) + 0.5


def _softplus(x):
    return jnp.maximum(x, 0.0) + jnp.log(1.0 + jnp.exp(-jnp.abs(x)))


def _cumsum_rows(x):
    n = x.shape[0]
    row = lax.broadcasted_iota(jnp.int32, x.shape, 0)
    s = 1
    while s < n:
        x = x + jnp.where(row >= s, pltpu.roll(x, s, 0), 0.0)
        s *= 2
    return x


def _rms_rows(x, g):
    ms = jnp.mean(x * x, axis=-1, keepdims=True)
    return x * lax.rsqrt(ms + NORM_EPS) * g


def _proj_kernel(*refs, sample, tiles_per_seq, tm, n_shift_tiles):
    if sample:
        x_ref, g_ref, w_ref, mu_ref, prev_ref, o_ref, xs_ref, ps_ref = refs
    else:
        x_ref, g_ref, w_ref, mu_ref, o_ref, xs_ref, last_ref = refs
    i = pl.program_id(0)
    j = pl.program_id(1)

    @pl.when(j == 0)
    def _():
        xs_ref[...] = _rms_rows(x_ref[...], g_ref[...]).astype(BF16)
        if sample:
            ps_ref[...] = prev_ref[...].astype(BF16)

    xs = xs_ref[...]
    subs = [slice(n * PROJ_SUB, (n + 1) * PROJ_SUB) for n in range(PROJ_TN // PROJ_SUB)]
    if sample:
        ps = ps_ref[...]
        for cs in subs:
            y = _mm(xs, w_ref[:, cs])
            o_ref[:, cs] = y + mu_ref[:, cs] * (_mm(ps, w_ref[:, cs]) - y)
        return

    @pl.when(j < n_shift_tiles)
    def _():
        @pl.when(i == 0)
        def _():
            last_ref[j] = jnp.zeros((8, PROJ_TN), F32)

        row = lax.broadcasted_iota(jnp.int32, (tm, PROJ_SUB), 0)
        for cs in subs:
            y = _mm(xs, w_ref[:, cs])
            prev_row = jnp.where(i % tiles_per_seq == 0, 0.0, last_ref[j, 7:8, cs])
            shifted = jnp.where(row == 0, prev_row, pltpu.roll(y, 1, 0))
            last_ref[j, :, cs] = y[tm - 8:, :]
            o_ref[:, cs] = y + mu_ref[:, cs] * (shifted - y)

    @pl.when(j >= n_shift_tiles)
    def _():
        for cs in subs:
            o_ref[:, cs] = _mm(xs, w_ref[:, cs])


def _proj_call(x, g, w, layer, mu, prev, seq_len):
    m = x.shape[0]
    sample = prev is not None
    tm = min(512, m if sample else seq_len)
    nj = N_PROJ // PROJ_TN
    n_shift_tiles = -(-N_SHIFTED // PROJ_TN)
    in_specs = [
        pl.BlockSpec((tm, D_MODEL), lambda i, j: (i, 0)),
        pl.BlockSpec((1, D_MODEL), lambda i, j: (0, 0)),
        pl.BlockSpec((None, D_MODEL, PROJ_TN), lambda i, j: (layer, 0, j)),
        pl.BlockSpec((1, PROJ_TN), lambda i, j: (0, j)),
    ]
    args = [x, g, w, mu]
    scratch = [pltpu.VMEM((tm, D_MODEL), BF16)]
    if sample:
        in_specs.append(pl.BlockSpec((tm, D_MODEL), lambda i, j: (i, 0)))
        args.append(prev)
        scratch.append(pltpu.VMEM((tm, D_MODEL), BF16))
    else:
        scratch.append(pltpu.VMEM((n_shift_tiles, 8, PROJ_TN), F32))
    return pl.pallas_call(
        functools.partial(_proj_kernel, sample=sample, tiles_per_seq=max(seq_len // tm, 1), tm=tm,
                          n_shift_tiles=n_shift_tiles),
        grid=(m // tm, nj),
        in_specs=in_specs,
        out_specs=pl.BlockSpec((tm, PROJ_TN), lambda i, j: (i, j)),
        out_shape=jax.ShapeDtypeStruct((m, N_PROJ), F32),
        scratch_shapes=scratch,
        compiler_params=_cparams(("arbitrary", "arbitrary")),
        name="proj_sample" if sample else "proj_prompt",
    )(*args)


def _rwkv_kernel(*refs, C, BB, n_valid, has_s0, has_vres):
    it = iter(refs)
    r_ref, k_ref, v_ref, lo_ref = next(it), next(it), next(it), next(it)
    vlo_ref = vf_ref = vw2_ref = s0_ref = None
    if has_vres:
        vlo_ref, vf_ref = next(it), next(it)
    if has_s0:
        s0_ref = next(it)
    w2a_ref = next(it)
    if has_vres:
        vw2_ref = next(it)
    vec_ref = next(it)
    o_ref, so_ref, s_scr = next(it), next(it), next(it)

    c = pl.program_id(1)
    H = RWKV_HEAD_DIM
    chains = [(bi, p) for bi in range(BB) for p in range(N_PAIRS)]
    sls = [slice(p * LANES, (p + 1) * LANES) for _, p in chains]
    bis = [bi for bi, _ in chains]

    @pl.when(c == 0)
    def _():
        s_scr[...] = jnp.zeros_like(s_scr)
        if has_s0:
            for i, (bi, p) in enumerate(chains):
                s_scr[i, 0:H, 0:H] = s0_ref[bi, 2 * p]
                s_scr[i, H:2 * H, H:2 * H] = s0_ref[bi, 2 * p + 1]

    lane = lax.broadcasted_iota(jnp.int32, (C, LANES), 1)
    head0 = lane < H
    trow = lax.broadcasted_iota(jnp.int32, (C, LANES), 0)
    n_sq = C.bit_length() - 2
    fused = 2 * C == LANES

    def stack(x):
        return jnp.concatenate([jnp.where(head0, x, 0.0), jnp.where(head0, 0.0, x)], axis=0)

    def head_sum(x):
        s0 = jnp.sum(jnp.where(head0, x, 0.0), axis=-1, keepdims=True)
        s1 = jnp.sum(jnp.where(head0, 0.0, x), axis=-1, keepdims=True)
        return jnp.where(head0, s0, s1)

    def each(f, *lists):
        return [f(*xs) for xs in zip(*lists)]

    bf = lambda x: x.astype(BF16)
    vec = lambda row: [vec_ref[row:row + 1, sl] for sl in sls]
    w0, a0, k_k, k_a, r_k, lnx_g, lnx_b, v0 = (vec(i) for i in range(8))
    r = [r_ref[bi, :, sl] for bi, sl in zip(bis, sls)]
    k = [k_ref[bi, :, sl] for bi, sl in zip(bis, sls)]
    v = [v_ref[bi, :, sl] for bi, sl in zip(bis, sls)]

    los = [lo_ref[bi] for bi in range(BB)]
    w2a = w2a_ref[...]
    wl_rows = [_mm(jnp.where(head0, jnp.tanh(lo), 0.0), w2a) for lo in los]
    al_rows = [_mm(jnp.where(head0, 0.0, lo), w2a) for lo in los]
    wl = [wl_rows[bi][:, sl] for bi, sl in zip(bis, sls)]
    al_ = [al_rows[bi][:, sl] for bi, sl in zip(bis, sls)]
    if has_vres:
        vw2 = vw2_ref[...]
        vg_rows = [_mm(vlo_ref[bi], vw2) for bi in range(BB)]
        v = [vp + (vf_ref[bi, :, sl] - vp) * _sigmoid(z + vg_rows[bi][:, sl])
             for vp, bi, sl, z in zip(v, bis, sls, v0)]

    def decay_log(w0p, wlp):
        ld = -jnp.exp(-_softplus(-(w0p + wlp)) - 0.5)
        if n_valid < C:
            ld = jnp.where(trow < n_valid, ld, 0.0)
        return ld

    ld = each(decay_log, w0, wl)
    a = each(lambda z, y: _sigmoid(z + y), a0, al_)

    def unit_kk(kp, kkp):
        kk = kp * kkp
        return kk * lax.rsqrt(jnp.maximum(head_sum(kk * kk), 1e-24))

    kk = each(unit_kk, k, k_k)
    k = each(lambda kp, ap, kap: kp * (1.0 + (ap - 1.0) * kap), k, a, k_a)
    b = each(_cumsum_rows, ld)
    e_b = each(jnp.exp, b)
    e_nb = each(lambda x: jnp.exp(-x), b)
    al = each(lambda kkp, bp, ldp: bf(stack(-kkp * jnp.exp(bp - ldp))), kk, b, ld)
    be = each(lambda ap, kkp, e: bf(stack(ap * kkp * e)), a, kk, e_nb)
    kt = each(lambda kp, e: bf(stack(kp * e)), k, e_nb)
    rt = each(lambda rp, e: bf(stack(rp * e)), r, e_b)
    vs = each(lambda x: bf(stack(x)), v)
    S32 = [s_scr[i] for i in range(len(chains))]

    if fused:
        ri = lax.broadcasted_iota(jnp.int32, (2 * LANES, 2 * LANES), 0)
        ci = lax.broadcasted_iota(jnp.int32, (2 * LANES, 2 * LANES), 1)
        rr, cc = ri % LANES, ci % LANES
        same_head = (rr // C) == (cc // C)
        tri = (rr % C) >= (cc % C) + (ri < LANES).astype(jnp.int32)
        mask4 = same_head & tri
        r1 = lax.broadcasted_iota(jnp.int32, (LANES, LANES), 0)
        c1 = lax.broadcasted_iota(jnp.int32, (LANES, LANES), 1)
        eye = (r1 == c1).astype(F32)
        cat0 = lambda x, y: jnp.concatenate([x, y], axis=0)
        cat1 = lambda x, y: jnp.concatenate([x, y], axis=1)

        ar = each(cat0, al, rt)
        bk = each(cat0, be, kt)
        amat = each(lambda x, y: jnp.where(mask4, _mm_nt(x, y), 0.0), ar, bk)
        sv = each(lambda s, y: cat0(bf(s.T), y), S32, vs)
        x = each(lambda l, m, rhs: _mm(cat1(l, bf(m[:, LANES:])), rhs), ar, amat, sv)
        tinv = each(lambda m: eye + m[:LANES, :LANES], amat)
        q = each(lambda m: bf(m[:LANES, :LANES]), amat)
        q = each(lambda z: bf(_mm(z, z)), q)
        for _ in range(n_sq - 1):
            y = each(lambda z, t: _mm(z, cat1(bf(t), z)), q, tinv)
            tinv = each(lambda t, z: t + z[:, :LANES], tinv, y)
            q = each(lambda z: bf(z[:, LANES:]), y)
        tinv = each(lambda t, z: t + _mm(z, bf(t)), tinv, q)
        cm = each(lambda t, z: bf(_mm(bf(t), bf(z[:LANES]))), tinv, x)
        o2 = each(lambda z, m, cc_: z[LANES:] + _mm(bf(m[LANES:, :LANES]), cc_), x, amat, cm)
        s_new = each(lambda s, cc_, y, rhs, e: (s + _mm_tn(cat0(cc_, y), rhs)) * e[C - 1:C, :],
                     S32, cm, vs, bk, e_b)
    else:
        ri = lax.broadcasted_iota(jnp.int32, (2 * C, 2 * C), 0)
        ci = lax.broadcasted_iota(jnp.int32, (2 * C, 2 * C), 1)
        same_head = (ri // C) == (ci // C)
        strict = same_head & ((ri % C) > (ci % C))
        incl = same_head & ((ri % C) >= (ci % C))
        eye = (ri == ci).astype(F32)
        S = each(bf, S32)
        a_ak = each(lambda x, y: bf(jnp.where(strict, _mm_nt(x, y), 0.0)), al, kt)
        a_rb = each(lambda x, y: bf(jnp.where(incl, _mm_nt(x, y), 0.0)), rt, be)
        a_rk = each(lambda x, y: bf(jnp.where(incl, _mm_nt(x, y), 0.0)), rt, kt)
        cm = each(lambda x, s, m, y: _mm_nt(x, s) + _mm(m, y), al, S, a_ak, vs)
        if n_valid > 1:
            pw = each(lambda x, y: jnp.where(strict, _mm_nt(x, y), 0.0), al, be)
            tinv = each(lambda x: eye + x, pw)
            pwb = each(bf, pw)
            for _ in range(n_sq):
                pwb = each(lambda x: bf(_mm(x, x)), pwb)
                tinv = each(lambda t, x: t + _mm(bf(t), x), tinv, pwb)
            cm = each(lambda t, x: _mm(bf(t), bf(x)), tinv, cm)
        cm = each(bf, cm)
        o2 = each(lambda x, s, m1, c1, m2, y: _mm_nt(x, s) + _mm(m1, c1) + _mm(m2, y),
                  rt, S, a_rb, cm, a_rk, vs)
        s_new = each(lambda s, c1, x, y, z, e: (s + _mm_tn(c1, x) + _mm_tn(y, z)) * e[C - 1:C, :],
                     S32, cm, be, vs, kt, e_b)

    for i in range(len(chains)):
        s_scr[i] = s_new[i]

    for i, (bi, sl) in enumerate(zip(bis, sls)):
        o = o2[i][:C] + o2[i][C:]
        mean = head_sum(o) * (1.0 / H)
        d = o - mean
        var = head_sum(d * d) * (1.0 / H)
        o = d * lax.rsqrt(var + RWKV_GN_EPS) * lnx_g[i] + lnx_b[i]
        o_ref[bi, :, sl] = o + head_sum(r[i] * k[i] * r_k[i]) * v[i]

    @pl.when(c == pl.num_programs(1) - 1)
    def _():
        for i, (bi, p) in enumerate(chains):
            so_ref[bi, 2 * p] = s_scr[i, 0:H, 0:H]
            so_ref[bi, 2 * p + 1] = s_scr[i, H:2 * H, H:2 * H]


def _rwkv_call(P, Pfirst, s0, layer, w2a, vw2, vecs, *, C, n_valid, BB):
    B, T, _ = P.shape
    has_vres = Pfirst is not None
    has_s0 = s0 is not None

    def cols(off, width):
        idx = off // width
        return pl.BlockSpec((BB, C, width), lambda b, c: (b, c, idx))

    whole = lambda shape: pl.BlockSpec(shape, lambda b, c: (0,) * len(shape))
    state_block = (BB, RWKV_HEADS, RWKV_HEAD_DIM, RWKV_HEAD_DIM)
    if has_s0:
        state_spec = pl.BlockSpec((None,) + state_block, lambda b, c: (layer, b, 0, 0, 0))
        state_shape = s0.shape
    else:
        state_spec = pl.BlockSpec(state_block, lambda b, c: (b, 0, 0, 0))
        state_shape = (B,) + state_block[1:]
    in_specs = [cols(COL_R, D_MODEL), cols(COL_K, D_MODEL), cols(COL_V, D_MODEL), cols(COL_LO, LANES)]
    args = [P, P, P, P]
    if has_vres:
        in_specs += [cols(COL_VLO, LANES), cols(COL_V, D_MODEL)]
        args += [P, Pfirst]
    aliases = {}
    if has_s0:
        aliases = {len(args): 1}
        in_specs.append(state_spec)
        args.append(s0)
    in_specs.append(whole((LANES, D_MODEL)))
    args.append(w2a)
    if has_vres:
        in_specs.append(whole((LANES, D_MODEL)))
        args.append(vw2)
    in_specs.append(whole((8, D_MODEL)))
    args.append(vecs)
    return pl.pallas_call(
        functools.partial(_rwkv_kernel, C=C, BB=BB, n_valid=n_valid, has_s0=has_s0, has_vres=has_vres),
        grid=(B // BB, T // C),
        in_specs=in_specs,
        out_specs=[pl.BlockSpec((BB, C, D_MODEL), lambda b, c: (b, c, 0)), state_spec],
        out_shape=[jax.ShapeDtypeStruct((B, T, D_MODEL), F32),
                   jax.ShapeDtypeStruct(state_shape, F32)],
        input_output_aliases=aliases,
        scratch_shapes=[pltpu.VMEM((BB * N_PAIRS, LANES, LANES), F32)],
        compiler_params=_cparams(("parallel", "arbitrary")),
        name="rwkv7_chunk",
    )(*args)


def _gla_kernel(*refs, C, BB, n_valid, has_s0):
    if has_s0:
        q_ref, k_ref, v_ref, glo_ref, s0_ref, ga2_ref, gab_ref, g_ref, o_ref, so_ref, s_scr = refs
    else:
        q_ref, k_ref, v_ref, glo_ref, ga2_ref, gab_ref, g_ref, o_ref, so_ref, s_scr = refs
    c = pl.program_id(1)
    chains = [(bi, h) for bi in range(BB) for h in range(GLA_HEADS)]

    @pl.when(c == 0)
    def _():
        if has_s0:
            for i, (bi, h) in enumerate(chains):
                s_scr[i] = s0_ref[bi, h].T
        else:
            s_scr[...] = jnp.zeros_like(s_scr)

    ri = lax.broadcasted_iota(jnp.int32, (C, C), 0)
    ci = lax.broadcasted_iota(jnp.int32, (C, C), 1)
    causal = ri >= ci
    trow = lax.broadcasted_iota(jnp.int32, (C, GLA_DK), 0)
    bis = [bi for bi, _ in chains]
    ks = [slice(h * GLA_DK, (h + 1) * GLA_DK) for _, h in chains]
    vsl = [slice(h * GLA_DV, (h + 1) * GLA_DV) for _, h in chains]
    v = [v_ref[bi, :, s] for bi, s in zip(bis, vsl)]
    k = [k_ref[bi, :, s] for bi, s in zip(bis, ks)]
    ga2 = ga2_ref[...]
    gate_rows = [_mm(glo_ref[bi], ga2) for bi in range(BB)]

    def cum_log_decay(bi, s):
        la = -_softplus(-(gate_rows[bi][:, s] + gab_ref[:, s])) * (1.0 / GLA_GATE_NORMALIZER)
        if n_valid < C:
            la = jnp.where(trow < n_valid, la, 0.0)
        return _cumsum_rows(la)

    b = [cum_log_decay(bi, s) for bi, s in zip(bis, ks)]
    qd = [q_ref[bi, :, s] * (GLA_DK ** -0.5) * jnp.exp(bh) for bi, s, bh in zip(bis, ks, b)]
    kd = [kh * jnp.exp(-bh) for kh, bh in zip(k, b)]
    kl = [kh * jnp.exp(bh[C - 1:C, :] - bh) for kh, bh in zip(k, b)]
    St = [s_scr[i] for i in range(len(chains))]
    att = [jnp.where(causal, _mm_nt(x, y), 0.0) for x, y in zip(qd, kd)]
    o_s = [_mm_nt(x, s) for x, s in zip(qd, St)]
    upd = [_mm_tn(x, y) for x, y in zip(v, kl)]
    o = [_mm(x, y) + z for x, y, z in zip(att, v, o_s)]
    for i, (bi, s) in enumerate(zip(bis, vsl)):
        s_scr[i] = St[i] * jnp.exp(b[i][C - 1:C, :]) + upd[i]
        o_ref[bi, :, s] = (o[i] * lax.rsqrt(jnp.mean(o[i] * o[i], axis=-1, keepdims=True)
                                            + GLA_NORM_EPS) * g_ref[...])

    @pl.when(c == pl.num_programs(1) - 1)
    def _():
        for i, (bi, h) in enumerate(chains):
            so_ref[bi, h] = s_scr[i].T


def _gla_call(P, s0, layer, ga2, gab, g, *, C, n_valid, BB):
    B, T, _ = P.shape
    has_s0 = s0 is not None
    kw = GLA_HEADS * GLA_DK
    vw = GLA_HEADS * GLA_DV
    state_block = (BB, GLA_HEADS, GLA_DK, GLA_DV)
    if has_s0:
        state_spec = pl.BlockSpec((None,) + state_block, lambda b, c: (layer, b, 0, 0, 0))
        state_shape = s0.shape
    else:
        state_spec = pl.BlockSpec(state_block, lambda b, c: (b, 0, 0, 0))
        state_shape = (B,) + state_block[1:]
    in_specs = [pl.BlockSpec((BB, C, kw), lambda b, c: (b, c, COL_GQ // kw)),
                pl.BlockSpec((BB, C, kw), lambda b, c: (b, c, COL_GK // kw)),
                pl.BlockSpec((BB, C, vw), lambda b, c: (b, c, COL_GV // vw)),
                pl.BlockSpec((BB, C, LANES), lambda b, c: (b, c, COL_GLO // LANES))]
    args = [P, P, P, P]
    aliases = {}
    if has_s0:
        aliases = {len(args): 1}
        in_specs.append(state_spec)
        args.append(s0)
    in_specs += [pl.BlockSpec((LANES, kw), lambda b, c: (0, 0)),
                 pl.BlockSpec((1, kw), lambda b, c: (0, 0)),
                 pl.BlockSpec((1, GLA_DV), lambda b, c: (0, 0))]
    args += [ga2, gab, g]
    return pl.pallas_call(
        functools.partial(_gla_kernel, C=C, BB=BB, n_valid=n_valid, has_s0=has_s0),
        grid=(B // BB, T // C),
        in_specs=in_specs,
        out_specs=[pl.BlockSpec((BB, C, vw), lambda b, c: (b, c, 0)), state_spec],
        out_shape=[jax.ShapeDtypeStruct((B, T, vw), F32),
                   jax.ShapeDtypeStruct(state_shape, F32)],
        input_output_aliases=aliases,
        scratch_shapes=[pltpu.VMEM((BB * GLA_HEADS, GLA_DV, GLA_DK), F32)],
        compiler_params=_cparams(("parallel", "arbitrary")),
        name="gla_chunk",
    )(*args)


def _wo_kernel(ga_ref, gb_ref, ro_ref, go_ref, h_ref, w_ref, o_ref):
    merged = _sigmoid(ga_ref[...]) * ro_ref[...] + _sigmoid(gb_ref[...]) * go_ref[...]
    o_ref[...] = h_ref[...] + _mm(merged.astype(BF16), w_ref[...])


def _wo_call(P, ro, go, h, w):
    m = h.shape[0]
    tm = min(512, m)
    row = lambda i: (i, 0)
    return pl.pallas_call(
        _wo_kernel,
        grid=(m // tm,),
        in_specs=[pl.BlockSpec((tm, D_MODEL), lambda i: (i, COL_GATE_A // D_MODEL)),
                  pl.BlockSpec((tm, D_MODEL), lambda i: (i, COL_GATE_B // D_MODEL)),
                  pl.BlockSpec((tm, D_MODEL), row),
                  pl.BlockSpec((tm, D_MODEL), row),
                  pl.BlockSpec((tm, D_MODEL), row),
                  pl.BlockSpec((D_MODEL, D_MODEL), lambda i: (0, 0))],
        out_specs=pl.BlockSpec((tm, D_MODEL), row),
        out_shape=jax.ShapeDtypeStruct((m, D_MODEL), F32),
        compiler_params=_cparams(("parallel",)),
        name="merge_wo",
    )(P, P, ro, go, h, w)


FFN_SUB = 256


def _ffn_kernel(h_ref, g_ref, wg_ref, wu_ref, wd_ref, gf_ref, o_ref, *, final_norm):
    h = h_ref[...]
    xs = _rms_rows(h, g_ref[...]).astype(BF16)
    out = h
    for n in range(FFN_HIDDEN // FFN_SUB):
        cs = slice(n * FFN_SUB, (n + 1) * FFN_SUB)
        gate = _mm(xs, wg_ref[:, cs])
        up = _mm(xs, wu_ref[:, cs])
        act = (gate * _sigmoid(gate) * up).astype(BF16)
        out = out + _mm(act, wd_ref[cs, :])
    if final_norm:
        out = _rms_rows(out, gf_ref[...])
    o_ref[...] = out


def _ffn_call(h, g, w_gu, w_down, gf, final_norm):
    m = h.shape[0]
    tm = min(512, m)
    row = lambda i: (i, 0)
    return pl.pallas_call(
        functools.partial(_ffn_kernel, final_norm=final_norm),
        grid=(m // tm,),
        in_specs=[pl.BlockSpec((tm, D_MODEL), row),
                  pl.BlockSpec((1, D_MODEL), lambda i: (0, 0)),
                  pl.BlockSpec((D_MODEL, FFN_HIDDEN), lambda i: (0, 0)),
                  pl.BlockSpec((D_MODEL, FFN_HIDDEN), lambda i: (0, 1)),
                  pl.BlockSpec((FFN_HIDDEN, D_MODEL), lambda i: (0, 0)),
                  pl.BlockSpec((1, D_MODEL), lambda i: (0, 0))],
        out_specs=pl.BlockSpec((tm, D_MODEL), row),
        out_shape=jax.ShapeDtypeStruct((m, D_MODEL), F32),
        compiler_params=_cparams(("parallel",)),
        name="ffn",
    )(h, g, w_gu, w_gu, w_down, gf)


def _rms_kernel(x_ref, g_ref, o_ref):
    o_ref[...] = _rms_rows(x_ref[...], g_ref[...])


def _rms_call(x, g):
    return pl.pallas_call(
        _rms_kernel,
        out_shape=jax.ShapeDtypeStruct(x.shape, F32),
        name="rmsnorm_rows",
    )(x, g)


SRC_GQ, SRC_GK, SRC_GV, SRC_GLO, SRC_GATE_A, SRC_GATE_B = 3200, 3712, 4224, 5248, 5264, 6288
WCAT_ROWS = 256


def _wcat_kernel(w_ref, v1_ref, o_ref):
    def put(dst, src, width):
        o_ref[:, dst:dst + width] = w_ref[:, src:src + width].astype(BF16)

    def zero(lo, hi):
        o_ref[:, lo:hi] = jnp.zeros((o_ref.shape[0], hi - lo), BF16)

    put(COL_R, 0, COL_LO + 2 * LORA_DECAY)
    o_ref[:, COL_VLO:COL_VLO + LORA_VRES] = v1_ref[...].astype(BF16)
    zero(COL_VLO + LORA_VRES, COL_GLO)
    put(COL_GLO, SRC_GLO, GLA_GATE_RANK)
    zero(COL_GLO + GLA_GATE_RANK, COL_GQ)
    put(COL_GQ, SRC_GQ, GLA_HEADS * GLA_DK)
    put(COL_GATE_A, SRC_GATE_A, D_MODEL)
    put(COL_GATE_B, SRC_GATE_B, D_MODEL)
    put(COL_GV, SRC_GV, GLA_HEADS * GLA_DV)
    put(COL_GK, SRC_GK, GLA_HEADS * GLA_DK)


def _wcat_call(w_in, vres_w1_padded):
    depth, _, n_in = w_in.shape
    return pl.pallas_call(
        _wcat_kernel,
        grid=(depth, D_MODEL // WCAT_ROWS),
        in_specs=[pl.BlockSpec((None, WCAT_ROWS, n_in), lambda l, i: (l, i, 0)),
                  pl.BlockSpec((None, WCAT_ROWS, LORA_VRES), lambda l, i: (l, i, 0))],
        out_specs=pl.BlockSpec((None, WCAT_ROWS, N_PROJ), lambda l, i: (l, i, 0)),
        out_shape=jax.ShapeDtypeStruct((depth, D_MODEL, N_PROJ), BF16),
        compiler_params=_cparams(("parallel", "parallel")),
        name="w_in_reorder",
    )(w_in, vres_w1_padded)


def _pad_rows(x, n):
    return jnp.pad(x, ((0, n - x.shape[0]), (0, 0)))


def _layer_weights(l, mu_shift, vres_mu, p):
    mu = mu_shift[l]
    zeros_mu = lambda n: jnp.zeros((n,), F32)
    vlo_mu = jnp.pad(vres_mu[l - 1], (0, LANES - LORA_VRES)) if l > 0 else zeros_mu(LANES)
    mu_cat = jnp.concatenate([mu[:COL_VLO], vlo_mu, zeros_mu(N_PROJ - N_SHIFTED)])[None, :]
    lw = dict(mu_cat=mu_cat)
    lw['w2a'] = jnp.concatenate([p['rwkv_w2'][l], p['rwkv_a2'][l]], axis=0)
    lw['vecs'] = jnp.stack([p['rwkv_w0'][l], p['rwkv_a0'][l], p['rwkv_k_k'][l], p['rwkv_k_a'][l],
                            p['rwkv_r_k'][l].reshape(-1), p['rwkv_lnx_g'][l], p['rwkv_lnx_b'][l],
                            p['vres_v0'][l - 1] if l > 0 else zeros_mu(D_MODEL)], axis=0)
    lw['vw2'] = _pad_rows(p['vres_w2'][l - 1], LANES) if l > 0 else None
    lw['ga2'] = _pad_rows(p['gla_a2'][l], LANES)
    lw['gab'] = p['gla_ab'][l][None, :]
    lw['gla_g'] = p['gla_norm_g'][l][None, :]
    lw['w_o'] = p['w_o'][l].astype(BF16)
    lw['w_gu'] = p['ffn_w_gu'][l].astype(BF16)
    lw['w_down'] = p['ffn_w_down'][l].astype(BF16)
    lw['g_mix'] = p['norm_mix'][l][None, :]
    lw['g_ffn'] = p['norm_ffn'][l][None, :]
    return lw


def _rows_per_step(batch, want):
    while batch % want:
        want //= 2
    return want


def _run_group(x, shift_state, wkv_state, gla_state, w_cat, weights, g_final):
    B, T, _ = x.shape
    fresh = shift_state is None
    C = CHUNK if fresh else SAMPLE_CHUNK
    h = x.reshape(B * T, D_MODEL)
    new_shift, new_wkv, new_gla = [], [], []
    P_first = None
    depth = len(weights)
    for l, lw in enumerate(weights):
        new_shift.append(_rms_call(h.reshape(B, T, D_MODEL)[:, -1], lw['g_mix']))
        if fresh:
            P = _proj_call(h, lw['g_mix'], w_cat, l, lw['mu_cat'], None, T).reshape(B, T, N_PROJ)
        else:
            P = _proj_call(h, lw['g_mix'], w_cat, l, lw['mu_cat'], shift_state[l], T)
            P = jnp.pad(P[:, None, :], ((0, 0), (0, C - T), (0, 0)))
        Tp = P.shape[1]
        ro, s_wkv = _rwkv_call(P, P_first if l > 0 else None, wkv_state, l,
                               lw['w2a'], lw['vw2'], lw['vecs'], C=C, n_valid=min(T, C),
                               BB=_rows_per_step(B, RWKV_ROWS_PER_STEP))
        go, s_gla = _gla_call(P, gla_state, l, lw['ga2'], lw['gab'], lw['gla_g'],
                              C=C, n_valid=min(T, C), BB=_rows_per_step(B, GLA_ROWS_PER_STEP))
        if not fresh:
            wkv_state, gla_state = s_wkv, s_gla
        if l == 0:
            P_first = P
        if Tp != T:
            ro, go = ro[:, :T], go[:, :T]
            P2 = P[:, :T].reshape(B * T, N_PROJ)
        else:
            P2 = P.reshape(B * T, N_PROJ)
        h = _wo_call(P2, ro.reshape(B * T, D_MODEL), go.reshape(B * T, D_MODEL), h, lw['w_o'])
        h = _ffn_call(h, lw['g_ffn'], lw['w_gu'], lw['w_down'], g_final, final_norm=(l == depth - 1))
        new_wkv.append(s_wkv)
        new_gla.append(s_gla)
    if fresh:
        wkv_state, gla_state = jnp.stack(new_wkv), jnp.stack(new_gla)
    return (h.reshape(B, T, D_MODEL), jnp.stack(new_shift), wkv_state, gla_state)


def kernel(x_prompt, x_sample, state_shift, state_wkv, state_gla, norm_mix, w_in, mu_shift, rwkv_w0, rwkv_w2, rwkv_a0, rwkv_a2, rwkv_k_k, rwkv_k_a, rwkv_r_k, rwkv_lnx_g, rwkv_lnx_b, vres_w1, vres_mu, vres_w2, vres_v0, gla_a2, gla_ab, gla_norm_g, w_o, norm_ffn, ffn_w_gu, ffn_w_down, norm_final):
    p = dict(norm_mix=norm_mix, rwkv_w0=rwkv_w0, rwkv_w2=rwkv_w2, rwkv_a0=rwkv_a0, rwkv_a2=rwkv_a2,
             rwkv_k_k=rwkv_k_k, rwkv_k_a=rwkv_k_a, rwkv_r_k=rwkv_r_k, rwkv_lnx_g=rwkv_lnx_g,
             rwkv_lnx_b=rwkv_lnx_b, vres_w2=vres_w2, vres_v0=vres_v0, gla_a2=gla_a2, gla_ab=gla_ab,
             gla_norm_g=gla_norm_g, w_o=w_o, norm_ffn=norm_ffn, ffn_w_gu=ffn_w_gu, ffn_w_down=ffn_w_down)
    depth = w_in.shape[0]
    weights = [_layer_weights(l, mu_shift, vres_mu, p) for l in range(depth)]
    w_cat = _wcat_call(w_in, jnp.concatenate([jnp.zeros_like(vres_w1[:1]), vres_w1], axis=0))
    g_final = norm_final[None, :]
    y_p, shift_p, wkv_p, gla_p = _run_group(x_prompt, None, None, None, w_cat, weights, g_final)
    y_s, shift_s, wkv_s, gla_s = _run_group(x_sample, state_shift, state_wkv, state_gla, w_cat, weights,
                                            g_final)
    return (y_p, y_s, shift_p, wkv_p, gla_p, shift_s, wkv_s, gla_s)
```

```python
import functools

import jax
import jax.numpy as jnp
from jax import lax
from jax.experimental import pallas as pl
from jax.experimental.pallas import tpu as pltpu

F32 = jnp.float32
BF16 = jnp.bfloat16

LANES = 128
D_MODEL = 1024
RWKV_HEAD_DIM = 64
RWKV_HEADS = D_MODEL // RWKV_HEAD_DIM
HEADS_PER_PAIR = LANES // RWKV_HEAD_DIM
N_PAIRS = RWKV_HEADS // HEADS_PER_PAIR
LORA_DECAY = 64
LORA_ICLR = 64
LORA_VRES = 32
RWKV_GN_EPS = 64e-5
GLA_HEADS = 4
GLA_DK = 128
GLA_DV = 256
GLA_GATE_RANK = 16
GLA_GATE_NORMALIZER = 16.0
GLA_NORM_EPS = 1e-5
FFN_HIDDEN = 2816
NORM_EPS = 1e-6
CHUNK = 64
SAMPLE_CHUNK = 8

COL_R, COL_K, COL_V = 0, 1024, 2048
COL_LO, COL_VLO, COL_GLO = 3072, 3200, 3328
COL_GQ = 3584
COL_GATE_A, COL_GATE_B = 4096, 5120
COL_GV, COL_GK = 6144, 7168
N_SHIFTED = 3328
N_PROJ = 7680
PROJ_TN = 1536
PROJ_SUB = 512
VMEM_LIMIT = 48 * 1024 * 1024
ROWS_PER_STEP = 2
DECAY_SCALE = 0.6065306597126334


def _cparams(sem):
    return pltpu.CompilerParams(dimension_semantics=sem, vmem_limit_bytes=VMEM_LIMIT)


def _mm(a, b):
    return jnp.dot(a, b, preferred_element_type=F32)


def _mm_nt(a, b):
    return lax.dot_general(a, b, (((1,), (1,)), ((), ())), preferred_element_type=F32)


def _mm_tn(a, b):
    return lax.dot_general(a, b, (((0,), (0,)), ((), ())), preferred_element_type=F32)


def _sigmoid(x):
    return 0.5 * jnp.tanh(0.5 * x) + 0.5


def _softplus(x):
    return jnp.maximum(x, 0.0) + jnp.log(1.0 + jnp.exp(-jnp.abs(x)))


def _cumsum_rows(x):
    n = x.shape[0]
    row = lax.broadcasted_iota(jnp.int32, x.shape, 0)
    s = 1
    while s < n:
        x = x + jnp.where(row >= s, pltpu.roll(x, s, 0), 0.0)
        s *= 2
    return x


def _rms_rows(x, g):
    ms = jnp.mean(x * x, axis=-1, keepdims=True)
    return x * lax.rsqrt(ms + NORM_EPS) * g


def _proj_kernel(*refs, sample, tiles_per_seq, tm, n_shift_tiles):
    if sample:
        x_ref, g_ref, w_ref, mu_ref, prev_ref, o_ref, xs_ref, ps_ref = refs
    else:
        x_ref, g_ref, w_ref, mu_ref, o_ref, xs_ref, last_ref = refs
    i = pl.program_id(0)
    j = pl.program_id(1)

    @pl.when(j == 0)
    def _():
        xs_ref[...] = _rms_rows(x_ref[...], g_ref[...]).astype(BF16)
        if sample:
            ps_ref[...] = prev_ref[...].astype(BF16)

    xs = xs_ref[...]
    subs = [slice(n * PROJ_SUB, (n + 1) * PROJ_SUB) for n in range(PROJ_TN // PROJ_SUB)]
    if sample:
        ps = ps_ref[...]
        for cs in subs:
            y = _mm(xs, w_ref[:, cs])
            o_ref[:, cs] = y + mu_ref[:, cs] * (_mm(ps, w_ref[:, cs]) - y)
        return

    @pl.when(j < n_shift_tiles)
    def _():
        @pl.when(i == 0)
        def _():
            last_ref[j] = jnp.zeros((8, PROJ_TN), F32)

        row = lax.broadcasted_iota(jnp.int32, (tm, PROJ_SUB), 0)
        for cs in subs:
            y = _mm(xs, w_ref[:, cs])
            prev_row = jnp.where(i % tiles_per_seq == 0, 0.0, last_ref[j, 7:8, cs])
            shifted = jnp.where(row == 0, prev_row, pltpu.roll(y, 1, 0))
            last_ref[j, :, cs] = y[tm - 8:, :]
            o_ref[:, cs] = y + mu_ref[:, cs] * (shifted - y)

    @pl.when(j >= n_shift_tiles)
    def _():
        for cs in subs:
            o_ref[:, cs] = _mm(xs, w_ref[:, cs])


def _proj_call(x, g, w, layer, mu, prev, seq_len):
    m = x.shape[0]
    sample = prev is not None
    tm = min(512, m if sample else seq_len)
    nj = N_PROJ // PROJ_TN
    n_shift_tiles = -(-N_SHIFTED // PROJ_TN)
    in_specs = [
        pl.BlockSpec((tm, D_MODEL), lambda i, j: (i, 0)),
        pl.BlockSpec((1, D_MODEL), lambda i, j: (0, 0)),
        pl.BlockSpec((None, D_MODEL, PROJ_TN), lambda i, j: (layer, 0, j)),
        pl.BlockSpec((1, PROJ_TN), lambda i, j: (0, j)),
    ]
    args = [x, g, w, mu]
    scratch = [pltpu.VMEM((tm, D_MODEL), BF16)]
    if sample:
        in_specs.append(pl.BlockSpec((tm, D_MODEL), lambda i, j: (i, 0)))
        args.append(prev)
        scratch.append(pltpu.VMEM((tm, D_MODEL), BF16))
    else:
        scratch.append(pltpu.VMEM((n_shift_tiles, 8, PROJ_TN), F32))
    return pl.pallas_call(
        functools.partial(_proj_kernel, sample=sample, tiles_per_seq=max(seq_len // tm, 1), tm=tm,
                          n_shift_tiles=n_shift_tiles),
        grid=(m // tm, nj),
        in_specs=in_specs,
        out_specs=pl.BlockSpec((tm, PROJ_TN), lambda i, j: (i, j)),
        out_shape=jax.ShapeDtypeStruct((m, N_PROJ), F32),
        scratch_shapes=scratch,
        compiler_params=_cparams(("arbitrary", "arbitrary")),
        name="proj_sample" if sample else "proj_prompt",
    )(*args)


class _Shared:
    pass


def _rwkv_row_stages(bi, R, res, *, C, n_valid, has_vres):
    H = RWKV_HEAD_DIM
    head0, trow = R.head0, R.trow
    sls = [slice(p * LANES, (p + 1) * LANES) for p in range(N_PAIRS)]
    scr = [bi * N_PAIRS + p for p in range(N_PAIRS)]
    n_sq = C.bit_length() - 2
    fused = 2 * C == LANES
    st = {}
    bf = lambda x: x.astype(BF16)
    cat0 = lambda x, y: jnp.concatenate([x, y], axis=0)
    cat1 = lambda x, y: jnp.concatenate([x, y], axis=1)

    def each(f, *lists):
        return [f(*xs) for xs in zip(*lists)]

    def stack(x):
        return jnp.concatenate([jnp.where(head0, x, 0.0), jnp.where(head0, 0.0, x)], axis=0)

    def head_sum(x):
        s0 = jnp.sum(jnp.where(head0, x, 0.0), axis=-1, keepdims=True)
        s1 = jnp.sum(jnp.where(head0, 0.0, x), axis=-1, keepdims=True)
        return jnp.where(head0, s0, s1)

    def lora():
        lo = R.lo_ref[bi]
        w2a = R.w2a_ref[...]
        st['wl'] = _mm(jnp.where(head0, jnp.tanh(lo), 0.0), w2a)
        st['al'] = _mm(jnp.where(head0, 0.0, lo), w2a)
        if has_vres:
            st['vg'] = _mm(R.vlo_ref[bi], R.vw2_ref[...])

    def operands():
        vec = lambda row: [R.vec_ref[row:row + 1, sl] for sl in sls]
        w0, a0, k_k, k_a, r_k, lnx_g, lnx_b, v0 = (vec(i) for i in range(8))
        r = [R.r_ref[bi, :, sl] for sl in sls]
        k = [R.k_ref[bi, :, sl] for sl in sls]
        v = [R.v_ref[bi, :, sl] for sl in sls]
        if has_vres:
            v = [vp + (R.vf_ref[bi, :, sl] - vp) * _sigmoid(z + st['vg'][:, sl])
                 for vp, sl, z in zip(v, sls, v0)]

        def decay_log(w0p, sl):
            ld = -DECAY_SCALE * _sigmoid(w0p + st['wl'][:, sl])
            if n_valid < C:
                ld = jnp.where(trow < n_valid, ld, 0.0)
            return ld

        ld = each(decay_log, w0, sls)
        a = each(lambda z, sl: _sigmoid(z + st['al'][:, sl]), a0, sls)

        def unit_kk(kp, kkp):
            kk = kp * kkp
            return kk * lax.rsqrt(jnp.maximum(head_sum(kk * kk), 1e-24))

        kk = each(unit_kk, k, k_k)
        k = each(lambda kp, ap, kap: kp * (1.0 + (ap - 1.0) * kap), k, a, k_a)
        b = each(_cumsum_rows, ld)
        e_b = each(jnp.exp, b)
        e_nb = each(lambda x: jnp.exp(-x), b)
        st['al_s'] = each(lambda kkp, bp, ldp: bf(stack(-kkp * jnp.exp(bp - ldp))), kk, b, ld)
        st['be'] = each(lambda ap, kkp, e: bf(stack(ap * kkp * e)), a, kk, e_nb)
        st['kt'] = each(lambda kp, e: bf(stack(kp * e)), k, e_nb)
        st['rt'] = each(lambda rp, e: bf(stack(rp * e)), r, e_b)
        st['vs'] = each(lambda x: bf(stack(x)), v)
        st['decay'] = [e[C - 1:C, :] for e in e_b]
        st['S32'] = [R.s_scr[i] for i in scr]
        st['epi'] = (r, k, v, r_k, lnx_g, lnx_b)

    def finish(o2, s_new):
        for i, s in zip(scr, s_new):
            R.s_scr[i] = s
        r, k, v, r_k, lnx_g, lnx_b = st['epi']
        outs = []
        for p in range(N_PAIRS):
            o = o2[p][:C] + o2[p][C:]
            mean = head_sum(o) * (1.0 / H)
            d = o - mean
            var = head_sum(d * d) * (1.0 / H)
            o = d * lax.rsqrt(var + RWKV_GN_EPS) * lnx_g[p] + lnx_b[p]
            outs.append(o + head_sum(r[p] * k[p] * r_k[p]) * v[p])
        res[bi] = outs

    if fused:
        def amat():
            operands()
            st['ar'] = each(cat0, st['al_s'], st['rt'])
            st['bk'] = each(cat0, st['be'], st['kt'])
            st['amat'] = each(lambda x, y: jnp.where(R.mask4, _mm_nt(x, y), 0.0), st['ar'], st['bk'])

        def xq():
            sv = each(lambda s, y: cat0(bf(s.T), y), st['S32'], st['vs'])
            st['x'] = each(lambda l, m, rhs: _mm(cat1(l, bf(m[:, LANES:])), rhs), st['ar'], st['amat'], sv)
            st['tinv'] = each(lambda m: R.eye + m[:LANES, :LANES], st['amat'])
            q = each(lambda m: bf(m[:LANES, :LANES]), st['amat'])
            st['q'] = each(lambda z: bf(_mm(z, z)), q)

        def double():
            y = each(lambda z, t: _mm(z, cat1(bf(t), z)), st['q'], st['tinv'])
            st['tinv'] = each(lambda t, z: t + z[:, :LANES], st['tinv'], y)
            st['q'] = each(lambda z: bf(z[:, LANES:]), y)

        def double_last():
            st['tinv'] = each(lambda t, z: t + _mm(z, bf(t)), st['tinv'], st['q'])

        def solve():
            st['cm'] = each(lambda t, z: bf(_mm(bf(t), bf(z[:LANES]))), st['tinv'], st['x'])

        def out_state():
            o2 = each(lambda z, m, cc: z[LANES:] + _mm(bf(m[LANES:, :LANES]), cc),
                      st['x'], st['amat'], st['cm'])
            s_new = each(lambda s, cc, y, rhs, e: (s + _mm_tn(cat0(cc, y), rhs)) * e,
                         st['S32'], st['cm'], st['vs'], st['bk'], st['decay'])
            finish(o2, s_new)

        return [lora, amat, xq] + [double] * (n_sq - 1) + [double_last, solve, out_state]

    def small_a():
        operands()
        S = each(bf, st['S32'])
        st['S'] = S
        a_ak = each(lambda x, y: bf(jnp.where(R.strict, _mm_nt(x, y), 0.0)), st['al_s'], st['kt'])
        st['a_rb'] = each(lambda x, y: bf(jnp.where(R.incl, _mm_nt(x, y), 0.0)), st['rt'], st['be'])
        st['a_rk'] = each(lambda x, y: bf(jnp.where(R.incl, _mm_nt(x, y), 0.0)), st['rt'], st['kt'])
        st['cm'] = each(lambda x, s, m, y: _mm_nt(x, s) + _mm(m, y), st['al_s'], S, a_ak, st['vs'])

    def small_solve():
        pw = each(lambda x, y: jnp.where(R.strict, _mm_nt(x, y), 0.0), st['al_s'], st['be'])
        tinv = each(lambda x: R.eye + x, pw)
        pwb = each(bf, pw)
        for _ in range(n_sq):
            pwb = each(lambda x: bf(_mm(x, x)), pwb)
            tinv = each(lambda t, x: t + _mm(bf(t), x), tinv, pwb)
        st['cm'] = each(lambda t, x: _mm(bf(t), bf(x)), tinv, st['cm'])

    def small_out():
        cm = each(bf, st['cm'])
        o2 = each(lambda x, s, m1, c1, m2, y: _mm_nt(x, s) + _mm(m1, c1) + _mm(m2, y),
                  st['rt'], st['S'], st['a_rb'], cm, st['a_rk'], st['vs'])
        s_new = each(lambda s, c1, x, y, z, e: (s + _mm_tn(c1, x) + _mm_tn(y, z)) * e,
                     st['S32'], cm, st['be'], st['vs'], st['kt'], st['decay'])
        finish(o2, s_new)

    return [lora, small_a] + ([small_solve] if n_valid > 1 else []) + [small_out]


def _gla_row_stages(bi, R, res, *, C, n_valid):
    heads = range(GLA_HEADS)
    ks = [slice(h * GLA_DK, (h + 1) * GLA_DK) for h in heads]
    vsl = [slice(h * GLA_DV, (h + 1) * GLA_DV) for h in heads]
    scr = [bi * GLA_HEADS + h for h in heads]
    st = {}

    def gate():
        st['gate'] = _mm(R.glo_ref[bi], R.ga2_ref[...])

    def scores():
        def cum_log_decay(s):
            la = -_softplus(-(st['gate'][:, s] + R.gab_ref[:, s])) * (1.0 / GLA_GATE_NORMALIZER)
            if n_valid < C:
                la = jnp.where(R.trow < n_valid, la, 0.0)
            return _cumsum_rows(la)

        b = [cum_log_decay(s) for s in ks]
        k = [R.gk_ref[bi, :, s] for s in ks]
        st['v'] = [R.gv_ref[bi, :, s] for s in vsl]
        qd = [R.gq_ref[bi, :, s] * (GLA_DK ** -0.5) * jnp.exp(bh) for s, bh in zip(ks, b)]
        kd = [kh * jnp.exp(-bh) for kh, bh in zip(k, b)]
        kl = [kh * jnp.exp(bh[C - 1:C, :] - bh) for kh, bh in zip(k, b)]
        St = [R.g_scr[i] for i in scr]
        st['att'] = [jnp.where(R.causal, _mm_nt(x, y), 0.0) for x, y in zip(qd, kd)]
        st['o_s'] = [_mm_nt(x, s) for x, s in zip(qd, St)]
        upd = [_mm_tn(x, y) for x, y in zip(st['v'], kl)]
        for i, s_old, bh, u in zip(scr, St, b, upd):
            R.g_scr[i] = s_old * jnp.exp(bh[C - 1:C, :]) + u

    def outputs():
        o = [_mm(x, y) + z for x, y, z in zip(st['att'], st['v'], st['o_s'])]
        res[bi] = [oh * lax.rsqrt(jnp.mean(oh * oh, axis=-1, keepdims=True) + GLA_NORM_EPS) * R.gg_ref[...]
                   for oh in o]

    return [gate, scores, outputs]


def _mixer_kernel(*refs, C, BB, n_valid, has_s0, has_vres):
    R = _Shared()
    it = iter(refs)
    R.r_ref, R.k_ref, R.v_ref, R.lo_ref = next(it), next(it), next(it), next(it)
    if has_vres:
        R.vlo_ref, R.vf_ref = next(it), next(it)
    R.gq_ref, R.gk_ref, R.gv_ref, R.glo_ref = next(it), next(it), next(it), next(it)
    ga_ref, gb_ref, h_ref = next(it), next(it), next(it)
    if has_s0:
        s0w_ref, s0g_ref = next(it), next(it)
    R.w2a_ref = next(it)
    if has_vres:
        R.vw2_ref = next(it)
    R.vec_ref, R.ga2_ref, R.gab_ref, R.gg_ref, wo_ref = next(it), next(it), next(it), next(it), next(it)
    o_ref, sow_ref, sog_ref, R.s_scr, R.g_scr = next(it), next(it), next(it), next(it), next(it)

    c = pl.program_id(1)
    H = RWKV_HEAD_DIM
    rows = range(BB)

    @pl.when(c == 0)
    def _():
        R.s_scr[...] = jnp.zeros_like(R.s_scr)
        if has_s0:
            for bi in rows:
                for p in range(N_PAIRS):
                    R.s_scr[bi * N_PAIRS + p, 0:H, 0:H] = s0w_ref[bi, 2 * p]
                    R.s_scr[bi * N_PAIRS + p, H:2 * H, H:2 * H] = s0w_ref[bi, 2 * p + 1]
                for h in range(GLA_HEADS):
                    R.g_scr[bi * GLA_HEADS + h] = s0g_ref[bi, h].T
        else:
            R.g_scr[...] = jnp.zeros_like(R.g_scr)

    R.head0 = lax.broadcasted_iota(jnp.int32, (C, LANES), 1) < H
    R.trow = lax.broadcasted_iota(jnp.int32, (C, LANES), 0)
    ri = lax.broadcasted_iota(jnp.int32, (C, C), 0)
    ci = lax.broadcasted_iota(jnp.int32, (C, C), 1)
    R.causal = ri >= ci
    if 2 * C == LANES:
        ri = lax.broadcasted_iota(jnp.int32, (2 * LANES, 2 * LANES), 0)
        ci = lax.broadcasted_iota(jnp.int32, (2 * LANES, 2 * LANES), 1)
        rr, cc = ri % LANES, ci % LANES
        R.mask4 = ((rr // C) == (cc // C)) & ((rr % C) >= (cc % C) + (ri < LANES).astype(jnp.int32))
        r1 = lax.broadcasted_iota(jnp.int32, (LANES, LANES), 0)
        c1 = lax.broadcasted_iota(jnp.int32, (LANES, LANES), 1)
        R.eye = (r1 == c1).astype(F32)
    else:
        ri = lax.broadcasted_iota(jnp.int32, (2 * C, 2 * C), 0)
        ci = lax.broadcasted_iota(jnp.int32, (2 * C, 2 * C), 1)
        same_head = (ri // C) == (ci // C)
        R.strict = same_head & ((ri % C) > (ci % C))
        R.incl = same_head & ((ri % C) >= (ci % C))
        R.eye = (ri == ci).astype(F32)

    ro, go = {}, {}
    plans = []
    for bi in rows:
        plans.append(_rwkv_row_stages(bi, R, ro, C=C, n_valid=n_valid, has_vres=has_vres))
        plans.append(_gla_row_stages(bi, R, go, C=C, n_valid=n_valid))
    for d in range(max(len(stages) for stages in plans)):
        for stages in plans:
            if d < len(stages):
                stages[d]()

    merged = [_sigmoid(ga_ref[bi]) * jnp.concatenate(ro[bi], axis=1)
              + _sigmoid(gb_ref[bi]) * jnp.concatenate(go[bi], axis=1) for bi in rows]
    mix = _mm(jnp.concatenate(merged, axis=0).astype(BF16), wo_ref[...])
    for bi in rows:
        o_ref[bi] = h_ref[bi] + mix[bi * C:(bi + 1) * C]

    @pl.when(c == pl.num_programs(1) - 1)
    def _():
        for bi in rows:
            for p in range(N_PAIRS):
                sow_ref[bi, 2 * p] = R.s_scr[bi * N_PAIRS + p, 0:H, 0:H]
                sow_ref[bi, 2 * p + 1] = R.s_scr[bi * N_PAIRS + p, H:2 * H, H:2 * H]
            for h in range(GLA_HEADS):
                sog_ref[bi, h] = R.g_scr[bi * GLA_HEADS + h].T


def _mixer_call(P, Pfirst, h, s_wkv, s_gla, layer, lw, *, C, n_valid, BB):
    B, T, _ = P.shape
    has_vres = Pfirst is not None
    has_s0 = s_wkv is not None
    kw = GLA_HEADS * GLA_DK

    def cols(off, width):
        idx = off // width
        return pl.BlockSpec((BB, C, width), lambda b, c: (b, c, idx))

    whole = lambda shape: pl.BlockSpec(shape, lambda b, c: (0,) * len(shape))
    wkv_block = (BB, RWKV_HEADS, RWKV_HEAD_DIM, RWKV_HEAD_DIM)
    gla_block = (BB, GLA_HEADS, GLA_DK, GLA_DV)
    if has_s0:
        state_spec = lambda blk: pl.BlockSpec((None,) + blk, lambda b, c: (layer, b, 0, 0, 0))
        wkv_shape, gla_shape = s_wkv.shape, s_gla.shape
    else:
        state_spec = lambda blk: pl.BlockSpec(blk, lambda b, c: (b, 0, 0, 0))
        wkv_shape, gla_shape = (B,) + wkv_block[1:], (B,) + gla_block[1:]

    in_specs = [cols(COL_R, D_MODEL), cols(COL_K, D_MODEL), cols(COL_V, D_MODEL), cols(COL_LO, LANES)]
    args = [P, P, P, P]
    if has_vres:
        in_specs += [cols(COL_VLO, LANES), cols(COL_V, D_MODEL)]
        args += [P, Pfirst]
    in_specs += [cols(COL_GQ, kw), cols(COL_GK, kw), cols(COL_GV, D_MODEL), cols(COL_GLO, LANES),
                 cols(COL_GATE_A, D_MODEL), cols(COL_GATE_B, D_MODEL), cols(0, D_MODEL)]
    args += [P, P, P, P, P, P, h]
    aliases = {}
    if has_s0:
        aliases = {len(args): 1, len(args) + 1: 2}
        in_specs += [state_spec(wkv_block), state_spec(gla_block)]
        args += [s_wkv, s_gla]
    in_specs.append(whole((LANES, D_MODEL)))
    args.append(lw['w2a'])
    if has_vres:
        in_specs.append(whole((LANES, D_MODEL)))
        args.append(lw['vw2'])
    in_specs += [whole((8, D_MODEL)), whole((LANES, kw)), whole((1, kw)), whole((1, GLA_DV)),
                 whole((D_MODEL, D_MODEL))]
    args += [lw['vecs'], lw['ga2'], lw['gab'], lw['gla_g'], lw['w_o']]
    return pl.pallas_call(
        functools.partial(_mixer_kernel, C=C, BB=BB, n_valid=n_valid, has_s0=has_s0, has_vres=has_vres),
        grid=(B // BB, T // C),
        in_specs=in_specs,
        out_specs=[pl.BlockSpec((BB, C, D_MODEL), lambda b, c: (b, c, 0)),
                   state_spec(wkv_block), state_spec(gla_block)],
        out_shape=[jax.ShapeDtypeStruct((B, T, D_MODEL), F32),
                   jax.ShapeDtypeStruct(wkv_shape, F32),
                   jax.ShapeDtypeStruct(gla_shape, F32)],
        input_output_aliases=aliases,
        scratch_shapes=[pltpu.VMEM((BB * N_PAIRS, LANES, LANES), F32),
                        pltpu.VMEM((BB * GLA_HEADS, GLA_DV, GLA_DK), F32)],
        compiler_params=_cparams(("parallel", "arbitrary")),
        name="mixer",
    )(*args)


FFN_SUB = 256


def _ffn_kernel(h_ref, g_ref, wg_ref, wu_ref, wd_ref, gf_ref, o_ref, *, final_norm):
    h = h_ref[...]
    xs = _rms_rows(h, g_ref[...]).astype(BF16)
    out = h
    for n in range(FFN_HIDDEN // FFN_SUB):
        cs = slice(n * FFN_SUB, (n + 1) * FFN_SUB)
        gate = _mm(xs, wg_ref[:, cs])
        up = _mm(xs, wu_ref[:, cs])
        act = (gate * _sigmoid(gate) * up).astype(BF16)
        out = out + _mm(act, wd_ref[cs, :])
    if final_norm:
        out = _rms_rows(out, gf_ref[...])
    o_ref[...] = out


def _ffn_call(h, g, w_gu, w_down, gf, final_norm):
    m = h.shape[0]
    tm = min(512, m)
    row = lambda i: (i, 0)
    return pl.pallas_call(
        functools.partial(_ffn_kernel, final_norm=final_norm),
        grid=(m // tm,),
        in_specs=[pl.BlockSpec((tm, D_MODEL), row),
                  pl.BlockSpec((1, D_MODEL), lambda i: (0, 0)),
                  pl.BlockSpec((D_MODEL, FFN_HIDDEN), lambda i: (0, 0)),
                  pl.BlockSpec((D_MODEL, FFN_HIDDEN), lambda i: (0, 1)),
                  pl.BlockSpec((FFN_HIDDEN, D_MODEL), lambda i: (0, 0)),
                  pl.BlockSpec((1, D_MODEL), lambda i: (0, 0))],
        out_specs=pl.BlockSpec((tm, D_MODEL), row),
        out_shape=jax.ShapeDtypeStruct((m, D_MODEL), F32),
        compiler_params=_cparams(("parallel",)),
        name="ffn",
    )(h, g, w_gu, w_gu, w_down, gf)


def _rms_kernel(x_ref, g_ref, o_ref):
    o_ref[...] = _rms_rows(x_ref[...], g_ref[...])


def _rms_call(x, g):
    return pl.pallas_call(
        _rms_kernel,
        out_shape=jax.ShapeDtypeStruct(x.shape, F32),
        name="rmsnorm_rows",
    )(x, g)


SRC_GQ, SRC_GK, SRC_GV, SRC_GLO, SRC_GATE_A, SRC_GATE_B = 3200, 3712, 4224, 5248, 5264, 6288
WCAT_ROWS = 256


def _wcat_kernel(w_ref, v1_ref, o_ref):
    def put(dst, src, width):
        o_ref[:, dst:dst + width] = w_ref[:, src:src + width].astype(BF16)

    def zero(lo, hi):
        o_ref[:, lo:hi] = jnp.zeros((o_ref.shape[0], hi - lo), BF16)

    put(COL_R, 0, COL_LO + 2 * LORA_DECAY)
    o_ref[:, COL_VLO:COL_VLO + LORA_VRES] = v1_ref[...].astype(BF16)
    zero(COL_VLO + LORA_VRES, COL_GLO)
    put(COL_GLO, SRC_GLO, GLA_GATE_RANK)
    zero(COL_GLO + GLA_GATE_RANK, COL_GQ)
    put(COL_GQ, SRC_GQ, GLA_HEADS * GLA_DK)
    put(COL_GATE_A, SRC_GATE_A, D_MODEL)
    put(COL_GATE_B, SRC_GATE_B, D_MODEL)
    put(COL_GV, SRC_GV, GLA_HEADS * GLA_DV)
    put(COL_GK, SRC_GK, GLA_HEADS * GLA_DK)


def _wcat_call(w_in, vres_w1_padded):
    depth, _, n_in = w_in.shape
    return pl.pallas_call(
        _wcat_kernel,
        grid=(depth, D_MODEL // WCAT_ROWS),
        in_specs=[pl.BlockSpec((None, WCAT_ROWS, n_in), lambda l, i: (l, i, 0)),
                  pl.BlockSpec((None, WCAT_ROWS, LORA_VRES), lambda l, i: (l, i, 0))],
        out_specs=pl.BlockSpec((None, WCAT_ROWS, N_PROJ), lambda l, i: (l, i, 0)),
        out_shape=jax.ShapeDtypeStruct((depth, D_MODEL, N_PROJ), BF16),
        compiler_params=_cparams(("parallel", "parallel")),
        name="w_in_reorder",
    )(w_in, vres_w1_padded)


def _pad_rows(x, n):
    return jnp.pad(x, ((0, n - x.shape[0]), (0, 0)))


def _layer_weights(l, mu_shift, vres_mu, p):
    mu = mu_shift[l]
    zeros_mu = lambda n: jnp.zeros((n,), F32)
    vlo_mu = jnp.pad(vres_mu[l - 1], (0, LANES - LORA_VRES)) if l > 0 else zeros_mu(LANES)
    mu_cat = jnp.concatenate([mu[:COL_VLO], vlo_mu, zeros_mu(N_PROJ - N_SHIFTED)])[None, :]
    lw = dict(mu_cat=mu_cat)
    lw['w2a'] = jnp.concatenate([p['rwkv_w2'][l], p['rwkv_a2'][l]], axis=0)
    lw['vecs'] = jnp.stack([p['rwkv_w0'][l], p['rwkv_a0'][l], p['rwkv_k_k'][l], p['rwkv_k_a'][l],
                            p['rwkv_r_k'][l].reshape(-1), p['rwkv_lnx_g'][l], p['rwkv_lnx_b'][l],
                            p['vres_v0'][l - 1] if l > 0 else zeros_mu(D_MODEL)], axis=0)
    lw['vw2'] = _pad_rows(p['vres_w2'][l - 1], LANES) if l > 0 else None
    lw['ga2'] = _pad_rows(p['gla_a2'][l], LANES)
    lw['gab'] = p['gla_ab'][l][None, :]
    lw['gla_g'] = p['gla_norm_g'][l][None, :]
    lw['w_o'] = p['w_o'][l].astype(BF16)
    lw['w_gu'] = p['ffn_w_gu'][l].astype(BF16)
    lw['w_down'] = p['ffn_w_down'][l].astype(BF16)
    lw['g_mix'] = p['norm_mix'][l][None, :]
    lw['g_ffn'] = p['norm_ffn'][l][None, :]
    return lw


def _rows_per_step(batch, want):
    while batch % want:
        want //= 2
    return want


def _run_group(x, shift_state, wkv_state, gla_state, w_cat, weights, g_final):
    B, T, _ = x.shape
    fresh = shift_state is None
    C = CHUNK if fresh else SAMPLE_CHUNK
    Tp = -(-T // C) * C
    BB = _rows_per_step(B, ROWS_PER_STEP)
    h = x.reshape(B * T, D_MODEL)
    new_shift, new_wkv, new_gla = [], [], []
    P_first = None
    depth = len(weights)
    for l, lw in enumerate(weights):
        new_shift.append(_rms_call(h.reshape(B, T, D_MODEL)[:, -1], lw['g_mix']))
        P = _proj_call(h, lw['g_mix'], w_cat, l, lw['mu_cat'], None if fresh else shift_state[l], T)
        P = jnp.pad(P.reshape(B, T, N_PROJ), ((0, 0), (0, Tp - T), (0, 0)))
        h3 = jnp.pad(h.reshape(B, T, D_MODEL), ((0, 0), (0, Tp - T), (0, 0)))
        h3, s_wkv, s_gla = _mixer_call(P, P_first if l > 0 else None, h3, wkv_state, gla_state, l, lw,
                                       C=C, n_valid=min(T, C), BB=BB)
        if l == 0:
            P_first = P
        if fresh:
            new_wkv.append(s_wkv)
            new_gla.append(s_gla)
        else:
            wkv_state, gla_state = s_wkv, s_gla
        h = h3[:, :T].reshape(B * T, D_MODEL)
        h = _ffn_call(h, lw['g_ffn'], lw['w_gu'], lw['w_down'], g_final, final_norm=(l == depth - 1))
    if fresh:
        wkv_state, gla_state = jnp.stack(new_wkv), jnp.stack(new_gla)
    return (h.reshape(B, T, D_MODEL), jnp.stack(new_shift), wkv_state, gla_state)


def kernel(x_prompt, x_sample, state_shift, state_wkv, state_gla, norm_mix, w_in, mu_shift, rwkv_w0, rwkv_w2, rwkv_a0, rwkv_a2, rwkv_k_k, rwkv_k_a, rwkv_r_k, rwkv_lnx_g, rwkv_lnx_b, vres_w1, vres_mu, vres_w2, vres_v0, gla_a2, gla_ab, gla_norm_g, w_o, norm_ffn, ffn_w_gu, ffn_w_down, norm_final):
    p = dict(norm_mix=norm_mix, rwkv_w0=rwkv_w0, rwkv_w2=rwkv_w2, rwkv_a0=rwkv_a0, rwkv_a2=rwkv_a2,
             rwkv_k_k=rwkv_k_k, rwkv_k_a=rwkv_k_a, rwkv_r_k=rwkv_r_k, rwkv_lnx_g=rwkv_lnx_g,
             rwkv_lnx_b=rwkv_lnx_b, vres_w2=vres_w2, vres_v0=vres_v0, gla_a2=gla_a2, gla_ab=gla_ab,
             gla_norm_g=gla_norm_g, w_o=w_o, norm_ffn=norm_ffn, ffn_w_gu=ffn_w_gu, ffn_w_down=ffn_w_down)
    depth = w_in.shape[0]
    weights = [_layer_weights(l, mu_shift, vres_mu, p) for l in range(depth)]
    w_cat = _wcat_call(w_in, jnp.concatenate([jnp.zeros_like(vres_w1[:1]), vres_w1], axis=0))
    g_final = norm_final[None, :]
    y_p, shift_p, wkv_p, gla_p = _run_group(x_prompt, None, None, None, w_cat, weights, g_final)
    y_s, shift_s, wkv_s, gla_s = _run_group(x_sample, state_shift, state_wkv, state_gla, w_cat, weights,
                                            g_final)
    return (y_p, y_s, shift_p, wkv_p, gla_p, shift_s, wkv_s, gla_s)
```

```python
import functools

import jax
import jax.numpy as jnp
from jax import lax
from jax.experimental import pallas as pl
from jax.experimental.pallas import tpu as pltpu

F32 = jnp.float32
BF16 = jnp.bfloat16

LANES = 128
D_MODEL = 1024
RWKV_HEAD_DIM = 64
RWKV_HEADS = D_MODEL // RWKV_HEAD_DIM
HEADS_PER_PAIR = LANES // RWKV_HEAD_DIM
N_PAIRS = RWKV_HEADS // HEADS_PER_PAIR
LORA_DECAY = 64
LORA_ICLR = 64
LORA_VRES = 32
RWKV_GN_EPS = 64e-5
GLA_HEADS = 4
GLA_DK = 128
GLA_DV = 256
GLA_GATE_RANK = 16
GLA_GATE_NORMALIZER = 16.0
GLA_NORM_EPS = 1e-5
FFN_HIDDEN = 2816
NORM_EPS = 1e-6
CHUNK = 64
SAMPLE_CHUNK = 8

COL_R, COL_K, COL_V = 0, 1024, 2048
COL_LO, COL_VLO, COL_GLO = 3072, 3200, 3328
COL_GQ = 3584
COL_GATE_A, COL_GATE_B = 4096, 5120
COL_GV, COL_GK = 6144, 7168
N_SHIFTED = 3328
N_PROJ = 7680
PROJ_TM = 1024
PROJ_TN = 1536
PROJ_SUB = 512
VMEM_LIMIT = 48 * 1024 * 1024
ROWS_PER_STEP = 2
DECAY_SCALE = 0.6065306597126334


def _cparams(sem):
    return pltpu.CompilerParams(dimension_semantics=sem, vmem_limit_bytes=VMEM_LIMIT)


def _mm(a, b):
    return jnp.dot(a, b, preferred_element_type=F32)


def _mm_nt(a, b):
    return lax.dot_general(a, b, (((1,), (1,)), ((), ())), preferred_element_type=F32)


def _mm_tn(a, b):
    return lax.dot_general(a, b, (((0,), (0,)), ((), ())), preferred_element_type=F32)


def _sigmoid(x):
    return 0.5 * jnp.tanh(0.5 * x) + 0.5


def _softplus(x):
    return jnp.maximum(x, 0.0) + jnp.log(1.0 + jnp.exp(-jnp.abs(x)))


def _cumsum_rows(x):
    n = x.shape[0]
    row = lax.broadcasted_iota(jnp.int32, x.shape, 0)
    s = 1
    while s < n:
        x = x + jnp.where(row >= s, pltpu.roll(x, s, 0), 0.0)
        s *= 2
    return x


def _rms_rows(x, g):
    ms = jnp.mean(x * x, axis=-1, keepdims=True)
    return x * lax.rsqrt(ms + NORM_EPS) * g


def _proj_kernel(*refs, sample, tiles_per_seq, tm, n_shift_tiles):
    if sample:
        x_ref, g_ref, w_ref, mu_ref, prev_ref, o_ref, xs_ref, ps_ref = refs
    else:
        x_ref, g_ref, w_ref, mu_ref, o_ref, xs_ref, last_ref = refs
    i = pl.program_id(0)
    j = pl.program_id(1)

    @pl.when(j == 0)
    def _():
        xs_ref[...] = _rms_rows(x_ref[...], g_ref[...]).astype(BF16)
        if sample:
            ps_ref[...] = prev_ref[...].astype(BF16)

    xs = xs_ref[...]
    subs = [slice(n * PROJ_SUB, (n + 1) * PROJ_SUB) for n in range(PROJ_TN // PROJ_SUB)]
    if sample:
        ps = ps_ref[...]
        for cs in subs:
            y = _mm(xs, w_ref[:, cs])
            o_ref[:, cs] = y + mu_ref[:, cs] * (_mm(ps, w_ref[:, cs]) - y)
        return

    @pl.when(j < n_shift_tiles)
    def _():
        @pl.when(i == 0)
        def _():
            last_ref[j] = jnp.zeros((8, PROJ_TN), F32)

        row = lax.broadcasted_iota(jnp.int32, (tm, PROJ_SUB), 0)
        for cs in subs:
            y = _mm(xs, w_ref[:, cs])
            prev_row = jnp.where(i % tiles_per_seq == 0, 0.0, last_ref[j, 7:8, cs])
            shifted = jnp.where(row == 0, prev_row, pltpu.roll(y, 1, 0))
            last_ref[j, :, cs] = y[tm - 8:, :]
            o_ref[:, cs] = y + mu_ref[:, cs] * (shifted - y)

    @pl.when(j >= n_shift_tiles)
    def _():
        for cs in subs:
            o_ref[:, cs] = _mm(xs, w_ref[:, cs])


def _proj_call(x, g, w, layer, mu, prev, seq_len):
    m = x.shape[0]
    sample = prev is not None
    tm = min(PROJ_TM, m if sample else seq_len)
    nj = N_PROJ // PROJ_TN
    n_shift_tiles = -(-N_SHIFTED // PROJ_TN)
    in_specs = [
        pl.BlockSpec((tm, D_MODEL), lambda i, j: (i, 0)),
        pl.BlockSpec((1, D_MODEL), lambda i, j: (0, 0)),
        pl.BlockSpec((None, D_MODEL, PROJ_TN), lambda i, j: (layer, 0, j)),
        pl.BlockSpec((1, PROJ_TN), lambda i, j: (0, j)),
    ]
    args = [x, g, w, mu]
    scratch = [pltpu.VMEM((tm, D_MODEL), BF16)]
    if sample:
        in_specs.append(pl.BlockSpec((tm, D_MODEL), lambda i, j: (i, 0)))
        args.append(prev)
        scratch.append(pltpu.VMEM((tm, D_MODEL), BF16))
    else:
        scratch.append(pltpu.VMEM((n_shift_tiles, 8, PROJ_TN), F32))
    return pl.pallas_call(
        functools.partial(_proj_kernel, sample=sample, tiles_per_seq=max(seq_len // tm, 1), tm=tm,
                          n_shift_tiles=n_shift_tiles),
        grid=(m // tm, nj),
        in_specs=in_specs,
        out_specs=pl.BlockSpec((tm, PROJ_TN), lambda i, j: (i, j)),
        out_shape=jax.ShapeDtypeStruct((m, N_PROJ), F32),
        scratch_shapes=scratch,
        compiler_params=_cparams(("arbitrary", "arbitrary")),
        name="proj_sample" if sample else "proj_prompt",
    )(*args)


class _Shared:
    pass


def _rwkv_row_stages(bi, R, res, *, C, n_valid, has_vres):
    H = RWKV_HEAD_DIM
    head0, trow = R.head0, R.trow
    sls = [slice(p * LANES, (p + 1) * LANES) for p in range(N_PAIRS)]
    scr = [bi * N_PAIRS + p for p in range(N_PAIRS)]
    n_sq = C.bit_length() - 2
    fused = 2 * C == LANES
    st = {}
    bf = lambda x: x.astype(BF16)
    cat0 = lambda x, y: jnp.concatenate([x, y], axis=0)
    cat1 = lambda x, y: jnp.concatenate([x, y], axis=1)

    def each(f, *lists):
        return [f(*xs) for xs in zip(*lists)]

    def stack(x):
        return jnp.concatenate([jnp.where(head0, x, 0.0), jnp.where(head0, 0.0, x)], axis=0)

    def head_sum(x):
        s0 = jnp.sum(jnp.where(head0, x, 0.0), axis=-1, keepdims=True)
        s1 = jnp.sum(jnp.where(head0, 0.0, x), axis=-1, keepdims=True)
        return jnp.where(head0, s0, s1)

    def lora():
        lo = R.lo_ref[bi]
        w2a = R.w2a_ref[...]
        st['wl'] = _mm(jnp.where(head0, jnp.tanh(lo), 0.0), w2a)
        st['al'] = _mm(jnp.where(head0, 0.0, lo), w2a)
        if has_vres:
            st['vg'] = _mm(R.vlo_ref[bi], R.vw2_ref[...])

    def operands():
        vec = lambda row: [R.vec_ref[row:row + 1, sl] for sl in sls]
        w0, a0, k_k, k_a, r_k, lnx_g, lnx_b, v0 = (vec(i) for i in range(8))
        r = [R.r_ref[bi, :, sl] for sl in sls]
        k = [R.k_ref[bi, :, sl] for sl in sls]
        v = [R.v_ref[bi, :, sl] for sl in sls]
        if has_vres:
            v = [vp + (R.vf_ref[bi, :, sl] - vp) * _sigmoid(z + st['vg'][:, sl])
                 for vp, sl, z in zip(v, sls, v0)]

        def decay_log(w0p, sl):
            ld = -DECAY_SCALE * _sigmoid(w0p + st['wl'][:, sl])
            if n_valid < C:
                ld = jnp.where(trow < n_valid, ld, 0.0)
            return ld

        ld = each(decay_log, w0, sls)
        a = each(lambda z, sl: _sigmoid(z + st['al'][:, sl]), a0, sls)

        def unit_kk(kp, kkp):
            kk = kp * kkp
            return kk * lax.rsqrt(jnp.maximum(head_sum(kk * kk), 1e-24))

        kk = each(unit_kk, k, k_k)
        k = each(lambda kp, ap, kap: kp * (1.0 + (ap - 1.0) * kap), k, a, k_a)
        b = each(_cumsum_rows, ld)
        e_b = each(jnp.exp, b)
        e_nb = each(lambda x: jnp.exp(-x), b)
        st['al_s'] = each(lambda kkp, bp, ldp: bf(stack(-kkp * jnp.exp(bp - ldp))), kk, b, ld)
        st['be'] = each(lambda ap, kkp, e: bf(stack(ap * kkp * e)), a, kk, e_nb)
        st['kt'] = each(lambda kp, e: bf(stack(kp * e)), k, e_nb)
        st['rt'] = each(lambda rp, e: bf(stack(rp * e)), r, e_b)
        st['vs'] = each(lambda x: bf(stack(x)), v)
        st['decay'] = [e[C - 1:C, :] for e in e_b]
        st['S32'] = [R.s_scr[i] for i in scr]
        st['epi'] = (r, k, v, r_k, lnx_g, lnx_b)

    def finish(o2, s_new):
        for i, s in zip(scr, s_new):
            R.s_scr[i] = s
        r, k, v, r_k, lnx_g, lnx_b = st['epi']
        outs = []
        for p in range(N_PAIRS):
            o = o2[p][:C] + o2[p][C:]
            mean = head_sum(o) * (1.0 / H)
            d = o - mean
            var = head_sum(d * d) * (1.0 / H)
            o = d * lax.rsqrt(var + RWKV_GN_EPS) * lnx_g[p] + lnx_b[p]
            outs.append(o + head_sum(r[p] * k[p] * r_k[p]) * v[p])
        res[bi] = outs

    if fused:
        def amat():
            operands()
            st['ar'] = each(cat0, st['al_s'], st['rt'])
            st['bk'] = each(cat0, st['be'], st['kt'])
            st['amat'] = each(lambda x, y: jnp.where(R.mask4, _mm_nt(x, y), 0.0), st['ar'], st['bk'])

        def xq():
            sv = each(lambda s, y: cat0(bf(s.T), y), st['S32'], st['vs'])
            st['x'] = each(lambda l, m, rhs: _mm(cat1(l, bf(m[:, LANES:])), rhs), st['ar'], st['amat'], sv)
            st['tinv'] = each(lambda m: R.eye + m[:LANES, :LANES], st['amat'])
            q = each(lambda m: bf(m[:LANES, :LANES]), st['amat'])
            st['q'] = each(lambda z: bf(_mm(z, z)), q)

        def double():
            y = each(lambda z, t: _mm(z, cat1(bf(t), z)), st['q'], st['tinv'])
            st['tinv'] = each(lambda t, z: t + z[:, :LANES], st['tinv'], y)
            st['q'] = each(lambda z: bf(z[:, LANES:]), y)

        def double_last():
            st['tinv'] = each(lambda t, z: t + _mm(z, bf(t)), st['tinv'], st['q'])

        def solve():
            st['cm'] = each(lambda t, z: bf(_mm(bf(t), bf(z[:LANES]))), st['tinv'], st['x'])

        def out_state():
            o2 = each(lambda z, m, cc: z[LANES:] + _mm(bf(m[LANES:, :LANES]), cc),
                      st['x'], st['amat'], st['cm'])
            s_new = each(lambda s, cc, y, rhs, e: (s + _mm_tn(cat0(cc, y), rhs)) * e,
                         st['S32'], st['cm'], st['vs'], st['bk'], st['decay'])
            finish(o2, s_new)

        return [lora, amat, xq] + [double] * (n_sq - 1) + [double_last, solve, out_state]

    def small_a():
        operands()
        S = each(bf, st['S32'])
        st['S'] = S
        a_ak = each(lambda x, y: bf(jnp.where(R.strict, _mm_nt(x, y), 0.0)), st['al_s'], st['kt'])
        st['a_rb'] = each(lambda x, y: bf(jnp.where(R.incl, _mm_nt(x, y), 0.0)), st['rt'], st['be'])
        st['a_rk'] = each(lambda x, y: bf(jnp.where(R.incl, _mm_nt(x, y), 0.0)), st['rt'], st['kt'])
        st['cm'] = each(lambda x, s, m, y: _mm_nt(x, s) + _mm(m, y), st['al_s'], S, a_ak, st['vs'])

    def small_solve():
        pw = each(lambda x, y: jnp.where(R.strict, _mm_nt(x, y), 0.0), st['al_s'], st['be'])
        tinv = each(lambda x: R.eye + x, pw)
        pwb = each(bf, pw)
        for _ in range(n_sq):
            pwb = each(lambda x: bf(_mm(x, x)), pwb)
            tinv = each(lambda t, x: t + _mm(bf(t), x), tinv, pwb)
        st['cm'] = each(lambda t, x: _mm(bf(t), bf(x)), tinv, st['cm'])

    def small_out():
        cm = each(bf, st['cm'])
        o2 = each(lambda x, s, m1, c1, m2, y: _mm_nt(x, s) + _mm(m1, c1) + _mm(m2, y),
                  st['rt'], st['S'], st['a_rb'], cm, st['a_rk'], st['vs'])
        s_new = each(lambda s, c1, x, y, z, e: (s + _mm_tn(c1, x) + _mm_tn(y, z)) * e,
                     st['S32'], cm, st['be'], st['vs'], st['kt'], st['decay'])
        finish(o2, s_new)

    return [lora, small_a] + ([small_solve] if n_valid > 1 else []) + [small_out]


def _gla_row_stages(bi, R, res, *, C, n_valid):
    heads = range(GLA_HEADS)
    ks = [slice(h * GLA_DK, (h + 1) * GLA_DK) for h in heads]
    vsl = [slice(h * GLA_DV, (h + 1) * GLA_DV) for h in heads]
    scr = [bi * GLA_HEADS + h for h in heads]
    st = {}

    def gate():
        st['gate'] = _mm(R.glo_ref[bi], R.ga2_ref[...])

    def scores():
        def cum_log_decay(s):
            la = -_softplus(-(st['gate'][:, s] + R.gab_ref[:, s])) * (1.0 / GLA_GATE_NORMALIZER)
            if n_valid < C:
                la = jnp.where(R.trow < n_valid, la, 0.0)
            return _cumsum_rows(la)

        b = [cum_log_decay(s) for s in ks]
        k = [R.gk_ref[bi, :, s] for s in ks]
        st['v'] = [R.gv_ref[bi, :, s] for s in vsl]
        qd = [R.gq_ref[bi, :, s] * (GLA_DK ** -0.5) * jnp.exp(bh) for s, bh in zip(ks, b)]
        kd = [kh * jnp.exp(-bh) for kh, bh in zip(k, b)]
        kl = [kh * jnp.exp(bh[C - 1:C, :] - bh) for kh, bh in zip(k, b)]
        St = [R.g_scr[i] for i in scr]
        st['att'] = [jnp.where(R.causal, _mm_nt(x, y), 0.0) for x, y in zip(qd, kd)]
        st['o_s'] = [_mm_nt(x, s) for x, s in zip(qd, St)]
        upd = [_mm_tn(x, y) for x, y in zip(st['v'], kl)]
        for i, s_old, bh, u in zip(scr, St, b, upd):
            R.g_scr[i] = s_old * jnp.exp(bh[C - 1:C, :]) + u

    def outputs():
        o = [_mm(x, y) + z for x, y, z in zip(st['att'], st['v'], st['o_s'])]
        res[bi] = [oh * lax.rsqrt(jnp.mean(oh * oh, axis=-1, keepdims=True) + GLA_NORM_EPS) * R.gg_ref[...]
                   for oh in o]

    return [gate, scores, outputs]


def _mixer_kernel(*refs, C, BB, n_valid, has_s0, has_vres, layer, copy_other_layers):
    R = _Shared()
    it = iter(refs)
    R.r_ref, R.k_ref, R.v_ref, R.lo_ref = next(it), next(it), next(it), next(it)
    if has_vres:
        R.vlo_ref, R.vf_ref = next(it), next(it)
    R.gq_ref, R.gk_ref, R.gv_ref, R.glo_ref = next(it), next(it), next(it), next(it)
    ga_ref, gb_ref, h_ref = next(it), next(it), next(it)
    if has_s0:
        s0w_ref, s0g_ref = next(it), next(it)
    R.w2a_ref = next(it)
    if has_vres:
        R.vw2_ref = next(it)
    R.vec_ref, R.ga2_ref, R.gab_ref, R.gg_ref, wo_ref = next(it), next(it), next(it), next(it), next(it)
    o_ref, sow_ref, sog_ref, R.s_scr, R.g_scr = next(it), next(it), next(it), next(it), next(it)

    c = pl.program_id(1)
    H = RWKV_HEAD_DIM
    rows = range(BB)
    if has_s0 and copy_other_layers:
        s0w_all, s0g_all, sow_all, sog_all = s0w_ref, s0g_ref, sow_ref, sog_ref
        s0w_ref, s0g_ref = s0w_all.at[layer], s0g_all.at[layer]
        sow_ref, sog_ref = sow_all.at[layer], sog_all.at[layer]

    @pl.when(c == 0)
    def _():
        R.s_scr[...] = jnp.zeros_like(R.s_scr)
        if has_s0:
            for bi in rows:
                for p in range(N_PAIRS):
                    R.s_scr[bi * N_PAIRS + p, 0:H, 0:H] = s0w_ref[bi, 2 * p]
                    R.s_scr[bi * N_PAIRS + p, H:2 * H, H:2 * H] = s0w_ref[bi, 2 * p + 1]
                for h in range(GLA_HEADS):
                    R.g_scr[bi * GLA_HEADS + h] = s0g_ref[bi, h].T
        else:
            R.g_scr[...] = jnp.zeros_like(R.g_scr)

    R.head0 = lax.broadcasted_iota(jnp.int32, (C, LANES), 1) < H
    R.trow = lax.broadcasted_iota(jnp.int32, (C, LANES), 0)
    ri = lax.broadcasted_iota(jnp.int32, (C, C), 0)
    ci = lax.broadcasted_iota(jnp.int32, (C, C), 1)
    R.causal = ri >= ci
    if 2 * C == LANES:
        ri = lax.broadcasted_iota(jnp.int32, (2 * LANES, 2 * LANES), 0)
        ci = lax.broadcasted_iota(jnp.int32, (2 * LANES, 2 * LANES), 1)
        rr, cc = ri % LANES, ci % LANES
        R.mask4 = ((rr // C) == (cc // C)) & ((rr % C) >= (cc % C) + (ri < LANES).astype(jnp.int32))
        r1 = lax.broadcasted_iota(jnp.int32, (LANES, LANES), 0)
        c1 = lax.broadcasted_iota(jnp.int32, (LANES, LANES), 1)
        R.eye = (r1 == c1).astype(F32)
    else:
        ri = lax.broadcasted_iota(jnp.int32, (2 * C, 2 * C), 0)
        ci = lax.broadcasted_iota(jnp.int32, (2 * C, 2 * C), 1)
        same_head = (ri // C) == (ci // C)
        R.strict = same_head & ((ri % C) > (ci % C))
        R.incl = same_head & ((ri % C) >= (ci % C))
        R.eye = (ri == ci).astype(F32)

    ro, go = {}, {}
    plans = []
    for bi in rows:
        plans.append(_rwkv_row_stages(bi, R, ro, C=C, n_valid=n_valid, has_vres=has_vres))
        plans.append(_gla_row_stages(bi, R, go, C=C, n_valid=n_valid))
    for d in range(max(len(stages) for stages in plans)):
        for stages in plans:
            if d < len(stages):
                stages[d]()

    merged = [_sigmoid(ga_ref[bi]) * jnp.concatenate(ro[bi], axis=1)
              + _sigmoid(gb_ref[bi]) * jnp.concatenate(go[bi], axis=1) for bi in rows]
    mix = _mm(jnp.concatenate(merged, axis=0).astype(BF16), wo_ref[...])
    for bi in rows:
        o_ref[bi] = h_ref[bi] + mix[bi * C:(bi + 1) * C]

    @pl.when(c == pl.num_programs(1) - 1)
    def _():
        for bi in rows:
            for p in range(N_PAIRS):
                sow_ref[bi, 2 * p] = R.s_scr[bi * N_PAIRS + p, 0:H, 0:H]
                sow_ref[bi, 2 * p + 1] = R.s_scr[bi * N_PAIRS + p, H:2 * H, H:2 * H]
            for h in range(GLA_HEADS):
                sog_ref[bi, h] = R.g_scr[bi * GLA_HEADS + h].T
        if has_s0 and copy_other_layers:
            for other in range(s0w_all.shape[0]):
                if other != layer:
                    sow_all[other] = s0w_all[other]
                    sog_all[other] = s0g_all[other]


def _mixer_call(P, Pfirst, h, s_wkv, s_gla, layer, lw, *, C, n_valid, BB):
    B, T, _ = P.shape
    has_vres = Pfirst is not None
    has_s0 = s_wkv is not None
    kw = GLA_HEADS * GLA_DK

    def cols(off, width):
        idx = off // width
        return pl.BlockSpec((BB, C, width), lambda b, c: (b, c, idx))

    whole = lambda shape: pl.BlockSpec(shape, lambda b, c: (0,) * len(shape))
    wkv_block = (BB, RWKV_HEADS, RWKV_HEAD_DIM, RWKV_HEAD_DIM)
    gla_block = (BB, GLA_HEADS, GLA_DK, GLA_DV)
    copy_other_layers = has_s0 and layer == 0
    if copy_other_layers:
        depth = s_wkv.shape[0]
        state_spec = lambda blk: pl.BlockSpec((depth,) + blk, lambda b, c: (0, b, 0, 0, 0))
        wkv_shape, gla_shape = s_wkv.shape, s_gla.shape
    elif has_s0:
        state_spec = lambda blk: pl.BlockSpec((None,) + blk, lambda b, c: (layer, b, 0, 0, 0))
        wkv_shape, gla_shape = s_wkv.shape, s_gla.shape
    else:
        state_spec = lambda blk: pl.BlockSpec(blk, lambda b, c: (b, 0, 0, 0))
        wkv_shape, gla_shape = (B,) + wkv_block[1:], (B,) + gla_block[1:]

    in_specs = [cols(COL_R, D_MODEL), cols(COL_K, D_MODEL), cols(COL_V, D_MODEL), cols(COL_LO, LANES)]
    args = [P, P, P, P]
    if has_vres:
        in_specs += [cols(COL_VLO, LANES), cols(COL_V, D_MODEL)]
        args += [P, Pfirst]
    in_specs += [cols(COL_GQ, kw), cols(COL_GK, kw), cols(COL_GV, D_MODEL), cols(COL_GLO, LANES),
                 cols(COL_GATE_A, D_MODEL), cols(COL_GATE_B, D_MODEL), cols(0, D_MODEL)]
    args += [P, P, P, P, P, P, h]
    aliases = {}
    if has_s0:
        if not copy_other_layers:
            aliases = {len(args): 1, len(args) + 1: 2}
        in_specs += [state_spec(wkv_block), state_spec(gla_block)]
        args += [s_wkv, s_gla]
    in_specs.append(whole((LANES, D_MODEL)))
    args.append(lw['w2a'])
    if has_vres:
        in_specs.append(whole((LANES, D_MODEL)))
        args.append(lw['vw2'])
    in_specs += [whole((8, D_MODEL)), whole((LANES, kw)), whole((1, kw)), whole((1, GLA_DV)),
                 whole((D_MODEL, D_MODEL))]
    args += [lw['vecs'], lw['ga2'], lw['gab'], lw['gla_g'], lw['w_o']]
    return pl.pallas_call(
        functools.partial(_mixer_kernel, C=C, BB=BB, n_valid=n_valid, has_s0=has_s0, has_vres=has_vres,
                          layer=layer, copy_other_layers=copy_other_layers),
        grid=(B // BB, T // C),
        in_specs=in_specs,
        out_specs=[pl.BlockSpec((BB, C, D_MODEL), lambda b, c: (b, c, 0)),
                   state_spec(wkv_block), state_spec(gla_block)],
        out_shape=[jax.ShapeDtypeStruct((B, T, D_MODEL), F32),
                   jax.ShapeDtypeStruct(wkv_shape, F32),
                   jax.ShapeDtypeStruct(gla_shape, F32)],
        input_output_aliases=aliases,
        scratch_shapes=[pltpu.VMEM((BB * N_PAIRS, LANES, LANES), F32),
                        pltpu.VMEM((BB * GLA_HEADS, GLA_DV, GLA_DK), F32)],
        compiler_params=_cparams(("parallel", "arbitrary")),
        name="mixer",
    )(*args)


FFN_SUB = 256


def _ffn_kernel(h_ref, g_ref, wg_ref, wu_ref, wd_ref, gf_ref, o_ref, *, final_norm):
    h = h_ref[...]
    xs = _rms_rows(h, g_ref[...]).astype(BF16)
    out = h
    for n in range(FFN_HIDDEN // FFN_SUB):
        cs = slice(n * FFN_SUB, (n + 1) * FFN_SUB)
        gate = _mm(xs, wg_ref[:, cs])
        up = _mm(xs, wu_ref[:, cs])
        act = (gate * _sigmoid(gate) * up).astype(BF16)
        out = out + _mm(act, wd_ref[cs, :])
    if final_norm:
        out = _rms_rows(out, gf_ref[...])
    o_ref[...] = out


def _ffn_call(h, g, w_gu, w_down, gf, final_norm):
    m = h.shape[0]
    tm = min(512, m)
    row = lambda i: (i, 0)
    return pl.pallas_call(
        functools.partial(_ffn_kernel, final_norm=final_norm),
        grid=(m // tm,),
        in_specs=[pl.BlockSpec((tm, D_MODEL), row),
                  pl.BlockSpec((1, D_MODEL), lambda i: (0, 0)),
                  pl.BlockSpec((D_MODEL, FFN_HIDDEN), lambda i: (0, 0)),
                  pl.BlockSpec((D_MODEL, FFN_HIDDEN), lambda i: (0, 1)),
                  pl.BlockSpec((FFN_HIDDEN, D_MODEL), lambda i: (0, 0)),
                  pl.BlockSpec((1, D_MODEL), lambda i: (0, 0))],
        out_specs=pl.BlockSpec((tm, D_MODEL), row),
        out_shape=jax.ShapeDtypeStruct((m, D_MODEL), F32),
        compiler_params=_cparams(("parallel",)),
        name="ffn",
    )(h, g, w_gu, w_gu, w_down, gf)


def _rms_kernel(x_ref, g_ref, o_ref):
    o_ref[...] = _rms_rows(x_ref[...], g_ref[...])


def _rms_call(x, g):
    return pl.pallas_call(
        _rms_kernel,
        out_shape=jax.ShapeDtypeStruct(x.shape, F32),
        name="rmsnorm_rows",
    )(x, g)


SRC_GQ, SRC_GK, SRC_GV, SRC_GLO, SRC_GATE_A, SRC_GATE_B = 3200, 3712, 4224, 5248, 5264, 6288
WCAT_ROWS = 256


def _wcat_kernel(w_ref, v1_ref, o_ref):
    def put(dst, src, width):
        o_ref[:, dst:dst + width] = w_ref[:, src:src + width].astype(BF16)

    def zero(lo, hi):
        o_ref[:, lo:hi] = jnp.zeros((o_ref.shape[0], hi - lo), BF16)

    put(COL_R, 0, COL_LO + 2 * LORA_DECAY)
    o_ref[:, COL_VLO:COL_VLO + LORA_VRES] = v1_ref[...].astype(BF16)
    zero(COL_VLO + LORA_VRES, COL_GLO)
    put(COL_GLO, SRC_GLO, GLA_GATE_RANK)
    zero(COL_GLO + GLA_GATE_RANK, COL_GQ)
    put(COL_GQ, SRC_GQ, GLA_HEADS * GLA_DK)
    put(COL_GATE_A, SRC_GATE_A, D_MODEL)
    put(COL_GATE_B, SRC_GATE_B, D_MODEL)
    put(COL_GV, SRC_GV, GLA_HEADS * GLA_DV)
    put(COL_GK, SRC_GK, GLA_HEADS * GLA_DK)


def _wcat_call(w_in, vres_w1_padded):
    depth, _, n_in = w_in.shape
    return pl.pallas_call(
        _wcat_kernel,
        grid=(depth, D_MODEL // WCAT_ROWS),
        in_specs=[pl.BlockSpec((None, WCAT_ROWS, n_in), lambda l, i: (l, i, 0)),
                  pl.BlockSpec((None, WCAT_ROWS, LORA_VRES), lambda l, i: (l, i, 0))],
        out_specs=pl.BlockSpec((None, WCAT_ROWS, N_PROJ), lambda l, i: (l, i, 0)),
        out_shape=jax.ShapeDtypeStruct((depth, D_MODEL, N_PROJ), BF16),
        compiler_params=_cparams(("parallel", "parallel")),
        name="w_in_reorder",
    )(w_in, vres_w1_padded)


def _pad_rows(x, n):
    return jnp.pad(x, ((0, n - x.shape[0]), (0, 0)))


def _layer_weights(l, mu_shift, vres_mu, p):
    mu = mu_shift[l]
    zeros_mu = lambda n: jnp.zeros((n,), F32)
    vlo_mu = jnp.pad(vres_mu[l - 1], (0, LANES - LORA_VRES)) if l > 0 else zeros_mu(LANES)
    mu_cat = jnp.concatenate([mu[:COL_VLO], vlo_mu, zeros_mu(N_PROJ - N_SHIFTED)])[None, :]
    lw = dict(mu_cat=mu_cat)
    lw['w2a'] = jnp.concatenate([p['rwkv_w2'][l], p['rwkv_a2'][l]], axis=0)
    lw['vecs'] = jnp.stack([p['rwkv_w0'][l], p['rwkv_a0'][l], p['rwkv_k_k'][l], p['rwkv_k_a'][l],
                            p['rwkv_r_k'][l].reshape(-1), p['rwkv_lnx_g'][l], p['rwkv_lnx_b'][l],
                            p['vres_v0'][l - 1] if l > 0 else zeros_mu(D_MODEL)], axis=0)
    lw['vw2'] = _pad_rows(p['vres_w2'][l - 1], LANES) if l > 0 else None
    lw['ga2'] = _pad_rows(p['gla_a2'][l], LANES)
    lw['gab'] = p['gla_ab'][l][None, :]
    lw['gla_g'] = p['gla_norm_g'][l][None, :]
    lw['w_o'] = p['w_o'][l].astype(BF16)
    lw['w_gu'] = p['ffn_w_gu'][l].astype(BF16)
    lw['w_down'] = p['ffn_w_down'][l].astype(BF16)
    lw['g_mix'] = p['norm_mix'][l][None, :]
    lw['g_ffn'] = p['norm_ffn'][l][None, :]
    return lw


def _rows_per_step(batch, want):
    while batch % want:
        want //= 2
    return want


def _run_group(x, shift_state, wkv_state, gla_state, w_cat, weights, g_final):
    B, T, _ = x.shape
    fresh = shift_state is None
    C = CHUNK if fresh else SAMPLE_CHUNK
    Tp = -(-T // C) * C
    BB = _rows_per_step(B, ROWS_PER_STEP)
    h = x.reshape(B * T, D_MODEL)
    new_shift, new_wkv, new_gla = [], [], []
    P_first = None
    depth = len(weights)
    for l, lw in enumerate(weights):
        new_shift.append(_rms_call(h.reshape(B, T, D_MODEL)[:, -1], lw['g_mix']))
        P = _proj_call(h, lw['g_mix'], w_cat, l, lw['mu_cat'], None if fresh else shift_state[l], T)
        P = jnp.pad(P.reshape(B, T, N_PROJ), ((0, 0), (0, Tp - T), (0, 0)))
        h3 = jnp.pad(h.reshape(B, T, D_MODEL), ((0, 0), (0, Tp - T), (0, 0)))
        h3, s_wkv, s_gla = _mixer_call(P, P_first if l > 0 else None, h3, wkv_state, gla_state, l, lw,
                                       C=C, n_valid=min(T, C), BB=BB)
        if l == 0:
            P_first = P
        if fresh:
            new_wkv.append(s_wkv)
            new_gla.append(s_gla)
        else:
            wkv_state, gla_state = s_wkv, s_gla
        h = h3[:, :T].reshape(B * T, D_MODEL)
        h = _ffn_call(h, lw['g_ffn'], lw['w_gu'], lw['w_down'], g_final, final_norm=(l == depth - 1))
    if fresh:
        wkv_state, gla_state = jnp.stack(new_wkv), jnp.stack(new_gla)
    return (h.reshape(B, T, D_MODEL), jnp.stack(new_shift), wkv_state, gla_state)


def kernel(x_prompt, x_sample, state_shift, state_wkv, state_gla, norm_mix, w_in, mu_shift, rwkv_w0, rwkv_w2, rwkv_a0, rwkv_a2, rwkv_k_k, rwkv_k_a, rwkv_r_k, rwkv_lnx_g, rwkv_lnx_b, vres_w1, vres_mu, vres_w2, vres_v0, gla_a2, gla_ab, gla_norm_g, w_o, norm_ffn, ffn_w_gu, ffn_w_down, norm_final):
    p = dict(norm_mix=norm_mix, rwkv_w0=rwkv_w0, rwkv_w2=rwkv_w2, rwkv_a0=rwkv_a0, rwkv_a2=rwkv_a2,
             rwkv_k_k=rwkv_k_k, rwkv_k_a=rwkv_k_a, rwkv_r_k=rwkv_r_k, rwkv_lnx_g=rwkv_lnx_g,
             rwkv_lnx_b=rwkv_lnx_b, vres_w2=vres_w2, vres_v0=vres_v0, gla_a2=gla_a2, gla_ab=gla_ab,
             gla_norm_g=gla_norm_g, w_o=w_o, norm_ffn=norm_ffn, ffn_w_gu=ffn_w_gu, ffn_w_down=ffn_w_down)
    depth = w_in.shape[0]
    weights = [_layer_weights(l, mu_shift, vres_mu, p) for l in range(depth)]
    w_cat = _wcat_call(w_in, jnp.concatenate([jnp.zeros_like(vres_w1[:1]), vres_w1], axis=0))
    g_final = norm_final[None, :]
    y_p, shift_p, wkv_p, gla_p = _run_group(x_prompt, None, None, None, w_cat, weights, g_final)
    y_s, shift_s, wkv_s, gla_s = _run_group(x_sample, state_shift, state_wkv, state_gla, w_cat, weights,
                                            g_final)
    return (y_p, y_s, shift_p, wkv_p, gla_p, shift_s, wkv_s, gla_s)
```

```python
import functools

import jax
import jax.numpy as jnp
from jax import lax
from jax.experimental import pallas as pl
from jax.experimental.pallas import tpu as pltpu

F32 = jnp.float32
BF16 = jnp.bfloat16

LANES = 128
D_MODEL = 1024
RWKV_HEAD_DIM = 64
RWKV_HEADS = D_MODEL // RWKV_HEAD_DIM
HEADS_PER_PAIR = LANES // RWKV_HEAD_DIM
N_PAIRS = RWKV_HEADS // HEADS_PER_PAIR
LORA_DECAY = 64
LORA_ICLR = 64
LORA_VRES = 32
RWKV_GN_EPS = 64e-5
GLA_HEADS = 4
GLA_DK = 128
GLA_DV = 256
GLA_GATE_RANK = 16
GLA_GATE_NORMALIZER = 16.0
GLA_NORM_EPS = 1e-5
FFN_HIDDEN = 2816
NORM_EPS = 1e-6
CHUNK = 64
SAMPLE_CHUNK = 8

COL_R, COL_K, COL_V = 0, 1024, 2048
COL_LO, COL_VLO, COL_GLO = 3072, 3200, 3328
COL_GQ = 3584
COL_GATE_A, COL_GATE_B = 4096, 5120
COL_GV, COL_GK = 6144, 7168
N_SHIFTED = 3328
N_PROJ = 7680
PROJ_TM = 1024
PROJ_TN = 1536
PROJ_SUB = 512
PROMPT_PROJ_DTYPE = BF16
VMEM_LIMIT = 48 * 1024 * 1024
ROWS_PER_STEP = 2
DECAY_SCALE = 0.6065306597126334


def _cparams(sem):
    return pltpu.CompilerParams(dimension_semantics=sem, vmem_limit_bytes=VMEM_LIMIT)


def _mm(a, b):
    return jnp.dot(a, b, preferred_element_type=F32)


def _mm_nt(a, b):
    return lax.dot_general(a, b, (((1,), (1,)), ((), ())), preferred_element_type=F32)


def _mm_tn(a, b):
    return lax.dot_general(a, b, (((0,), (0,)), ((), ())), preferred_element_type=F32)


def _sigmoid(x):
    return 0.5 * jnp.tanh(0.5 * x) + 0.5


def _softplus(x):
    return jnp.maximum(x, 0.0) + jnp.log(1.0 + jnp.exp(-jnp.abs(x)))


def _cumsum_rows(x):
    n = x.shape[0]
    row = lax.broadcasted_iota(jnp.int32, x.shape, 0)
    s = 1
    while s < n:
        x = x + jnp.where(row >= s, pltpu.roll(x, s, 0), 0.0)
        s *= 2
    return x


def _rms_rows(x, g):
    ms = jnp.mean(x * x, axis=-1, keepdims=True)
    return x * lax.rsqrt(ms + NORM_EPS) * g


def _proj_kernel(*refs, sample, tiles_per_seq, tm, n_shift_tiles):
    if sample:
        x_ref, g_ref, w_ref, mu_ref, prev_ref, o_ref, xs_ref, ps_ref = refs
    else:
        x_ref, g_ref, w_ref, mu_ref, o_ref, xs_ref, last_ref = refs
    i = pl.program_id(0)
    j = pl.program_id(1)

    @pl.when(j == 0)
    def _():
        xs_ref[...] = _rms_rows(x_ref[...], g_ref[...]).astype(BF16)
        if sample:
            ps_ref[...] = prev_ref[...].astype(BF16)

    xs = xs_ref[...]
    subs = [slice(n * PROJ_SUB, (n + 1) * PROJ_SUB) for n in range(PROJ_TN // PROJ_SUB)]
    if sample:
        ps = ps_ref[...]
        for cs in subs:
            y = _mm(xs, w_ref[:, cs])
            o_ref[:, cs] = (y + mu_ref[:, cs] * (_mm(ps, w_ref[:, cs]) - y)).astype(o_ref.dtype)
        return

    @pl.when(j < n_shift_tiles)
    def _():
        @pl.when(i == 0)
        def _():
            last_ref[j] = jnp.zeros((8, PROJ_TN), F32)

        row = lax.broadcasted_iota(jnp.int32, (tm, PROJ_SUB), 0)
        for cs in subs:
            y = _mm(xs, w_ref[:, cs])
            prev_row = jnp.where(i % tiles_per_seq == 0, 0.0, last_ref[j, 7:8, cs])
            shifted = jnp.where(row == 0, prev_row, pltpu.roll(y, 1, 0))
            last_ref[j, :, cs] = y[tm - 8:, :]
            o_ref[:, cs] = (y + mu_ref[:, cs] * (shifted - y)).astype(o_ref.dtype)

    @pl.when(j >= n_shift_tiles)
    def _():
        for cs in subs:
            o_ref[:, cs] = _mm(xs, w_ref[:, cs]).astype(o_ref.dtype)


def _proj_call(x, g, w, layer, mu, prev, seq_len):
    m = x.shape[0]
    sample = prev is not None
    tm = min(PROJ_TM, m if sample else seq_len)
    nj = N_PROJ // PROJ_TN
    n_shift_tiles = -(-N_SHIFTED // PROJ_TN)
    in_specs = [
        pl.BlockSpec((tm, D_MODEL), lambda i, j: (i, 0)),
        pl.BlockSpec((1, D_MODEL), lambda i, j: (0, 0)),
        pl.BlockSpec((None, D_MODEL, PROJ_TN), lambda i, j: (layer, 0, j)),
        pl.BlockSpec((1, PROJ_TN), lambda i, j: (0, j)),
    ]
    args = [x, g, w, mu]
    scratch = [pltpu.VMEM((tm, D_MODEL), BF16)]
    if sample:
        in_specs.append(pl.BlockSpec((tm, D_MODEL), lambda i, j: (i, 0)))
        args.append(prev)
        scratch.append(pltpu.VMEM((tm, D_MODEL), BF16))
    else:
        scratch.append(pltpu.VMEM((n_shift_tiles, 8, PROJ_TN), F32))
    return pl.pallas_call(
        functools.partial(_proj_kernel, sample=sample, tiles_per_seq=max(seq_len // tm, 1), tm=tm,
                          n_shift_tiles=n_shift_tiles),
        grid=(m // tm, nj),
        in_specs=in_specs,
        out_specs=pl.BlockSpec((tm, PROJ_TN), lambda i, j: (i, j)),
        out_shape=jax.ShapeDtypeStruct((m, N_PROJ), F32 if sample else PROMPT_PROJ_DTYPE),
        scratch_shapes=scratch,
        compiler_params=_cparams(("arbitrary", "arbitrary")),
        name="proj_sample" if sample else "proj_prompt",
    )(*args)


class _Shared:
    pass


def _rwkv_row_stages(bi, R, res, *, C, n_valid, has_vres):
    H = RWKV_HEAD_DIM
    head0, trow = R.head0, R.trow
    sls = [slice(p * LANES, (p + 1) * LANES) for p in range(N_PAIRS)]
    scr = [bi * N_PAIRS + p for p in range(N_PAIRS)]
    n_sq = C.bit_length() - 2
    fused = 2 * C == LANES
    st = {}
    bf = lambda x: x.astype(BF16)
    cat0 = lambda x, y: jnp.concatenate([x, y], axis=0)
    cat1 = lambda x, y: jnp.concatenate([x, y], axis=1)

    def each(f, *lists):
        return [f(*xs) for xs in zip(*lists)]

    def stack(x):
        return jnp.concatenate([jnp.where(head0, x, 0.0), jnp.where(head0, 0.0, x)], axis=0)

    def head_sum(x):
        s0 = jnp.sum(jnp.where(head0, x, 0.0), axis=-1, keepdims=True)
        s1 = jnp.sum(jnp.where(head0, 0.0, x), axis=-1, keepdims=True)
        return jnp.where(head0, s0, s1)

    def lora():
        lo = R.lo_ref[bi].astype(F32)
        w2a = R.w2a_ref[...]
        st['wl'] = _mm(jnp.where(head0, jnp.tanh(lo), 0.0), w2a)
        st['al'] = _mm(jnp.where(head0, 0.0, lo), w2a)
        if has_vres:
            st['vg'] = _mm(R.vlo_ref[bi].astype(F32), R.vw2_ref[...])

    def operands():
        vec = lambda row: [R.vec_ref[row:row + 1, sl] for sl in sls]
        w0, a0, k_k, k_a, r_k, lnx_g, lnx_b, v0 = (vec(i) for i in range(8))
        r = [R.r_ref[bi, :, sl].astype(F32) for sl in sls]
        k = [R.k_ref[bi, :, sl].astype(F32) for sl in sls]
        v = [R.v_ref[bi, :, sl].astype(F32) for sl in sls]
        if has_vres:
            v = [vp + (R.vf_ref[bi, :, sl].astype(F32) - vp) * _sigmoid(z + st['vg'][:, sl])
                 for vp, sl, z in zip(v, sls, v0)]

        def decay_log(w0p, sl):
            ld = -DECAY_SCALE * _sigmoid(w0p + st['wl'][:, sl])
            if n_valid < C:
                ld = jnp.where(trow < n_valid, ld, 0.0)
            return ld

        ld = each(decay_log, w0, sls)
        a = each(lambda z, sl: _sigmoid(z + st['al'][:, sl]), a0, sls)

        def unit_kk(kp, kkp):
            kk = kp * kkp
            return kk * lax.rsqrt(jnp.maximum(head_sum(kk * kk), 1e-24))

        kk = each(unit_kk, k, k_k)
        k = each(lambda kp, ap, kap: kp * (1.0 + (ap - 1.0) * kap), k, a, k_a)
        b = each(_cumsum_rows, ld)
        e_b = each(jnp.exp, b)
        e_nb = each(lambda x: jnp.exp(-x), b)
        st['al_s'] = each(lambda kkp, bp, ldp: bf(stack(-kkp * jnp.exp(bp - ldp))), kk, b, ld)
        st['be'] = each(lambda ap, kkp, e: bf(stack(ap * kkp * e)), a, kk, e_nb)
        st['kt'] = each(lambda kp, e: bf(stack(kp * e)), k, e_nb)
        st['rt'] = each(lambda rp, e: bf(stack(rp * e)), r, e_b)
        st['vs'] = each(lambda x: bf(stack(x)), v)
        st['decay'] = [e[C - 1:C, :] for e in e_b]
        st['S32'] = [R.s_scr[i] for i in scr]
        st['epi'] = (r, k, v, r_k, lnx_g, lnx_b)

    def finish(o2, s_new):
        for i, s in zip(scr, s_new):
            R.s_scr[i] = s
        r, k, v, r_k, lnx_g, lnx_b = st['epi']
        outs = []
        for p in range(N_PAIRS):
            o = o2[p][:C] + o2[p][C:]
            mean = head_sum(o) * (1.0 / H)
            d = o - mean
            var = head_sum(d * d) * (1.0 / H)
            o = d * lax.rsqrt(var + RWKV_GN_EPS) * lnx_g[p] + lnx_b[p]
            outs.append(o + head_sum(r[p] * k[p] * r_k[p]) * v[p])
        res[bi] = outs

    if fused:
        def amat():
            operands()
            st['ar'] = each(cat0, st['al_s'], st['rt'])
            st['bk'] = each(cat0, st['be'], st['kt'])
            st['amat'] = each(lambda x, y: jnp.where(R.mask4, _mm_nt(x, y), 0.0), st['ar'], st['bk'])

        def xq():
            sv = each(lambda s, y: cat0(bf(s.T), y), st['S32'], st['vs'])
            st['x'] = each(lambda l, m, rhs: _mm(cat1(l, bf(m[:, LANES:])), rhs), st['ar'], st['amat'], sv)
            st['tinv'] = each(lambda m: R.eye + m[:LANES, :LANES], st['amat'])
            q = each(lambda m: bf(m[:LANES, :LANES]), st['amat'])
            st['q'] = each(lambda z: bf(_mm(z, z)), q)

        def double():
            y = each(lambda z, t: _mm(z, cat1(bf(t), z)), st['q'], st['tinv'])
            st['tinv'] = each(lambda t, z: t + z[:, :LANES], st['tinv'], y)
            st['q'] = each(lambda z: bf(z[:, LANES:]), y)

        def double_last():
            st['tinv'] = each(lambda t, z: t + _mm(z, bf(t)), st['tinv'], st['q'])

        def solve():
            st['cm'] = each(lambda t, z: bf(_mm(bf(t), bf(z[:LANES]))), st['tinv'], st['x'])

        def out_state():
            o2 = each(lambda z, m, cc: z[LANES:] + _mm(bf(m[LANES:, :LANES]), cc),
                      st['x'], st['amat'], st['cm'])
            s_new = each(lambda s, cc, y, rhs, e: (s + _mm_tn(cat0(cc, y), rhs)) * e,
                         st['S32'], st['cm'], st['vs'], st['bk'], st['decay'])
            finish(o2, s_new)

        return [lora, amat, xq] + [double] * (n_sq - 1) + [double_last, solve, out_state]

    def small_a():
        operands()
        S = each(bf, st['S32'])
        st['S'] = S
        a_ak = each(lambda x, y: bf(jnp.where(R.strict, _mm_nt(x, y), 0.0)), st['al_s'], st['kt'])
        st['a_rb'] = each(lambda x, y: bf(jnp.where(R.incl, _mm_nt(x, y), 0.0)), st['rt'], st['be'])
        st['a_rk'] = each(lambda x, y: bf(jnp.where(R.incl, _mm_nt(x, y), 0.0)), st['rt'], st['kt'])
        st['cm'] = each(lambda x, s, m, y: _mm_nt(x, s) + _mm(m, y), st['al_s'], S, a_ak, st['vs'])

    def small_solve():
        pw = each(lambda x, y: jnp.where(R.strict, _mm_nt(x, y), 0.0), st['al_s'], st['be'])
        tinv = each(lambda x: R.eye + x, pw)
        pwb = each(bf, pw)
        for _ in range(n_sq):
            pwb = each(lambda x: bf(_mm(x, x)), pwb)
            tinv = each(lambda t, x: t + _mm(bf(t), x), tinv, pwb)
        st['cm'] = each(lambda t, x: _mm(bf(t), bf(x)), tinv, st['cm'])

    def small_out():
        cm = each(bf, st['cm'])
        o2 = each(lambda x, s, m1, c1, m2, y: _mm_nt(x, s) + _mm(m1, c1) + _mm(m2, y),
                  st['rt'], st['S'], st['a_rb'], cm, st['a_rk'], st['vs'])
        s_new = each(lambda s, c1, x, y, z, e: (s + _mm_tn(c1, x) + _mm_tn(y, z)) * e,
                     st['S32'], cm, st['be'], st['vs'], st['kt'], st['decay'])
        finish(o2, s_new)

    return [lora, small_a] + ([small_solve] if n_valid > 1 else []) + [small_out]


def _gla_row_stages(bi, R, res, *, C, n_valid):
    heads = range(GLA_HEADS)
    ks = [slice(h * GLA_DK, (h + 1) * GLA_DK) for h in heads]
    vsl = [slice(h * GLA_DV, (h + 1) * GLA_DV) for h in heads]
    scr = [bi * GLA_HEADS + h for h in heads]
    st = {}

    def gate():
        st['gate'] = _mm(R.glo_ref[bi].astype(F32), R.ga2_ref[...])

    def scores():
        def cum_log_decay(s):
            la = -_softplus(-(st['gate'][:, s] + R.gab_ref[:, s])) * (1.0 / GLA_GATE_NORMALIZER)
            if n_valid < C:
                la = jnp.where(R.trow < n_valid, la, 0.0)
            return _cumsum_rows(la)

        b = [cum_log_decay(s) for s in ks]
        k = [R.gk_ref[bi, :, s].astype(F32) for s in ks]
        st['v'] = [R.gv_ref[bi, :, s].astype(F32) for s in vsl]
        qd = [R.gq_ref[bi, :, s].astype(F32) * (GLA_DK ** -0.5) * jnp.exp(bh) for s, bh in zip(ks, b)]
        kd = [kh * jnp.exp(-bh) for kh, bh in zip(k, b)]
        kl = [kh * jnp.exp(bh[C - 1:C, :] - bh) for kh, bh in zip(k, b)]
        St = [R.g_scr[i] for i in scr]
        st['att'] = [jnp.where(R.causal, _mm_nt(x, y), 0.0) for x, y in zip(qd, kd)]
        st['o_s'] = [_mm_nt(x, s) for x, s in zip(qd, St)]
        upd = [_mm_tn(x, y) for x, y in zip(st['v'], kl)]
        for i, s_old, bh, u in zip(scr, St, b, upd):
            R.g_scr[i] = s_old * jnp.exp(bh[C - 1:C, :]) + u

    def outputs():
        o = [_mm(x, y) + z for x, y, z in zip(st['att'], st['v'], st['o_s'])]
        res[bi] = [oh * lax.rsqrt(jnp.mean(oh * oh, axis=-1, keepdims=True) + GLA_NORM_EPS) * R.gg_ref[...]
                   for oh in o]

    return [gate, scores, outputs]


def _mixer_kernel(*refs, C, BB, n_valid, has_s0, has_vres, layer, copy_other_layers):
    R = _Shared()
    it = iter(refs)
    R.r_ref, R.k_ref, R.v_ref, R.lo_ref = next(it), next(it), next(it), next(it)
    if has_vres:
        R.vlo_ref, R.vf_ref = next(it), next(it)
    R.gq_ref, R.gk_ref, R.gv_ref, R.glo_ref = next(it), next(it), next(it), next(it)
    ga_ref, gb_ref, h_ref = next(it), next(it), next(it)
    if has_s0:
        s0w_ref, s0g_ref = next(it), next(it)
    R.w2a_ref = next(it)
    if has_vres:
        R.vw2_ref = next(it)
    R.vec_ref, R.ga2_ref, R.gab_ref, R.gg_ref, wo_ref = next(it), next(it), next(it), next(it), next(it)
    o_ref, sow_ref, sog_ref, R.s_scr, R.g_scr = next(it), next(it), next(it), next(it), next(it)

    c = pl.program_id(1)
    H = RWKV_HEAD_DIM
    rows = range(BB)
    if has_s0 and copy_other_layers:
        s0w_all, s0g_all, sow_all, sog_all = s0w_ref, s0g_ref, sow_ref, sog_ref
        s0w_ref, s0g_ref = s0w_all.at[layer], s0g_all.at[layer]
        sow_ref, sog_ref = sow_all.at[layer], sog_all.at[layer]

    @pl.when(c == 0)
    def _():
        R.s_scr[...] = jnp.zeros_like(R.s_scr)
        if has_s0:
            for bi in rows:
                for p in range(N_PAIRS):
                    R.s_scr[bi * N_PAIRS + p, 0:H, 0:H] = s0w_ref[bi, 2 * p]
                    R.s_scr[bi * N_PAIRS + p, H:2 * H, H:2 * H] = s0w_ref[bi, 2 * p + 1]
                for h in range(GLA_HEADS):
                    R.g_scr[bi * GLA_HEADS + h] = s0g_ref[bi, h].T
        else:
            R.g_scr[...] = jnp.zeros_like(R.g_scr)

    R.head0 = lax.broadcasted_iota(jnp.int32, (C, LANES), 1) < H
    R.trow = lax.broadcasted_iota(jnp.int32, (C, LANES), 0)
    ri = lax.broadcasted_iota(jnp.int32, (C, C), 0)
    ci = lax.broadcasted_iota(jnp.int32, (C, C), 1)
    R.causal = ri >= ci
    if 2 * C == LANES:
        ri = lax.broadcasted_iota(jnp.int32, (2 * LANES, 2 * LANES), 0)
        ci = lax.broadcasted_iota(jnp.int32, (2 * LANES, 2 * LANES), 1)
        rr, cc = ri % LANES, ci % LANES
        R.mask4 = ((rr // C) == (cc // C)) & ((rr % C) >= (cc % C) + (ri < LANES).astype(jnp.int32))
        r1 = lax.broadcasted_iota(jnp.int32, (LANES, LANES), 0)
        c1 = lax.broadcasted_iota(jnp.int32, (LANES, LANES), 1)
        R.eye = (r1 == c1).astype(F32)
    else:
        ri = lax.broadcasted_iota(jnp.int32, (2 * C, 2 * C), 0)
        ci = lax.broadcasted_iota(jnp.int32, (2 * C, 2 * C), 1)
        same_head = (ri // C) == (ci // C)
        R.strict = same_head & ((ri % C) > (ci % C))
        R.incl = same_head & ((ri % C) >= (ci % C))
        R.eye = (ri == ci).astype(F32)

    ro, go = {}, {}
    plans = []
    for bi in rows:
        plans.append(_rwkv_row_stages(bi, R, ro, C=C, n_valid=n_valid, has_vres=has_vres))
        plans.append(_gla_row_stages(bi, R, go, C=C, n_valid=n_valid))
    for d in range(max(len(stages) for stages in plans)):
        for stages in plans:
            if d < len(stages):
                stages[d]()

    merged = [_sigmoid(ga_ref[bi].astype(F32)) * jnp.concatenate(ro[bi], axis=1)
              + _sigmoid(gb_ref[bi].astype(F32)) * jnp.concatenate(go[bi], axis=1) for bi in rows]
    mix = _mm(jnp.concatenate(merged, axis=0).astype(BF16), wo_ref[...])
    for bi in rows:
        o_ref[bi] = h_ref[bi] + mix[bi * C:(bi + 1) * C]

    @pl.when(c == pl.num_programs(1) - 1)
    def _():
        for bi in rows:
            for p in range(N_PAIRS):
                sow_ref[bi, 2 * p] = R.s_scr[bi * N_PAIRS + p, 0:H, 0:H]
                sow_ref[bi, 2 * p + 1] = R.s_scr[bi * N_PAIRS + p, H:2 * H, H:2 * H]
            for h in range(GLA_HEADS):
                sog_ref[bi, h] = R.g_scr[bi * GLA_HEADS + h].T
        if has_s0 and copy_other_layers:
            for other in range(s0w_all.shape[0]):
                if other != layer:
                    sow_all[other] = s0w_all[other]
                    sog_all[other] = s0g_all[other]


def _mixer_call(P, Pfirst, h, s_wkv, s_gla, layer, lw, *, C, n_valid, BB):
    B, T, _ = P.shape
    has_vres = Pfirst is not None
    has_s0 = s_wkv is not None
    kw = GLA_HEADS * GLA_DK

    def cols(off, width):
        idx = off // width
        return pl.BlockSpec((BB, C, width), lambda b, c: (b, c, idx))

    whole = lambda shape: pl.BlockSpec(shape, lambda b, c: (0,) * len(shape))
    wkv_block = (BB, RWKV_HEADS, RWKV_HEAD_DIM, RWKV_HEAD_DIM)
    gla_block = (BB, GLA_HEADS, GLA_DK, GLA_DV)
    copy_other_layers = has_s0 and layer == 0
    if copy_other_layers:
        depth = s_wkv.shape[0]
        state_spec = lambda blk: pl.BlockSpec((depth,) + blk, lambda b, c: (0, b, 0, 0, 0))
        wkv_shape, gla_shape = s_wkv.shape, s_gla.shape
    elif has_s0:
        state_spec = lambda blk: pl.BlockSpec((None,) + blk, lambda b, c: (layer, b, 0, 0, 0))
        wkv_shape, gla_shape = s_wkv.shape, s_gla.shape
    else:
        state_spec = lambda blk: pl.BlockSpec(blk, lambda b, c: (b, 0, 0, 0))
        wkv_shape, gla_shape = (B,) + wkv_block[1:], (B,) + gla_block[1:]

    in_specs = [cols(COL_R, D_MODEL), cols(COL_K, D_MODEL), cols(COL_V, D_MODEL), cols(COL_LO, LANES)]
    args = [P, P, P, P]
    if has_vres:
        in_specs += [cols(COL_VLO, LANES), cols(COL_V, D_MODEL)]
        args += [P, Pfirst]
    in_specs += [cols(COL_GQ, kw), cols(COL_GK, kw), cols(COL_GV, D_MODEL), cols(COL_GLO, LANES),
                 cols(COL_GATE_A, D_MODEL), cols(COL_GATE_B, D_MODEL), cols(0, D_MODEL)]
    args += [P, P, P, P, P, P, h]
    aliases = {}
    if has_s0:
        if not copy_other_layers:
            aliases = {len(args): 1, len(args) + 1: 2}
        in_specs += [state_spec(wkv_block), state_spec(gla_block)]
        args += [s_wkv, s_gla]
    in_specs.append(whole((LANES, D_MODEL)))
    args.append(lw['w2a'])
    if has_vres:
        in_specs.append(whole((LANES, D_MODEL)))
        args.append(lw['vw2'])
    in_specs += [whole((8, D_MODEL)), whole((LANES, kw)), whole((1, kw)), whole((1, GLA_DV)),
                 whole((D_MODEL, D_MODEL))]
    args += [lw['vecs'], lw['ga2'], lw['gab'], lw['gla_g'], lw['w_o']]
    return pl.pallas_call(
        functools.partial(_mixer_kernel, C=C, BB=BB, n_valid=n_valid, has_s0=has_s0, has_vres=has_vres,
                          layer=layer, copy_other_layers=copy_other_layers),
        grid=(B // BB, T // C),
        in_specs=in_specs,
        out_specs=[pl.BlockSpec((BB, C, D_MODEL), lambda b, c: (b, c, 0)),
                   state_spec(wkv_block), state_spec(gla_block)],
        out_shape=[jax.ShapeDtypeStruct((B, T, D_MODEL), F32),
                   jax.ShapeDtypeStruct(wkv_shape, F32),
                   jax.ShapeDtypeStruct(gla_shape, F32)],
        input_output_aliases=aliases,
        scratch_shapes=[pltpu.VMEM((BB * N_PAIRS, LANES, LANES), F32),
                        pltpu.VMEM((BB * GLA_HEADS, GLA_DV, GLA_DK), F32)],
        compiler_params=_cparams(("parallel", "arbitrary")),
        name="mixer",
    )(*args)


FFN_SUB = 256


def _ffn_kernel(h_ref, g_ref, wg_ref, wu_ref, wd_ref, gf_ref, o_ref, *, final_norm):
    h = h_ref[...]
    xs = _rms_rows(h, g_ref[...]).astype(BF16)
    out = h
    for n in range(FFN_HIDDEN // FFN_SUB):
        cs = slice(n * FFN_SUB, (n + 1) * FFN_SUB)
        gate = _mm(xs, wg_ref[:, cs])
        up = _mm(xs, wu_ref[:, cs])
        act = (gate * _sigmoid(gate) * up).astype(BF16)
        out = out + _mm(act, wd_ref[cs, :])
    if final_norm:
        out = _rms_rows(out, gf_ref[...])
    o_ref[...] = out


def _ffn_call(h, g, w_gu, w_down, gf, final_norm):
    m = h.shape[0]
    tm = min(512, m)
    row = lambda i: (i, 0)
    return pl.pallas_call(
        functools.partial(_ffn_kernel, final_norm=final_norm),
        grid=(m // tm,),
        in_specs=[pl.BlockSpec((tm, D_MODEL), row),
                  pl.BlockSpec((1, D_MODEL), lambda i: (0, 0)),
                  pl.BlockSpec((D_MODEL, FFN_HIDDEN), lambda i: (0, 0)),
                  pl.BlockSpec((D_MODEL, FFN_HIDDEN), lambda i: (0, 1)),
                  pl.BlockSpec((FFN_HIDDEN, D_MODEL), lambda i: (0, 0)),
                  pl.BlockSpec((1, D_MODEL), lambda i: (0, 0))],
        out_specs=pl.BlockSpec((tm, D_MODEL), row),
        out_shape=jax.ShapeDtypeStruct((m, D_MODEL), F32),
        compiler_params=_cparams(("parallel",)),
        name="ffn",
    )(h, g, w_gu, w_gu, w_down, gf)


def _rms_kernel(x_ref, g_ref, o_ref):
    o_ref[...] = _rms_rows(x_ref[...], g_ref[...])


def _rms_call(x, g):
    return pl.pallas_call(
        _rms_kernel,
        out_shape=jax.ShapeDtypeStruct(x.shape, F32),
        name="rmsnorm_rows",
    )(x, g)


SRC_GQ, SRC_GK, SRC_GV, SRC_GLO, SRC_GATE_A, SRC_GATE_B = 3200, 3712, 4224, 5248, 5264, 6288
WCAT_ROWS = 256


def _wcat_kernel(w_ref, v1_ref, o_ref):
    def put(dst, src, width):
        o_ref[:, dst:dst + width] = w_ref[:, src:src + width].astype(BF16)

    def zero(lo, hi):
        o_ref[:, lo:hi] = jnp.zeros((o_ref.shape[0], hi - lo), BF16)

    put(COL_R, 0, COL_LO + 2 * LORA_DECAY)
    o_ref[:, COL_VLO:COL_VLO + LORA_VRES] = v1_ref[...].astype(BF16)
    zero(COL_VLO + LORA_VRES, COL_GLO)
    put(COL_GLO, SRC_GLO, GLA_GATE_RANK)
    zero(COL_GLO + GLA_GATE_RANK, COL_GQ)
    put(COL_GQ, SRC_GQ, GLA_HEADS * GLA_DK)
    put(COL_GATE_A, SRC_GATE_A, D_MODEL)
    put(COL_GATE_B, SRC_GATE_B, D_MODEL)
    put(COL_GV, SRC_GV, GLA_HEADS * GLA_DV)
    put(COL_GK, SRC_GK, GLA_HEADS * GLA_DK)


def _wcat_call(w_in, vres_w1_padded):
    depth, _, n_in = w_in.shape
    return pl.pallas_call(
        _wcat_kernel,
        grid=(depth, D_MODEL // WCAT_ROWS),
        in_specs=[pl.BlockSpec((None, WCAT_ROWS, n_in), lambda l, i: (l, i, 0)),
                  pl.BlockSpec((None, WCAT_ROWS, LORA_VRES), lambda l, i: (l, i, 0))],
        out_specs=pl.BlockSpec((None, WCAT_ROWS, N_PROJ), lambda l, i: (l, i, 0)),
        out_shape=jax.ShapeDtypeStruct((depth, D_MODEL, N_PROJ), BF16),
        compiler_params=_cparams(("parallel", "parallel")),
        name="w_in_reorder",
    )(w_in, vres_w1_padded)


def _pad_rows(x, n):
    return jnp.pad(x, ((0, n - x.shape[0]), (0, 0)))


def _layer_weights(l, mu_shift, vres_mu, p):
    mu = mu_shift[l]
    zeros_mu = lambda n: jnp.zeros((n,), F32)
    vlo_mu = jnp.pad(vres_mu[l - 1], (0, LANES - LORA_VRES)) if l > 0 else zeros_mu(LANES)
    mu_cat = jnp.concatenate([mu[:COL_VLO], vlo_mu, zeros_mu(N_PROJ - N_SHIFTED)])[None, :]
    lw = dict(mu_cat=mu_cat)
    lw['w2a'] = jnp.concatenate([p['rwkv_w2'][l], p['rwkv_a2'][l]], axis=0)
    lw['vecs'] = jnp.stack([p['rwkv_w0'][l], p['rwkv_a0'][l], p['rwkv_k_k'][l], p['rwkv_k_a'][l],
                            p['rwkv_r_k'][l].reshape(-1), p['rwkv_lnx_g'][l], p['rwkv_lnx_b'][l],
                            p['vres_v0'][l - 1] if l > 0 else zeros_mu(D_MODEL)], axis=0)
    lw['vw2'] = _pad_rows(p['vres_w2'][l - 1], LANES) if l > 0 else None
    lw['ga2'] = _pad_rows(p['gla_a2'][l], LANES)
    lw['gab'] = p['gla_ab'][l][None, :]
    lw['gla_g'] = p['gla_norm_g'][l][None, :]
    lw['w_o'] = p['w_o'][l].astype(BF16)
    lw['w_gu'] = p['ffn_w_gu'][l].astype(BF16)
    lw['w_down'] = p['ffn_w_down'][l].astype(BF16)
    lw['g_mix'] = p['norm_mix'][l][None, :]
    lw['g_ffn'] = p['norm_ffn'][l][None, :]
    return lw


def _rows_per_step(batch, want):
    while batch % want:
        want //= 2
    return want


def _run_group(x, shift_state, wkv_state, gla_state, w_cat, weights, g_final):
    B, T, _ = x.shape
    fresh = shift_state is None
    C = CHUNK if fresh else SAMPLE_CHUNK
    Tp = -(-T // C) * C
    BB = _rows_per_step(B, ROWS_PER_STEP)
    h = x.reshape(B * T, D_MODEL)
    new_shift, new_wkv, new_gla = [], [], []
    P_first = None
    depth = len(weights)
    for l, lw in enumerate(weights):
        new_shift.append(_rms_call(h.reshape(B, T, D_MODEL)[:, -1], lw['g_mix']))
        P = _proj_call(h, lw['g_mix'], w_cat, l, lw['mu_cat'], None if fresh else shift_state[l], T)
        P = jnp.pad(P.reshape(B, T, N_PROJ), ((0, 0), (0, Tp - T), (0, 0)))
        h3 = jnp.pad(h.reshape(B, T, D_MODEL), ((0, 0), (0, Tp - T), (0, 0)))
        h3, s_wkv, s_gla = _mixer_call(P, P_first if l > 0 else None, h3, wkv_state, gla_state, l, lw,
                                       C=C, n_valid=min(T, C), BB=BB)
        if l == 0:
            P_first = P
        if fresh:
            new_wkv.append(s_wkv)
            new_gla.append(s_gla)
        else:
            wkv_state, gla_state = s_wkv, s_gla
        h = h3[:, :T].reshape(B * T, D_MODEL)
        h = _ffn_call(h, lw['g_ffn'], lw['w_gu'], lw['w_down'], g_final, final_norm=(l == depth - 1))
    if fresh:
        wkv_state, gla_state = jnp.stack(new_wkv), jnp.stack(new_gla)
    return (h.reshape(B, T, D_MODEL), jnp.stack(new_shift), wkv_state, gla_state)


def kernel(x_prompt, x_sample, state_shift, state_wkv, state_gla, norm_mix, w_in, mu_shift, rwkv_w0, rwkv_w2, rwkv_a0, rwkv_a2, rwkv_k_k, rwkv_k_a, rwkv_r_k, rwkv_lnx_g, rwkv_lnx_b, vres_w1, vres_mu, vres_w2, vres_v0, gla_a2, gla_ab, gla_norm_g, w_o, norm_ffn, ffn_w_gu, ffn_w_down, norm_final):
    p = dict(norm_mix=norm_mix, rwkv_w0=rwkv_w0, rwkv_w2=rwkv_w2, rwkv_a0=rwkv_a0, rwkv_a2=rwkv_a2,
             rwkv_k_k=rwkv_k_k, rwkv_k_a=rwkv_k_a, rwkv_r_k=rwkv_r_k, rwkv_lnx_g=rwkv_lnx_g,
             rwkv_lnx_b=rwkv_lnx_b, vres_w2=vres_w2, vres_v0=vres_v0, gla_a2=gla_a2, gla_ab=gla_ab,
             gla_norm_g=gla_norm_g, w_o=w_o, norm_ffn=norm_ffn, ffn_w_gu=ffn_w_gu, ffn_w_down=ffn_w_down)
    depth = w_in.shape[0]
    weights = [_layer_weights(l, mu_shift, vres_mu, p) for l in range(depth)]
    w_cat = _wcat_call(w_in, jnp.concatenate([jnp.zeros_like(vres_w1[:1]), vres_w1], axis=0))
    g_final = norm_final[None, :]
    y_p, shift_p, wkv_p, gla_p = _run_group(x_prompt, None, None, None, w_cat, weights, g_final)
    y_s, shift_s, wkv_s, gla_s = _run_group(x_sample, state_shift, state_wkv, state_gla, w_cat, weights,
                                            g_final)
    return (y_p, y_s, shift_p, wkv_p, gla_p, shift_s, wkv_s, gla_s)
```

```python
import functools

import jax
import jax.numpy as jnp
from jax import lax
from jax.experimental import pallas as pl
from jax.experimental.pallas import tpu as pltpu

F32 = jnp.float32
BF16 = jnp.bfloat16

LANES = 128
D_MODEL = 1024
RWKV_HEAD_DIM = 64
RWKV_HEADS = D_MODEL // RWKV_HEAD_DIM
HEADS_PER_PAIR = LANES // RWKV_HEAD_DIM
N_PAIRS = RWKV_HEADS // HEADS_PER_PAIR
LORA_DECAY = 64
LORA_ICLR = 64
LORA_VRES = 32
RWKV_GN_EPS = 64e-5
GLA_HEADS = 4
GLA_DK = 128
GLA_DV = 256
GLA_GATE_RANK = 16
GLA_GATE_NORMALIZER = 16.0
GLA_NORM_EPS = 1e-5
FFN_HIDDEN = 2816
NORM_EPS = 1e-6
CHUNK = 64
SAMPLE_CHUNK = 8

COL_R, COL_K, COL_V = 0, 1024, 2048
COL_LO, COL_VLO, COL_GLO = 3072, 3200, 3328
COL_GQ = 3584
COL_GATE_A, COL_GATE_B = 4096, 5120
COL_GV, COL_GK = 6144, 7168
N_SHIFTED = 3328
N_PROJ = 7680
PROJ_TM = 512
PROJ_SUB = 512
PROMPT_PROJ_DTYPE = BF16
VMEM_LIMIT = 48 * 1024 * 1024
ROWS_PER_STEP = 2
DECAY_SCALE = 0.6065306597126334


def _cparams(sem):
    return pltpu.CompilerParams(dimension_semantics=sem, vmem_limit_bytes=VMEM_LIMIT)


def _mm(a, b):
    return jnp.dot(a, b, preferred_element_type=F32)


def _mm_nt(a, b):
    return lax.dot_general(a, b, (((1,), (1,)), ((), ())), preferred_element_type=F32)


def _mm_tn(a, b):
    return lax.dot_general(a, b, (((0,), (0,)), ((), ())), preferred_element_type=F32)


def _sigmoid(x):
    return 0.5 * jnp.tanh(0.5 * x) + 0.5


def _softplus(x):
    return jnp.maximum(x, 0.0) + jnp.log(1.0 + jnp.exp(-jnp.abs(x)))


def _cumsum_rows(x):
    n = x.shape[0]
    row = lax.broadcasted_iota(jnp.int32, x.shape, 0)
    s = 1
    while s < n:
        x = x + jnp.where(row >= s, pltpu.roll(x, s, 0), 0.0)
        s *= 2
    return x


def _rms_rows(x, g):
    ms = jnp.mean(x * x, axis=-1, keepdims=True)
    return x * lax.rsqrt(ms + NORM_EPS) * g


def _proj_kernel(*refs, sample, tiles_per_seq, tm):
    if sample:
        x_ref, g_ref, w_ref, mu_ref, prev_ref, o_ref = refs
    else:
        x_ref, g_ref, w_ref, mu_ref, o_ref, last_ref = refs
    xs = _rms_rows(x_ref[...], g_ref[...]).astype(BF16)
    subs = [slice(n * PROJ_SUB, (n + 1) * PROJ_SUB) for n in range(N_PROJ // PROJ_SUB)]
    n_shift = -(-N_SHIFTED // PROJ_SUB)
    if sample:
        ps = prev_ref[...].astype(BF16)
        for n, cs in enumerate(subs):
            y = _mm(xs, w_ref[:, cs])
            if n < n_shift:
                y = y + mu_ref[:, cs] * (_mm(ps, w_ref[:, cs]) - y)
            o_ref[:, cs] = y.astype(o_ref.dtype)
        return

    i = pl.program_id(0)

    @pl.when(i == 0)
    def _():
        last_ref[...] = jnp.zeros_like(last_ref)

    row = lax.broadcasted_iota(jnp.int32, (tm, PROJ_SUB), 0)
    seq_start = i % tiles_per_seq == 0
    for n, cs in enumerate(subs):
        y = _mm(xs, w_ref[:, cs])
        if n < n_shift:
            prev_row = jnp.where(seq_start, 0.0, last_ref[7:8, cs])
            shifted = jnp.where(row == 0, prev_row, pltpu.roll(y, 1, 0))
            last_ref[:, cs] = y[tm - 8:, :]
            y = y + mu_ref[:, cs] * (shifted - y)
        o_ref[:, cs] = y.astype(o_ref.dtype)


def _proj_call(x, g, w, layer, mu, prev, seq_len):
    m = x.shape[0]
    sample = prev is not None
    tm = min(PROJ_TM, m if sample else seq_len)
    row = lambda i: (i, 0)
    in_specs = [
        pl.BlockSpec((tm, D_MODEL), row),
        pl.BlockSpec((1, D_MODEL), lambda i: (0, 0)),
        pl.BlockSpec((None, D_MODEL, N_PROJ), lambda i: (layer, 0, 0), pipeline_mode=pl.Buffered(1)),
        pl.BlockSpec((1, N_PROJ), lambda i: (0, 0)),
    ]
    args = [x, g, w, mu]
    scratch = []
    if sample:
        in_specs.append(pl.BlockSpec((tm, D_MODEL), row))
        args.append(prev)
    else:
        scratch.append(pltpu.VMEM((8, -(-N_SHIFTED // PROJ_SUB) * PROJ_SUB), F32))
    return pl.pallas_call(
        functools.partial(_proj_kernel, sample=sample, tiles_per_seq=max(seq_len // tm, 1), tm=tm),
        grid=(m // tm,),
        in_specs=in_specs,
        out_specs=pl.BlockSpec((tm, N_PROJ), row),
        out_shape=jax.ShapeDtypeStruct((m, N_PROJ), F32 if sample else PROMPT_PROJ_DTYPE),
        scratch_shapes=scratch,
        compiler_params=_cparams(("arbitrary",)),
        name="proj_sample" if sample else "proj_prompt",
    )(*args)


class _Shared:
    pass


def _rwkv_row_stages(bi, R, res, *, C, n_valid, has_vres):
    H = RWKV_HEAD_DIM
    head0, trow = R.head0, R.trow
    sls = [slice(p * LANES, (p + 1) * LANES) for p in range(N_PAIRS)]
    scr = [bi * N_PAIRS + p for p in range(N_PAIRS)]
    n_sq = C.bit_length() - 2
    fused = 2 * C == LANES
    st = {}
    bf = lambda x: x.astype(BF16)
    cat0 = lambda x, y: jnp.concatenate([x, y], axis=0)
    cat1 = lambda x, y: jnp.concatenate([x, y], axis=1)

    def each(f, *lists):
        return [f(*xs) for xs in zip(*lists)]

    def stack(x):
        return jnp.concatenate([jnp.where(head0, x, 0.0), jnp.where(head0, 0.0, x)], axis=0)

    def head_sum(x):
        s0 = jnp.sum(jnp.where(head0, x, 0.0), axis=-1, keepdims=True)
        s1 = jnp.sum(jnp.where(head0, 0.0, x), axis=-1, keepdims=True)
        return jnp.where(head0, s0, s1)

    def lora():
        lo = R.lo_ref[bi].astype(F32)
        w2a = R.w2a_ref[...]
        st['wl'] = _mm(jnp.where(head0, jnp.tanh(lo), 0.0), w2a)
        st['al'] = _mm(jnp.where(head0, 0.0, lo), w2a)
        if has_vres:
            st['vg'] = _mm(R.vlo_ref[bi].astype(F32), R.vw2_ref[...])

    def operands():
        vec = lambda row: [R.vec_ref[row:row + 1, sl] for sl in sls]
        w0, a0, k_k, k_a, r_k, lnx_g, lnx_b, v0 = (vec(i) for i in range(8))
        r = [R.r_ref[bi, :, sl].astype(F32) for sl in sls]
        k = [R.k_ref[bi, :, sl].astype(F32) for sl in sls]
        v = [R.v_ref[bi, :, sl].astype(F32) for sl in sls]
        if has_vres:
            v = [vp + (R.vf_ref[bi, :, sl].astype(F32) - vp) * _sigmoid(z + st['vg'][:, sl])
                 for vp, sl, z in zip(v, sls, v0)]

        def decay_log(w0p, sl):
            ld = -DECAY_SCALE * _sigmoid(w0p + st['wl'][:, sl])
            if n_valid < C:
                ld = jnp.where(trow < n_valid, ld, 0.0)
            return ld

        ld = each(decay_log, w0, sls)
        a = each(lambda z, sl: _sigmoid(z + st['al'][:, sl]), a0, sls)

        def unit_kk(kp, kkp):
            kk = kp * kkp
            return kk * lax.rsqrt(jnp.maximum(head_sum(kk * kk), 1e-24))

        kk = each(unit_kk, k, k_k)
        k = each(lambda kp, ap, kap: kp * (1.0 + (ap - 1.0) * kap), k, a, k_a)
        b = each(_cumsum_rows, ld)
        e_b = each(jnp.exp, b)
        e_nb = each(lambda x: jnp.exp(-x), b)
        st['al_s'] = each(lambda kkp, bp, ldp: bf(stack(-kkp * jnp.exp(bp - ldp))), kk, b, ld)
        st['be'] = each(lambda ap, kkp, e: bf(stack(ap * kkp * e)), a, kk, e_nb)
        st['kt'] = each(lambda kp, e: bf(stack(kp * e)), k, e_nb)
        st['rt'] = each(lambda rp, e: bf(stack(rp * e)), r, e_b)
        st['vs'] = each(lambda x: bf(stack(x)), v)
        st['decay'] = [e[C - 1:C, :] for e in e_b]
        st['S32'] = [R.s_scr[i] for i in scr]
        st['epi'] = (r, k, v, r_k, lnx_g, lnx_b)

    def finish(o2, s_new):
        for i, s in zip(scr, s_new):
            R.s_scr[i] = s
        r, k, v, r_k, lnx_g, lnx_b = st['epi']
        outs = []
        for p in range(N_PAIRS):
            o = o2[p][:C] + o2[p][C:]
            mean = head_sum(o) * (1.0 / H)
            d = o - mean
            var = head_sum(d * d) * (1.0 / H)
            o = d * lax.rsqrt(var + RWKV_GN_EPS) * lnx_g[p] + lnx_b[p]
            outs.append(o + head_sum(r[p] * k[p] * r_k[p]) * v[p])
        res[bi] = outs

    if fused:
        def amat():
            operands()
            st['ar'] = each(cat0, st['al_s'], st['rt'])
            st['bk'] = each(cat0, st['be'], st['kt'])
            st['amat'] = each(lambda x, y: jnp.where(R.mask4, _mm_nt(x, y), 0.0), st['ar'], st['bk'])

        def xq():
            sv = each(lambda s, y: cat0(bf(s.T), y), st['S32'], st['vs'])
            st['x'] = each(lambda l, m, rhs: _mm(cat1(l, bf(m[:, LANES:])), rhs), st['ar'], st['amat'], sv)
            st['tinv'] = each(lambda m: R.eye + m[:LANES, :LANES], st['amat'])
            q = each(lambda m: bf(m[:LANES, :LANES]), st['amat'])
            st['q'] = each(lambda z: bf(_mm(z, z)), q)

        def double():
            y = each(lambda z, t: _mm(z, cat1(bf(t), z)), st['q'], st['tinv'])
            st['tinv'] = each(lambda t, z: t + z[:, :LANES], st['tinv'], y)
            st['q'] = each(lambda z: bf(z[:, LANES:]), y)

        def double_last():
            st['tinv'] = each(lambda t, z: t + _mm(z, bf(t)), st['tinv'], st['q'])

        def solve():
            st['cm'] = each(lambda t, z: bf(_mm(bf(t), bf(z[:LANES]))), st['tinv'], st['x'])

        def out_state():
            o2 = each(lambda z, m, cc: z[LANES:] + _mm(bf(m[LANES:, :LANES]), cc),
                      st['x'], st['amat'], st['cm'])
            s_new = each(lambda s, cc, y, rhs, e: (s + _mm_tn(cat0(cc, y), rhs)) * e,
                         st['S32'], st['cm'], st['vs'], st['bk'], st['decay'])
            finish(o2, s_new)

        return [lora, amat, xq] + [double] * (n_sq - 1) + [double_last, solve, out_state]

    def small_a():
        operands()
        S = each(bf, st['S32'])
        st['S'] = S
        a_ak = each(lambda x, y: bf(jnp.where(R.strict, _mm_nt(x, y), 0.0)), st['al_s'], st['kt'])
        st['a_rb'] = each(lambda x, y: bf(jnp.where(R.incl, _mm_nt(x, y), 0.0)), st['rt'], st['be'])
        st['a_rk'] = each(lambda x, y: bf(jnp.where(R.incl, _mm_nt(x, y), 0.0)), st['rt'], st['kt'])
        st['cm'] = each(lambda x, s, m, y: _mm_nt(x, s) + _mm(m, y), st['al_s'], S, a_ak, st['vs'])

    def small_solve():
        pw = each(lambda x, y: jnp.where(R.strict, _mm_nt(x, y), 0.0), st['al_s'], st['be'])
        tinv = each(lambda x: R.eye + x, pw)
        pwb = each(bf, pw)
        for _ in range(n_sq):
            pwb = each(lambda x: bf(_mm(x, x)), pwb)
            tinv = each(lambda t, x: t + _mm(bf(t), x), tinv, pwb)
        st['cm'] = each(lambda t, x: _mm(bf(t), bf(x)), tinv, st['cm'])

    def small_out():
        cm = each(bf, st['cm'])
        o2 = each(lambda x, s, m1, c1, m2, y: _mm_nt(x, s) + _mm(m1, c1) + _mm(m2, y),
                  st['rt'], st['S'], st['a_rb'], cm, st['a_rk'], st['vs'])
        s_new = each(lambda s, c1, x, y, z, e: (s + _mm_tn(c1, x) + _mm_tn(y, z)) * e,
                     st['S32'], cm, st['be'], st['vs'], st['kt'], st['decay'])
        finish(o2, s_new)

    return [lora, small_a] + ([small_solve] if n_valid > 1 else []) + [small_out]


def _gla_row_stages(bi, R, res, *, C, n_valid):
    heads = range(GLA_HEADS)
    ks = [slice(h * GLA_DK, (h + 1) * GLA_DK) for h in heads]
    vsl = [slice(h * GLA_DV, (h + 1) * GLA_DV) for h in heads]
    scr = [bi * GLA_HEADS + h for h in heads]
    st = {}

    def gate():
        st['gate'] = _mm(R.glo_ref[bi].astype(F32), R.ga2_ref[...])

    def scores():
        def cum_log_decay(s):
            la = -_softplus(-(st['gate'][:, s] + R.gab_ref[:, s])) * (1.0 / GLA_GATE_NORMALIZER)
            if n_valid < C:
                la = jnp.where(R.trow < n_valid, la, 0.0)
            return _cumsum_rows(la)

        b = [cum_log_decay(s) for s in ks]
        k = [R.gk_ref[bi, :, s].astype(F32) for s in ks]
        st['v'] = [R.gv_ref[bi, :, s].astype(F32) for s in vsl]
        qd = [R.gq_ref[bi, :, s].astype(F32) * (GLA_DK ** -0.5) * jnp.exp(bh) for s, bh in zip(ks, b)]
        kd = [kh * jnp.exp(-bh) for kh, bh in zip(k, b)]
        kl = [kh * jnp.exp(bh[C - 1:C, :] - bh) for kh, bh in zip(k, b)]
        St = [R.g_scr[i] for i in scr]
        st['att'] = [jnp.where(R.causal, _mm_nt(x, y), 0.0) for x, y in zip(qd, kd)]
        st['o_s'] = [_mm_nt(x, s) for x, s in zip(qd, St)]
        upd = [_mm_tn(x, y) for x, y in zip(st['v'], kl)]
        for i, s_old, bh, u in zip(scr, St, b, upd):
            R.g_scr[i] = s_old * jnp.exp(bh[C - 1:C, :]) + u

    def outputs():
        o = [_mm(x, y) + z for x, y, z in zip(st['att'], st['v'], st['o_s'])]
        res[bi] = [oh * lax.rsqrt(jnp.mean(oh * oh, axis=-1, keepdims=True) + GLA_NORM_EPS) * R.gg_ref[...]
                   for oh in o]

    return [gate, scores, outputs]


def _mixer_kernel(*refs, C, BB, n_valid, has_s0, has_vres, layer, copy_other_layers):
    R = _Shared()
    it = iter(refs)
    R.r_ref, R.k_ref, R.v_ref, R.lo_ref = next(it), next(it), next(it), next(it)
    if has_vres:
        R.vlo_ref, R.vf_ref = next(it), next(it)
    R.gq_ref, R.gk_ref, R.gv_ref, R.glo_ref = next(it), next(it), next(it), next(it)
    ga_ref, gb_ref, h_ref = next(it), next(it), next(it)
    if has_s0:
        s0w_ref, s0g_ref = next(it), next(it)
    R.w2a_ref = next(it)
    if has_vres:
        R.vw2_ref = next(it)
    R.vec_ref, R.ga2_ref, R.gab_ref, R.gg_ref, wo_ref = next(it), next(it), next(it), next(it), next(it)
    o_ref, sow_ref, sog_ref, R.s_scr, R.g_scr = next(it), next(it), next(it), next(it), next(it)

    c = pl.program_id(1)
    H = RWKV_HEAD_DIM
    rows = range(BB)
    if has_s0 and copy_other_layers:
        s0w_all, s0g_all, sow_all, sog_all = s0w_ref, s0g_ref, sow_ref, sog_ref
        s0w_ref, s0g_ref = s0w_all.at[layer], s0g_all.at[layer]
        sow_ref, sog_ref = sow_all.at[layer], sog_all.at[layer]

    @pl.when(c == 0)
    def _():
        R.s_scr[...] = jnp.zeros_like(R.s_scr)
        if has_s0:
            for bi in rows:
                for p in range(N_PAIRS):
                    R.s_scr[bi * N_PAIRS + p, 0:H, 0:H] = s0w_ref[bi, 2 * p]
                    R.s_scr[bi * N_PAIRS + p, H:2 * H, H:2 * H] = s0w_ref[bi, 2 * p + 1]
                for h in range(GLA_HEADS):
                    R.g_scr[bi * GLA_HEADS + h] = s0g_ref[bi, h].T
        else:
            R.g_scr[...] = jnp.zeros_like(R.g_scr)

    R.head0 = lax.broadcasted_iota(jnp.int32, (C, LANES), 1) < H
    R.trow = lax.broadcasted_iota(jnp.int32, (C, LANES), 0)
    ri = lax.broadcasted_iota(jnp.int32, (C, C), 0)
    ci = lax.broadcasted_iota(jnp.int32, (C, C), 1)
    R.causal = ri >= ci
    if 2 * C == LANES:
        ri = lax.broadcasted_iota(jnp.int32, (2 * LANES, 2 * LANES), 0)
        ci = lax.broadcasted_iota(jnp.int32, (2 * LANES, 2 * LANES), 1)
        rr, cc = ri % LANES, ci % LANES
        R.mask4 = ((rr // C) == (cc // C)) & ((rr % C) >= (cc % C) + (ri < LANES).astype(jnp.int32))
        r1 = lax.broadcasted_iota(jnp.int32, (LANES, LANES), 0)
        c1 = lax.broadcasted_iota(jnp.int32, (LANES, LANES), 1)
        R.eye = (r1 == c1).astype(F32)
    else:
        ri = lax.broadcasted_iota(jnp.int32, (2 * C, 2 * C), 0)
        ci = lax.broadcasted_iota(jnp.int32, (2 * C, 2 * C), 1)
        same_head = (ri // C) == (ci // C)
        R.strict = same_head & ((ri % C) > (ci % C))
        R.incl = same_head & ((ri % C) >= (ci % C))
        R.eye = (ri == ci).astype(F32)

    ro, go = {}, {}
    plans = []
    for bi in rows:
        plans.append(_rwkv_row_stages(bi, R, ro, C=C, n_valid=n_valid, has_vres=has_vres))
        plans.append(_gla_row_stages(bi, R, go, C=C, n_valid=n_valid))
    for d in range(max(len(stages) for stages in plans)):
        for stages in plans:
            if d < len(stages):
                stages[d]()

    merged = [_sigmoid(ga_ref[bi].astype(F32)) * jnp.concatenate(ro[bi], axis=1)
              + _sigmoid(gb_ref[bi].astype(F32)) * jnp.concatenate(go[bi], axis=1) for bi in rows]
    mix = _mm(jnp.concatenate(merged, axis=0).astype(BF16), wo_ref[...])
    for bi in rows:
        o_ref[bi] = h_ref[bi] + mix[bi * C:(bi + 1) * C]

    @pl.when(c == pl.num_programs(1) - 1)
    def _():
        for bi in rows:
            for p in range(N_PAIRS):
                sow_ref[bi, 2 * p] = R.s_scr[bi * N_PAIRS + p, 0:H, 0:H]
                sow_ref[bi, 2 * p + 1] = R.s_scr[bi * N_PAIRS + p, H:2 * H, H:2 * H]
            for h in range(GLA_HEADS):
                sog_ref[bi, h] = R.g_scr[bi * GLA_HEADS + h].T
        if has_s0 and copy_other_layers:
            for other in range(s0w_all.shape[0]):
                if other != layer:
                    sow_all[other] = s0w_all[other]
                    sog_all[other] = s0g_all[other]


def _mixer_call(P, Pfirst, h, s_wkv, s_gla, layer, lw, *, C, n_valid, BB):
    B, T, _ = P.shape
    has_vres = Pfirst is not None
    has_s0 = s_wkv is not None
    kw = GLA_HEADS * GLA_DK

    def cols(off, width):
        idx = off // width
        return pl.BlockSpec((BB, C, width), lambda b, c: (b, c, idx))

    whole = lambda shape: pl.BlockSpec(shape, lambda b, c: (0,) * len(shape))
    wkv_block = (BB, RWKV_HEADS, RWKV_HEAD_DIM, RWKV_HEAD_DIM)
    gla_block = (BB, GLA_HEADS, GLA_DK, GLA_DV)
    copy_other_layers = has_s0 and layer == 0
    if copy_other_layers:
        depth = s_wkv.shape[0]
        state_spec = lambda blk: pl.BlockSpec((depth,) + blk, lambda b, c: (0, b, 0, 0, 0))
        wkv_shape, gla_shape = s_wkv.shape, s_gla.shape
    elif has_s0:
        state_spec = lambda blk: pl.BlockSpec((None,) + blk, lambda b, c: (layer, b, 0, 0, 0))
        wkv_shape, gla_shape = s_wkv.shape, s_gla.shape
    else:
        state_spec = lambda blk: pl.BlockSpec(blk, lambda b, c: (b, 0, 0, 0))
        wkv_shape, gla_shape = (B,) + wkv_block[1:], (B,) + gla_block[1:]

    in_specs = [cols(COL_R, D_MODEL), cols(COL_K, D_MODEL), cols(COL_V, D_MODEL), cols(COL_LO, LANES)]
    args = [P, P, P, P]
    if has_vres:
        in_specs += [cols(COL_VLO, LANES), cols(COL_V, D_MODEL)]
        args += [P, Pfirst]
    in_specs += [cols(COL_GQ, kw), cols(COL_GK, kw), cols(COL_GV, D_MODEL), cols(COL_GLO, LANES),
                 cols(COL_GATE_A, D_MODEL), cols(COL_GATE_B, D_MODEL), cols(0, D_MODEL)]
    args += [P, P, P, P, P, P, h]
    aliases = {}
    if has_s0:
        if not copy_other_layers:
            aliases = {len(args): 1, len(args) + 1: 2}
        in_specs += [state_spec(wkv_block), state_spec(gla_block)]
        args += [s_wkv, s_gla]
    in_specs.append(whole((LANES, D_MODEL)))
    args.append(lw['w2a'])
    if has_vres:
        in_specs.append(whole((LANES, D_MODEL)))
        args.append(lw['vw2'])
    in_specs += [whole((8, D_MODEL)), whole((LANES, kw)), whole((1, kw)), whole((1, GLA_DV)),
                 whole((D_MODEL, D_MODEL))]
    args += [lw['vecs'], lw['ga2'], lw['gab'], lw['gla_g'], lw['w_o']]
    return pl.pallas_call(
        functools.partial(_mixer_kernel, C=C, BB=BB, n_valid=n_valid, has_s0=has_s0, has_vres=has_vres,
                          layer=layer, copy_other_layers=copy_other_layers),
        grid=(B // BB, T // C),
        in_specs=in_specs,
        out_specs=[pl.BlockSpec((BB, C, D_MODEL), lambda b, c: (b, c, 0)),
                   state_spec(wkv_block), state_spec(gla_block)],
        out_shape=[jax.ShapeDtypeStruct((B, T, D_MODEL), F32),
                   jax.ShapeDtypeStruct(wkv_shape, F32),
                   jax.ShapeDtypeStruct(gla_shape, F32)],
        input_output_aliases=aliases,
        scratch_shapes=[pltpu.VMEM((BB * N_PAIRS, LANES, LANES), F32),
                        pltpu.VMEM((BB * GLA_HEADS, GLA_DV, GLA_DK), F32)],
        compiler_params=_cparams(("parallel", "arbitrary")),
        name="mixer",
    )(*args)


FFN_SUB = 256


def _ffn_kernel(h_ref, g_ref, wg_ref, wu_ref, wd_ref, gf_ref, o_ref, *, final_norm):
    h = h_ref[...]
    xs = _rms_rows(h, g_ref[...]).astype(BF16)
    out = h
    for n in range(FFN_HIDDEN // FFN_SUB):
        cs = slice(n * FFN_SUB, (n + 1) * FFN_SUB)
        gate = _mm(xs, wg_ref[:, cs])
        up = _mm(xs, wu_ref[:, cs])
        act = (gate * _sigmoid(gate) * up).astype(BF16)
        out = out + _mm(act, wd_ref[cs, :])
    if final_norm:
        out = _rms_rows(out, gf_ref[...])
    o_ref[...] = out


def _ffn_call(h, g, w_gu, w_down, gf, final_norm):
    m = h.shape[0]
    tm = min(512, m)
    row = lambda i: (i, 0)
    return pl.pallas_call(
        functools.partial(_ffn_kernel, final_norm=final_norm),
        grid=(m // tm,),
        in_specs=[pl.BlockSpec((tm, D_MODEL), row),
                  pl.BlockSpec((1, D_MODEL), lambda i: (0, 0)),
                  pl.BlockSpec((D_MODEL, FFN_HIDDEN), lambda i: (0, 0)),
                  pl.BlockSpec((D_MODEL, FFN_HIDDEN), lambda i: (0, 1)),
                  pl.BlockSpec((FFN_HIDDEN, D_MODEL), lambda i: (0, 0)),
                  pl.BlockSpec((1, D_MODEL), lambda i: (0, 0))],
        out_specs=pl.BlockSpec((tm, D_MODEL), row),
        out_shape=jax.ShapeDtypeStruct((m, D_MODEL), F32),
        compiler_params=_cparams(("parallel",)),
        name="ffn",
    )(h, g, w_gu, w_gu, w_down, gf)


def _rms_kernel(x_ref, g_ref, o_ref):
    o_ref[...] = _rms_rows(x_ref[...], g_ref[...])


def _rms_call(x, g):
    return pl.pallas_call(
        _rms_kernel,
        out_shape=jax.ShapeDtypeStruct(x.shape, F32),
        name="rmsnorm_rows",
    )(x, g)


SRC_GQ, SRC_GK, SRC_GV, SRC_GLO, SRC_GATE_A, SRC_GATE_B = 3200, 3712, 4224, 5248, 5264, 6288
WCAT_ROWS = 256


def _wcat_kernel(w_ref, v1_ref, o_ref):
    def put(dst, src, width):
        o_ref[:, dst:dst + width] = w_ref[:, src:src + width].astype(BF16)

    def zero(lo, hi):
        o_ref[:, lo:hi] = jnp.zeros((o_ref.shape[0], hi - lo), BF16)

    put(COL_R, 0, COL_LO + 2 * LORA_DECAY)
    o_ref[:, COL_VLO:COL_VLO + LORA_VRES] = v1_ref[...].astype(BF16)
    zero(COL_VLO + LORA_VRES, COL_GLO)
    put(COL_GLO, SRC_GLO, GLA_GATE_RANK)
    zero(COL_GLO + GLA_GATE_RANK, COL_GQ)
    put(COL_GQ, SRC_GQ, GLA_HEADS * GLA_DK)
    put(COL_GATE_A, SRC_GATE_A, D_MODEL)
    put(COL_GATE_B, SRC_GATE_B, D_MODEL)
    put(COL_GV, SRC_GV, GLA_HEADS * GLA_DV)
    put(COL_GK, SRC_GK, GLA_HEADS * GLA_DK)


def _wcat_call(w_in, vres_w1_padded):
    depth, _, n_in = w_in.shape
    return pl.pallas_call(
        _wcat_kernel,
        grid=(depth, D_MODEL // WCAT_ROWS),
        in_specs=[pl.BlockSpec((None, WCAT_ROWS, n_in), lambda l, i: (l, i, 0)),
                  pl.BlockSpec((None, WCAT_ROWS, LORA_VRES), lambda l, i: (l, i, 0))],
        out_specs=pl.BlockSpec((None, WCAT_ROWS, N_PROJ), lambda l, i: (l, i, 0)),
        out_shape=jax.ShapeDtypeStruct((depth, D_MODEL, N_PROJ), BF16),
        compiler_params=_cparams(("parallel", "parallel")),
        name="w_in_reorder",
    )(w_in, vres_w1_padded)


def _pad_rows(x, n):
    return jnp.pad(x, ((0, n - x.shape[0]), (0, 0)))


def _layer_weights(l, mu_shift, vres_mu, p):
    mu = mu_shift[l]
    zeros_mu = lambda n: jnp.zeros((n,), F32)
    vlo_mu = jnp.pad(vres_mu[l - 1], (0, LANES - LORA_VRES)) if l > 0 else zeros_mu(LANES)
    mu_cat = jnp.concatenate([mu[:COL_VLO], vlo_mu, zeros_mu(N_PROJ - N_SHIFTED)])[None, :]
    lw = dict(mu_cat=mu_cat)
    lw['w2a'] = jnp.concatenate([p['rwkv_w2'][l], p['rwkv_a2'][l]], axis=0)
    lw['vecs'] = jnp.stack([p['rwkv_w0'][l], p['rwkv_a0'][l], p['rwkv_k_k'][l], p['rwkv_k_a'][l],
                            p['rwkv_r_k'][l].reshape(-1), p['rwkv_lnx_g'][l], p['rwkv_lnx_b'][l],
                            p['vres_v0'][l - 1] if l > 0 else zeros_mu(D_MODEL)], axis=0)
    lw['vw2'] = _pad_rows(p['vres_w2'][l - 1], LANES) if l > 0 else None
    lw['ga2'] = _pad_rows(p['gla_a2'][l], LANES)
    lw['gab'] = p['gla_ab'][l][None, :]
    lw['gla_g'] = p['gla_norm_g'][l][None, :]
    lw['w_o'] = p['w_o'][l].astype(BF16)
    lw['w_gu'] = p['ffn_w_gu'][l].astype(BF16)
    lw['w_down'] = p['ffn_w_down'][l].astype(BF16)
    lw['g_mix'] = p['norm_mix'][l][None, :]
    lw['g_ffn'] = p['norm_ffn'][l][None, :]
    return lw


def _rows_per_step(batch, want):
    while batch % want:
        want //= 2
    return want


def _run_group(x, shift_state, wkv_state, gla_state, w_cat, weights, g_final):
    B, T, _ = x.shape
    fresh = shift_state is None
    C = CHUNK if fresh else SAMPLE_CHUNK
    Tp = -(-T // C) * C
    BB = _rows_per_step(B, ROWS_PER_STEP)
    h = x.reshape(B * T, D_MODEL)
    new_shift, new_wkv, new_gla = [], [], []
    P_first = None
    depth = len(weights)
    for l, lw in enumerate(weights):
        new_shift.append(_rms_call(h.reshape(B, T, D_MODEL)[:, -1], lw['g_mix']))
        P = _proj_call(h, lw['g_mix'], w_cat, l, lw['mu_cat'], None if fresh else shift_state[l], T)
        P = jnp.pad(P.reshape(B, T, N_PROJ), ((0, 0), (0, Tp - T), (0, 0)))
        h3 = jnp.pad(h.reshape(B, T, D_MODEL), ((0, 0), (0, Tp - T), (0, 0)))
        h3, s_wkv, s_gla = _mixer_call(P, P_first if l > 0 else None, h3, wkv_state, gla_state, l, lw,
                                       C=C, n_valid=min(T, C), BB=BB)
        if l == 0:
            P_first = P
        if fresh:
            new_wkv.append(s_wkv)
            new_gla.append(s_gla)
        else:
            wkv_state, gla_state = s_wkv, s_gla
        h = h3[:, :T].reshape(B * T, D_MODEL)
        h = _ffn_call(h, lw['g_ffn'], lw['w_gu'], lw['w_down'], g_final, final_norm=(l == depth - 1))
    if fresh:
        wkv_state, gla_state = jnp.stack(new_wkv), jnp.stack(new_gla)
    return (h.reshape(B, T, D_MODEL), jnp.stack(new_shift), wkv_state, gla_state)


def kernel(x_prompt, x_sample, state_shift, state_wkv, state_gla, norm_mix, w_in, mu_shift, rwkv_w0, rwkv_w2, rwkv_a0, rwkv_a2, rwkv_k_k, rwkv_k_a, rwkv_r_k, rwkv_lnx_g, rwkv_lnx_b, vres_w1, vres_mu, vres_w2, vres_v0, gla_a2, gla_ab, gla_norm_g, w_o, norm_ffn, ffn_w_gu, ffn_w_down, norm_final):
    p = dict(norm_mix=norm_mix, rwkv_w0=rwkv_w0, rwkv_w2=rwkv_w2, rwkv_a0=rwkv_a0, rwkv_a2=rwkv_a2,
             rwkv_k_k=rwkv_k_k, rwkv_k_a=rwkv_k_a, rwkv_r_k=rwkv_r_k, rwkv_lnx_g=rwkv_lnx_g,
             rwkv_lnx_b=rwkv_lnx_b, vres_w2=vres_w2, vres_v0=vres_v0, gla_a2=gla_a2, gla_ab=gla_ab,
             gla_norm_g=gla_norm_g, w_o=w_o, norm_ffn=norm_ffn, ffn_w_gu=ffn_w_gu, ffn_w_down=ffn_w_down)
    depth = w_in.shape[0]
    weights = [_layer_weights(l, mu_shift, vres_mu, p) for l in range(depth)]
    w_cat = _wcat_call(w_in, jnp.concatenate([jnp.zeros_like(vres_w1[:1]), vres_w1], axis=0))
    g_final = norm_final[None, :]
    y_p, shift_p, wkv_p, gla_p = _run_group(x_prompt, None, None, None, w_cat, weights, g_final)
    y_s, shift_s, wkv_s, gla_s = _run_group(x_sample, state_shift, state_wkv, state_gla, w_cat, weights,
                                            g_final)
    return (y_p, y_s, shift_p, wkv_p, gla_p, shift_s, wkv_s, gla_s)
```

```python
import functools

import jax
import jax.numpy as jnp
from jax import lax
from jax.experimental import pallas as pl
from jax.experimental.pallas import tpu as pltpu

F32 = jnp.float32
BF16 = jnp.bfloat16

LANES = 128
D_MODEL = 1024
RWKV_HEAD_DIM = 64
RWKV_HEADS = D_MODEL // RWKV_HEAD_DIM
HEADS_PER_PAIR = LANES // RWKV_HEAD_DIM
N_PAIRS = RWKV_HEADS // HEADS_PER_PAIR
LORA_DECAY = 64
LORA_ICLR = 64
LORA_VRES = 32
RWKV_GN_EPS = 64e-5
GLA_HEADS = 4
GLA_DK = 128
GLA_DV = 256
GLA_GATE_RANK = 16
GLA_GATE_NORMALIZER = 16.0
GLA_NORM_EPS = 1e-5
FFN_HIDDEN = 2816
NORM_EPS = 1e-6
CHUNK = 64
SAMPLE_CHUNK = 8

COL_R, COL_K, COL_V = 0, 1024, 2048
COL_LO, COL_VLO, COL_GLO = 3072, 3200, 3328
COL_GQ = 3584
COL_GATE_A, COL_GATE_B = 4096, 5120
COL_GV, COL_GK = 6144, 7168
N_SHIFTED = 3328
N_PROJ = 7680
PROJ_TM = 512
PROJ_SUB = 512
PROMPT_PROJ_DTYPE = BF16
VMEM_LIMIT = 48 * 1024 * 1024
PROMPT_ROWS_PER_STEP = 4
SAMPLE_ROWS_PER_STEP = 2
DECAY_SCALE = 0.6065306597126334


def _cparams(sem):
    return pltpu.CompilerParams(dimension_semantics=sem, vmem_limit_bytes=VMEM_LIMIT)


def _mm(a, b):
    return jnp.dot(a, b, preferred_element_type=F32)


def _mm_nt(a, b):
    return lax.dot_general(a, b, (((1,), (1,)), ((), ())), preferred_element_type=F32)


def _mm_tn(a, b):
    return lax.dot_general(a, b, (((0,), (0,)), ((), ())), preferred_element_type=F32)


def _sigmoid(x):
    return 0.5 * jnp.tanh(0.5 * x) + 0.5


def _softplus(x):
    return jnp.maximum(x, 0.0) + jnp.log(1.0 + jnp.exp(-jnp.abs(x)))


def _cumsum_rows(x):
    n = x.shape[0]
    row = lax.broadcasted_iota(jnp.int32, x.shape, 0)
    s = 1
    while s < n:
        x = x + jnp.where(row >= s, pltpu.roll(x, s, 0), 0.0)
        s *= 2
    return x


def _rms_rows(x, g):
    ms = jnp.mean(x * x, axis=-1, keepdims=True)
    return x * lax.rsqrt(ms + NORM_EPS) * g


def _proj_kernel(*refs, sample, tiles_per_seq, tm):
    if sample:
        x_ref, g_ref, w_ref, mu_ref, prev_ref, o_ref = refs
    else:
        x_ref, g_ref, w_ref, mu_ref, o_ref, last_ref = refs
    xs = _rms_rows(x_ref[...], g_ref[...]).astype(BF16)
    subs = [slice(n * PROJ_SUB, (n + 1) * PROJ_SUB) for n in range(N_PROJ // PROJ_SUB)]
    n_shift = -(-N_SHIFTED // PROJ_SUB)
    if sample:
        ps = prev_ref[...].astype(BF16)
        for n, cs in enumerate(subs):
            y = _mm(xs, w_ref[:, cs])
            if n < n_shift:
                y = y + mu_ref[:, cs] * (_mm(ps, w_ref[:, cs]) - y)
            o_ref[:, cs] = y.astype(o_ref.dtype)
        return

    i = pl.program_id(0)

    @pl.when(i == 0)
    def _():
        last_ref[...] = jnp.zeros_like(last_ref)

    row = lax.broadcasted_iota(jnp.int32, (tm, PROJ_SUB), 0)
    seq_start = i % tiles_per_seq == 0
    for n, cs in enumerate(subs):
        y = _mm(xs, w_ref[:, cs])
        if n < n_shift:
            prev_row = jnp.where(seq_start, 0.0, last_ref[7:8, cs])
            shifted = jnp.where(row == 0, prev_row, pltpu.roll(y, 1, 0))
            last_ref[:, cs] = y[tm - 8:, :]
            y = y + mu_ref[:, cs] * (shifted - y)
        o_ref[:, cs] = y.astype(o_ref.dtype)


def _proj_call(x, g, w, layer, mu, prev, seq_len):
    m = x.shape[0]
    sample = prev is not None
    tm = min(PROJ_TM, m if sample else seq_len)
    row = lambda i: (i, 0)
    in_specs = [
        pl.BlockSpec((tm, D_MODEL), row),
        pl.BlockSpec((1, D_MODEL), lambda i: (0, 0)),
        pl.BlockSpec((None, D_MODEL, N_PROJ), lambda i: (layer, 0, 0), pipeline_mode=pl.Buffered(1)),
        pl.BlockSpec((1, N_PROJ), lambda i: (0, 0)),
    ]
    args = [x, g, w, mu]
    scratch = []
    if sample:
        in_specs.append(pl.BlockSpec((tm, D_MODEL), row))
        args.append(prev)
    else:
        scratch.append(pltpu.VMEM((8, -(-N_SHIFTED // PROJ_SUB) * PROJ_SUB), F32))
    return pl.pallas_call(
        functools.partial(_proj_kernel, sample=sample, tiles_per_seq=max(seq_len // tm, 1), tm=tm),
        grid=(m // tm,),
        in_specs=in_specs,
        out_specs=pl.BlockSpec((tm, N_PROJ), row),
        out_shape=jax.ShapeDtypeStruct((m, N_PROJ), F32 if sample else PROMPT_PROJ_DTYPE),
        scratch_shapes=scratch,
        compiler_params=_cparams(("arbitrary",)),
        name="proj_sample" if sample else "proj_prompt",
    )(*args)


class _Shared:
    pass


def _rwkv_row_stages(bi, R, res, *, C, n_valid, has_vres):
    H = RWKV_HEAD_DIM
    head0, trow = R.head0, R.trow
    sls = [slice(p * LANES, (p + 1) * LANES) for p in range(N_PAIRS)]
    scr = [bi * N_PAIRS + p for p in range(N_PAIRS)]
    n_sq = C.bit_length() - 2
    fused = 2 * C == LANES
    st = {}
    bf = lambda x: x.astype(BF16)
    cat0 = lambda x, y: jnp.concatenate([x, y], axis=0)
    cat1 = lambda x, y: jnp.concatenate([x, y], axis=1)

    def each(f, *lists):
        return [f(*xs) for xs in zip(*lists)]

    def stack(x):
        return jnp.concatenate([jnp.where(head0, x, 0.0), jnp.where(head0, 0.0, x)], axis=0)

    def head_sum(x):
        s0 = jnp.sum(jnp.where(head0, x, 0.0), axis=-1, keepdims=True)
        s1 = jnp.sum(jnp.where(head0, 0.0, x), axis=-1, keepdims=True)
        return jnp.where(head0, s0, s1)

    def lora():
        lo = R.lo_ref[bi].astype(F32)
        w2a = R.w2a_ref[...]
        st['wl'] = _mm(jnp.where(head0, jnp.tanh(lo), 0.0), w2a)
        st['al'] = _mm(jnp.where(head0, 0.0, lo), w2a)
        if has_vres:
            st['vg'] = _mm(R.vlo_ref[bi].astype(F32), R.vw2_ref[...])

    def operands():
        vec = lambda row: [R.vec_ref[row:row + 1, sl] for sl in sls]
        w0, a0, k_k, k_a, r_k, lnx_g, lnx_b, v0 = (vec(i) for i in range(8))
        r = [R.r_ref[bi, :, sl].astype(F32) for sl in sls]
        k = [R.k_ref[bi, :, sl].astype(F32) for sl in sls]
        v = [R.v_ref[bi, :, sl].astype(F32) for sl in sls]
        if has_vres:
            v = [vp + (R.vf_ref[bi, :, sl].astype(F32) - vp) * _sigmoid(z + st['vg'][:, sl])
                 for vp, sl, z in zip(v, sls, v0)]

        def decay_log(w0p, sl):
            ld = -DECAY_SCALE * _sigmoid(w0p + st['wl'][:, sl])
            if n_valid < C:
                ld = jnp.where(trow < n_valid, ld, 0.0)
            return ld

        ld = each(decay_log, w0, sls)
        a = each(lambda z, sl: _sigmoid(z + st['al'][:, sl]), a0, sls)

        def unit_kk(kp, kkp):
            kk = kp * kkp
            return kk * lax.rsqrt(jnp.maximum(head_sum(kk * kk), 1e-24))

        kk = each(unit_kk, k, k_k)
        k = each(lambda kp, ap, kap: kp * (1.0 + (ap - 1.0) * kap), k, a, k_a)
        b = each(_cumsum_rows, ld)
        e_b = each(jnp.exp, b)
        e_nb = each(lambda x: jnp.exp(-x), b)
        st['al_s'] = each(lambda kkp, bp, ldp: bf(stack(-kkp * jnp.exp(bp - ldp))), kk, b, ld)
        st['be'] = each(lambda ap, kkp, e: bf(stack(ap * kkp * e)), a, kk, e_nb)
        st['kt'] = each(lambda kp, e: bf(stack(kp * e)), k, e_nb)
        st['rt'] = each(lambda rp, e: bf(stack(rp * e)), r, e_b)
        st['vs'] = each(lambda x: bf(stack(x)), v)
        st['decay'] = [e[C - 1:C, :] for e in e_b]
        st['S32'] = [R.s_scr[i] for i in scr]
        st['epi'] = (r, k, v, r_k, lnx_g, lnx_b)

    def finish(o2, s_new):
        for i, s in zip(scr, s_new):
            R.s_scr[i] = s
        r, k, v, r_k, lnx_g, lnx_b = st['epi']
        outs = []
        for p in range(N_PAIRS):
            o = o2[p][:C] + o2[p][C:]
            mean = head_sum(o) * (1.0 / H)
            d = o - mean
            var = head_sum(d * d) * (1.0 / H)
            o = d * lax.rsqrt(var + RWKV_GN_EPS) * lnx_g[p] + lnx_b[p]
            outs.append(o + head_sum(r[p] * k[p] * r_k[p]) * v[p])
        res[bi] = outs

    if fused:
        def amat():
            operands()
            st['ar'] = each(cat0, st['al_s'], st['rt'])
            st['bk'] = each(cat0, st['be'], st['kt'])
            st['amat'] = each(lambda x, y: jnp.where(R.mask4, _mm_nt(x, y), 0.0), st['ar'], st['bk'])

        def xq():
            sv = each(lambda s, y: cat0(bf(s.T), y), st['S32'], st['vs'])
            st['x'] = each(lambda l, m, rhs: _mm(cat1(l, bf(m[:, LANES:])), rhs), st['ar'], st['amat'], sv)
            st['tinv'] = each(lambda m: R.eye + m[:LANES, :LANES], st['amat'])
            q = each(lambda m: bf(m[:LANES, :LANES]), st['amat'])
            st['q'] = each(lambda z: bf(_mm(z, z)), q)

        def double():
            y = each(lambda z, t: _mm(z, cat1(bf(t), z)), st['q'], st['tinv'])
            st['tinv'] = each(lambda t, z: t + z[:, :LANES], st['tinv'], y)
            st['q'] = each(lambda z: bf(z[:, LANES:]), y)

        def double_last():
            st['tinv'] = each(lambda t, z: t + _mm(z, bf(t)), st['tinv'], st['q'])

        def solve():
            st['cm'] = each(lambda t, z: bf(_mm(bf(t), bf(z[:LANES]))), st['tinv'], st['x'])

        def out_state():
            o2 = each(lambda z, m, cc: z[LANES:] + _mm(bf(m[LANES:, :LANES]), cc),
                      st['x'], st['amat'], st['cm'])
            s_new = each(lambda s, cc, y, rhs, e: (s + _mm_tn(cat0(cc, y), rhs)) * e,
                         st['S32'], st['cm'], st['vs'], st['bk'], st['decay'])
            finish(o2, s_new)

        return [lora, amat, xq] + [double] * (n_sq - 1) + [double_last, solve, out_state]

    def small_a():
        operands()
        S = each(bf, st['S32'])
        st['S'] = S
        a_ak = each(lambda x, y: bf(jnp.where(R.strict, _mm_nt(x, y), 0.0)), st['al_s'], st['kt'])
        st['a_rb'] = each(lambda x, y: bf(jnp.where(R.incl, _mm_nt(x, y), 0.0)), st['rt'], st['be'])
        st['a_rk'] = each(lambda x, y: bf(jnp.where(R.incl, _mm_nt(x, y), 0.0)), st['rt'], st['kt'])
        st['cm'] = each(lambda x, s, m, y: _mm_nt(x, s) + _mm(m, y), st['al_s'], S, a_ak, st['vs'])

    def small_solve():
        pw = each(lambda x, y: jnp.where(R.strict, _mm_nt(x, y), 0.0), st['al_s'], st['be'])
        tinv = each(lambda x: R.eye + x, pw)
        pwb = each(bf, pw)
        for _ in range(n_sq):
            pwb = each(lambda x: bf(_mm(x, x)), pwb)
            tinv = each(lambda t, x: t + _mm(bf(t), x), tinv, pwb)
        st['cm'] = each(lambda t, x: _mm(bf(t), bf(x)), tinv, st['cm'])

    def small_out():
        cm = each(bf, st['cm'])
        o2 = each(lambda x, s, m1, c1, m2, y: _mm_nt(x, s) + _mm(m1, c1) + _mm(m2, y),
                  st['rt'], st['S'], st['a_rb'], cm, st['a_rk'], st['vs'])
        s_new = each(lambda s, c1, x, y, z, e: (s + _mm_tn(c1, x) + _mm_tn(y, z)) * e,
                     st['S32'], cm, st['be'], st['vs'], st['kt'], st['decay'])
        finish(o2, s_new)

    return [lora, small_a] + ([small_solve] if n_valid > 1 else []) + [small_out]


def _gla_row_stages(bi, R, res, *, C, n_valid):
    heads = range(GLA_HEADS)
    ks = [slice(h * GLA_DK, (h + 1) * GLA_DK) for h in heads]
    vsl = [slice(h * GLA_DV, (h + 1) * GLA_DV) for h in heads]
    scr = [bi * GLA_HEADS + h for h in heads]
    st = {}

    def gate():
        st['gate'] = _mm(R.glo_ref[bi].astype(F32), R.ga2_ref[...])

    def scores():
        def cum_log_decay(s):
            la = -_softplus(-(st['gate'][:, s] + R.gab_ref[:, s])) * (1.0 / GLA_GATE_NORMALIZER)
            if n_valid < C:
                la = jnp.where(R.trow < n_valid, la, 0.0)
            return _cumsum_rows(la)

        b = [cum_log_decay(s) for s in ks]
        k = [R.gk_ref[bi, :, s].astype(F32) for s in ks]
        st['v'] = [R.gv_ref[bi, :, s].astype(F32) for s in vsl]
        qd = [R.gq_ref[bi, :, s].astype(F32) * (GLA_DK ** -0.5) * jnp.exp(bh) for s, bh in zip(ks, b)]
        kd = [kh * jnp.exp(-bh) for kh, bh in zip(k, b)]
        kl = [kh * jnp.exp(bh[C - 1:C, :] - bh) for kh, bh in zip(k, b)]
        St = [R.g_scr[i] for i in scr]
        st['att'] = [jnp.where(R.causal, _mm_nt(x, y), 0.0) for x, y in zip(qd, kd)]
        st['o_s'] = [_mm_nt(x, s) for x, s in zip(qd, St)]
        upd = [_mm_tn(x, y) for x, y in zip(st['v'], kl)]
        for i, s_old, bh, u in zip(scr, St, b, upd):
            R.g_scr[i] = s_old * jnp.exp(bh[C - 1:C, :]) + u

    def outputs():
        o = [_mm(x, y) + z for x, y, z in zip(st['att'], st['v'], st['o_s'])]
        res[bi] = [oh * lax.rsqrt(jnp.mean(oh * oh, axis=-1, keepdims=True) + GLA_NORM_EPS) * R.gg_ref[...]
                   for oh in o]

    return [gate, scores, outputs]


def _mixer_kernel(*refs, C, BB, n_valid, has_s0, has_vres, layer, copy_other_layers):
    R = _Shared()
    it = iter(refs)
    R.r_ref, R.k_ref, R.v_ref, R.lo_ref = next(it), next(it), next(it), next(it)
    if has_vres:
        R.vlo_ref, R.vf_ref = next(it), next(it)
    R.gq_ref, R.gk_ref, R.gv_ref, R.glo_ref = next(it), next(it), next(it), next(it)
    ga_ref, gb_ref, h_ref = next(it), next(it), next(it)
    if has_s0:
        s0w_ref, s0g_ref = next(it), next(it)
    R.w2a_ref = next(it)
    if has_vres:
        R.vw2_ref = next(it)
    R.vec_ref, R.ga2_ref, R.gab_ref, R.gg_ref, wo_ref = next(it), next(it), next(it), next(it), next(it)
    o_ref, sow_ref, sog_ref, R.s_scr, R.g_scr = next(it), next(it), next(it), next(it), next(it)

    c = pl.program_id(1)
    H = RWKV_HEAD_DIM
    rows = range(BB)
    if has_s0 and copy_other_layers:
        s0w_all, s0g_all, sow_all, sog_all = s0w_ref, s0g_ref, sow_ref, sog_ref
        s0w_ref, s0g_ref = s0w_all.at[layer], s0g_all.at[layer]
        sow_ref, sog_ref = sow_all.at[layer], sog_all.at[layer]

    @pl.when(c == 0)
    def _():
        R.s_scr[...] = jnp.zeros_like(R.s_scr)
        if has_s0:
            for bi in rows:
                for p in range(N_PAIRS):
                    R.s_scr[bi * N_PAIRS + p, 0:H, 0:H] = s0w_ref[bi, 2 * p]
                    R.s_scr[bi * N_PAIRS + p, H:2 * H, H:2 * H] = s0w_ref[bi, 2 * p + 1]
                for h in range(GLA_HEADS):
                    R.g_scr[bi * GLA_HEADS + h] = s0g_ref[bi, h].T
        else:
            R.g_scr[...] = jnp.zeros_like(R.g_scr)

    R.head0 = lax.broadcasted_iota(jnp.int32, (C, LANES), 1) < H
    R.trow = lax.broadcasted_iota(jnp.int32, (C, LANES), 0)
    ri = lax.broadcasted_iota(jnp.int32, (C, C), 0)
    ci = lax.broadcasted_iota(jnp.int32, (C, C), 1)
    R.causal = ri >= ci
    if 2 * C == LANES:
        ri = lax.broadcasted_iota(jnp.int32, (2 * LANES, 2 * LANES), 0)
        ci = lax.broadcasted_iota(jnp.int32, (2 * LANES, 2 * LANES), 1)
        rr, cc = ri % LANES, ci % LANES
        R.mask4 = ((rr // C) == (cc // C)) & ((rr % C) >= (cc % C) + (ri < LANES).astype(jnp.int32))
        r1 = lax.broadcasted_iota(jnp.int32, (LANES, LANES), 0)
        c1 = lax.broadcasted_iota(jnp.int32, (LANES, LANES), 1)
        R.eye = (r1 == c1).astype(F32)
    else:
        ri = lax.broadcasted_iota(jnp.int32, (2 * C, 2 * C), 0)
        ci = lax.broadcasted_iota(jnp.int32, (2 * C, 2 * C), 1)
        same_head = (ri // C) == (ci // C)
        R.strict = same_head & ((ri % C) > (ci % C))
        R.incl = same_head & ((ri % C) >= (ci % C))
        R.eye = (ri == ci).astype(F32)

    ro, go = {}, {}
    plans = []
    for bi in rows:
        plans.append(_rwkv_row_stages(bi, R, ro, C=C, n_valid=n_valid, has_vres=has_vres))
        plans.append(_gla_row_stages(bi, R, go, C=C, n_valid=n_valid))
    for d in range(max(len(stages) for stages in plans)):
        for stages in plans:
            if d < len(stages):
                stages[d]()

    merged = [_sigmoid(ga_ref[bi].astype(F32)) * jnp.concatenate(ro[bi], axis=1)
              + _sigmoid(gb_ref[bi].astype(F32)) * jnp.concatenate(go[bi], axis=1) for bi in rows]
    mix = _mm(jnp.concatenate(merged, axis=0).astype(BF16), wo_ref[...])
    for bi in rows:
        o_ref[bi] = h_ref[bi] + mix[bi * C:(bi + 1) * C]

    @pl.when(c == pl.num_programs(1) - 1)
    def _():
        for bi in rows:
            for p in range(N_PAIRS):
                sow_ref[bi, 2 * p] = R.s_scr[bi * N_PAIRS + p, 0:H, 0:H]
                sow_ref[bi, 2 * p + 1] = R.s_scr[bi * N_PAIRS + p, H:2 * H, H:2 * H]
            for h in range(GLA_HEADS):
                sog_ref[bi, h] = R.g_scr[bi * GLA_HEADS + h].T
        if has_s0 and copy_other_layers:
            for other in range(s0w_all.shape[0]):
                if other != layer:
                    sow_all[other] = s0w_all[other]
                    sog_all[other] = s0g_all[other]


def _mixer_call(P, Pfirst, h, s_wkv, s_gla, layer, lw, w_o, *, C, n_valid, BB):
    B, T, _ = P.shape
    has_vres = Pfirst is not None
    has_s0 = s_wkv is not None
    kw = GLA_HEADS * GLA_DK

    def cols(off, width):
        idx = off // width
        return pl.BlockSpec((BB, C, width), lambda b, c: (b, c, idx))

    whole = lambda shape: pl.BlockSpec(shape, lambda b, c: (0,) * len(shape))
    wkv_block = (BB, RWKV_HEADS, RWKV_HEAD_DIM, RWKV_HEAD_DIM)
    gla_block = (BB, GLA_HEADS, GLA_DK, GLA_DV)
    copy_other_layers = has_s0 and layer == 0
    if copy_other_layers:
        depth = s_wkv.shape[0]
        state_spec = lambda blk: pl.BlockSpec((depth,) + blk, lambda b, c: (0, b, 0, 0, 0))
        wkv_shape, gla_shape = s_wkv.shape, s_gla.shape
    elif has_s0:
        state_spec = lambda blk: pl.BlockSpec((None,) + blk, lambda b, c: (layer, b, 0, 0, 0))
        wkv_shape, gla_shape = s_wkv.shape, s_gla.shape
    else:
        state_spec = lambda blk: pl.BlockSpec(blk, lambda b, c: (b, 0, 0, 0))
        wkv_shape, gla_shape = (B,) + wkv_block[1:], (B,) + gla_block[1:]

    in_specs = [cols(COL_R, D_MODEL), cols(COL_K, D_MODEL), cols(COL_V, D_MODEL), cols(COL_LO, LANES)]
    args = [P, P, P, P]
    if has_vres:
        in_specs += [cols(COL_VLO, LANES), cols(COL_V, D_MODEL)]
        args += [P, Pfirst]
    in_specs += [cols(COL_GQ, kw), cols(COL_GK, kw), cols(COL_GV, D_MODEL), cols(COL_GLO, LANES),
                 cols(COL_GATE_A, D_MODEL), cols(COL_GATE_B, D_MODEL), cols(0, D_MODEL)]
    args += [P, P, P, P, P, P, h]
    aliases = {}
    if has_s0:
        if not copy_other_layers:
            aliases = {len(args): 1, len(args) + 1: 2}
        in_specs += [state_spec(wkv_block), state_spec(gla_block)]
        args += [s_wkv, s_gla]
    in_specs.append(whole((LANES, D_MODEL)))
    args.append(lw['w2a'])
    if has_vres:
        in_specs.append(whole((LANES, D_MODEL)))
        args.append(lw['vw2'])
    in_specs += [whole((8, D_MODEL)), whole((LANES, kw)), whole((1, kw)), whole((1, GLA_DV)),
                 pl.BlockSpec((None, D_MODEL, D_MODEL), lambda b, c: (layer, 0, 0))]
    args += [lw['vecs'], lw['ga2'], lw['gab'], lw['gla_g'], w_o]
    return pl.pallas_call(
        functools.partial(_mixer_kernel, C=C, BB=BB, n_valid=n_valid, has_s0=has_s0, has_vres=has_vres,
                          layer=layer, copy_other_layers=copy_other_layers),
        grid=(B // BB, T // C),
        in_specs=in_specs,
        out_specs=[pl.BlockSpec((BB, C, D_MODEL), lambda b, c: (b, c, 0)),
                   state_spec(wkv_block), state_spec(gla_block)],
        out_shape=[jax.ShapeDtypeStruct((B, T, D_MODEL), F32),
                   jax.ShapeDtypeStruct(wkv_shape, F32),
                   jax.ShapeDtypeStruct(gla_shape, F32)],
        input_output_aliases=aliases,
        scratch_shapes=[pltpu.VMEM((BB * N_PAIRS, LANES, LANES), F32),
                        pltpu.VMEM((BB * GLA_HEADS, GLA_DV, GLA_DK), F32)],
        compiler_params=_cparams(("parallel", "arbitrary")),
        name="mixer",
    )(*args)


FFN_SUB = 256
FFN_TM = 1024


def _ffn_kernel(h_ref, g_ref, wg_ref, wu_ref, wd_ref, gf_ref, o_ref, *, final_norm):
    h = h_ref[...]
    xs = _rms_rows(h, g_ref[...]).astype(BF16)
    out = h
    for n in range(FFN_HIDDEN // FFN_SUB):
        cs = slice(n * FFN_SUB, (n + 1) * FFN_SUB)
        gate = _mm(xs, wg_ref[:, cs])
        up = _mm(xs, wu_ref[:, cs])
        act = (gate * _sigmoid(gate) * up).astype(BF16)
        out = out + _mm(act, wd_ref[cs, :])
    if final_norm:
        out = _rms_rows(out, gf_ref[...])
    o_ref[...] = out


def _ffn_call(h, g, w_gu, w_down, layer, gf, final_norm):
    m = h.shape[0]
    tm = min(FFN_TM, m)
    row = lambda i: (i, 0)
    resident = pl.Buffered(1)
    return pl.pallas_call(
        functools.partial(_ffn_kernel, final_norm=final_norm),
        grid=(m // tm,),
        in_specs=[pl.BlockSpec((tm, D_MODEL), row),
                  pl.BlockSpec((1, D_MODEL), lambda i: (0, 0)),
                  pl.BlockSpec((None, D_MODEL, FFN_HIDDEN), lambda i: (layer, 0, 0), pipeline_mode=resident),
                  pl.BlockSpec((None, D_MODEL, FFN_HIDDEN), lambda i: (layer, 0, 1), pipeline_mode=resident),
                  pl.BlockSpec((None, FFN_HIDDEN, D_MODEL), lambda i: (layer, 0, 0), pipeline_mode=resident),
                  pl.BlockSpec((1, D_MODEL), lambda i: (0, 0))],
        out_specs=pl.BlockSpec((tm, D_MODEL), row),
        out_shape=jax.ShapeDtypeStruct((m, D_MODEL), F32),
        compiler_params=_cparams(("parallel",)),
        name="ffn",
    )(h, g, w_gu, w_gu, w_down, gf)


def _rms_kernel(x_ref, g_ref, o_ref):
    o_ref[...] = _rms_rows(x_ref[...], g_ref[...])


def _rms_call(x, g):
    return pl.pallas_call(
        _rms_kernel,
        out_shape=jax.ShapeDtypeStruct(x.shape, F32),
        name="rmsnorm_rows",
    )(x, g)


SRC_GQ, SRC_GK, SRC_GV, SRC_GLO, SRC_GATE_A, SRC_GATE_B = 3200, 3712, 4224, 5248, 5264, 6288
WCAT_ROWS = 256


def _wcat_kernel(w_ref, v1_ref, o_ref):
    def put(dst, src, width):
        o_ref[:, dst:dst + width] = w_ref[:, src:src + width].astype(BF16)

    def zero(lo, hi):
        o_ref[:, lo:hi] = jnp.zeros((o_ref.shape[0], hi - lo), BF16)

    put(COL_R, 0, COL_LO + 2 * LORA_DECAY)
    o_ref[:, COL_VLO:COL_VLO + LORA_VRES] = v1_ref[...].astype(BF16)
    zero(COL_VLO + LORA_VRES, COL_GLO)
    put(COL_GLO, SRC_GLO, GLA_GATE_RANK)
    zero(COL_GLO + GLA_GATE_RANK, COL_GQ)
    put(COL_GQ, SRC_GQ, GLA_HEADS * GLA_DK)
    put(COL_GATE_A, SRC_GATE_A, D_MODEL)
    put(COL_GATE_B, SRC_GATE_B, D_MODEL)
    put(COL_GV, SRC_GV, GLA_HEADS * GLA_DV)
    put(COL_GK, SRC_GK, GLA_HEADS * GLA_DK)


def _wcat_call(w_in, vres_w1_padded):
    depth, _, n_in = w_in.shape
    return pl.pallas_call(
        _wcat_kernel,
        grid=(depth, D_MODEL // WCAT_ROWS),
        in_specs=[pl.BlockSpec((None, WCAT_ROWS, n_in), lambda l, i: (l, i, 0)),
                  pl.BlockSpec((None, WCAT_ROWS, LORA_VRES), lambda l, i: (l, i, 0))],
        out_specs=pl.BlockSpec((None, WCAT_ROWS, N_PROJ), lambda l, i: (l, i, 0)),
        out_shape=jax.ShapeDtypeStruct((depth, D_MODEL, N_PROJ), BF16),
        compiler_params=_cparams(("parallel", "parallel")),
        name="w_in_reorder",
    )(w_in, vres_w1_padded)


def _pad_rows(x, n):
    return jnp.pad(x, ((0, n - x.shape[0]), (0, 0)))


def _layer_weights(l, mu_shift, vres_mu, p):
    mu = mu_shift[l]
    zeros_mu = lambda n: jnp.zeros((n,), F32)
    vlo_mu = jnp.pad(vres_mu[l - 1], (0, LANES - LORA_VRES)) if l > 0 else zeros_mu(LANES)
    mu_cat = jnp.concatenate([mu[:COL_VLO], vlo_mu, zeros_mu(N_PROJ - N_SHIFTED)])[None, :]
    lw = dict(mu_cat=mu_cat)
    lw['w2a'] = jnp.concatenate([p['rwkv_w2'][l], p['rwkv_a2'][l]], axis=0)
    lw['vecs'] = jnp.stack([p['rwkv_w0'][l], p['rwkv_a0'][l], p['rwkv_k_k'][l], p['rwkv_k_a'][l],
                            p['rwkv_r_k'][l].reshape(-1), p['rwkv_lnx_g'][l], p['rwkv_lnx_b'][l],
                            p['vres_v0'][l - 1] if l > 0 else zeros_mu(D_MODEL)], axis=0)
    lw['vw2'] = _pad_rows(p['vres_w2'][l - 1], LANES) if l > 0 else None
    lw['ga2'] = _pad_rows(p['gla_a2'][l], LANES)
    lw['gab'] = p['gla_ab'][l][None, :]
    lw['gla_g'] = p['gla_norm_g'][l][None, :]
    lw['g_mix'] = p['norm_mix'][l][None, :]
    lw['g_ffn'] = p['norm_ffn'][l][None, :]
    return lw


def _rows_per_step(batch, want):
    while batch % want:
        want //= 2
    return want


def _run_group(x, shift_state, wkv_state, gla_state, big, weights, g_final):
    w_cat, w_o, w_gu, w_down = big
    B, T, _ = x.shape
    fresh = shift_state is None
    C = CHUNK if fresh else SAMPLE_CHUNK
    Tp = -(-T // C) * C
    BB = _rows_per_step(B, PROMPT_ROWS_PER_STEP if fresh else SAMPLE_ROWS_PER_STEP)
    h = x.reshape(B * T, D_MODEL)
    new_shift, new_wkv, new_gla = [], [], []
    P_first = None
    depth = len(weights)
    for l, lw in enumerate(weights):
        new_shift.append(_rms_call(h.reshape(B, T, D_MODEL)[:, -1], lw['g_mix']))
        P = _proj_call(h, lw['g_mix'], w_cat, l, lw['mu_cat'], None if fresh else shift_state[l], T)
        P = jnp.pad(P.reshape(B, T, N_PROJ), ((0, 0), (0, Tp - T), (0, 0)))
        h3 = jnp.pad(h.reshape(B, T, D_MODEL), ((0, 0), (0, Tp - T), (0, 0)))
        h3, s_wkv, s_gla = _mixer_call(P, P_first if l > 0 else None, h3, wkv_state, gla_state, l, lw, w_o,
                                       C=C, n_valid=min(T, C), BB=BB)
        if l == 0:
            P_first = P
        if fresh:
            new_wkv.append(s_wkv)
            new_gla.append(s_gla)
        else:
            wkv_state, gla_state = s_wkv, s_gla
        h = h3[:, :T].reshape(B * T, D_MODEL)
        h = _ffn_call(h, lw['g_ffn'], w_gu, w_down, l, g_final, final_norm=(l == depth - 1))
    if fresh:
        wkv_state, gla_state = jnp.stack(new_wkv), jnp.stack(new_gla)
    return (h.reshape(B, T, D_MODEL), jnp.stack(new_shift), wkv_state, gla_state)


def kernel(x_prompt, x_sample, state_shift, state_wkv, state_gla, norm_mix, w_in, mu_shift, rwkv_w0, rwkv_w2, rwkv_a0, rwkv_a2, rwkv_k_k, rwkv_k_a, rwkv_r_k, rwkv_lnx_g, rwkv_lnx_b, vres_w1, vres_mu, vres_w2, vres_v0, gla_a2, gla_ab, gla_norm_g, w_o, norm_ffn, ffn_w_gu, ffn_w_down, norm_final):
    p = dict(norm_mix=norm_mix, rwkv_w0=rwkv_w0, rwkv_w2=rwkv_w2, rwkv_a0=rwkv_a0, rwkv_a2=rwkv_a2,
             rwkv_k_k=rwkv_k_k, rwkv_k_a=rwkv_k_a, rwkv_r_k=rwkv_r_k, rwkv_lnx_g=rwkv_lnx_g,
             rwkv_lnx_b=rwkv_lnx_b, vres_w2=vres_w2, vres_v0=vres_v0, gla_a2=gla_a2, gla_ab=gla_ab,
             gla_norm_g=gla_norm_g, w_o=w_o, norm_ffn=norm_ffn, ffn_w_gu=ffn_w_gu, ffn_w_down=ffn_w_down)
    depth = w_in.shape[0]
    weights = [_layer_weights(l, mu_shift, vres_mu, p) for l in range(depth)]
    w_cat = _wcat_call(w_in, jnp.concatenate([jnp.zeros_like(vres_w1[:1]), vres_w1], axis=0))
    big = (w_cat, w_o.astype(BF16), ffn_w_gu.astype(BF16), ffn_w_down.astype(BF16))
    g_final = norm_final[None, :]
    y_p, shift_p, wkv_p, gla_p = _run_group(x_prompt, None, None, None, big, weights, g_final)
    y_s, shift_s, wkv_s, gla_s = _run_group(x_sample, state_shift, state_wkv, state_gla, big, weights, g_final)
    return (y_p, y_s, shift_p, wkv_p, gla_p, shift_s, wkv_s, gla_s)
```

```python
import functools

import jax
import jax.numpy as jnp
from jax import lax
from jax.experimental import pallas as pl
from jax.experimental.pallas import tpu as pltpu

F32 = jnp.float32
BF16 = jnp.bfloat16

LANES = 128
D_MODEL = 1024
RWKV_HEAD_DIM = 64
RWKV_HEADS = D_MODEL // RWKV_HEAD_DIM
HEADS_PER_PAIR = LANES // RWKV_HEAD_DIM
N_PAIRS = RWKV_HEADS // HEADS_PER_PAIR
LORA_DECAY = 64
LORA_ICLR = 64
LORA_VRES = 32
RWKV_GN_EPS = 64e-5
GLA_HEADS = 4
GLA_DK = 128
GLA_DV = 256
GLA_GATE_RANK = 16
GLA_GATE_NORMALIZER = 16.0
GLA_NORM_EPS = 1e-5
FFN_HIDDEN = 2816
NORM_EPS = 1e-6
CHUNK = 64
SAMPLE_CHUNK = 8

COL_R, COL_K, COL_V = 0, 1024, 2048
COL_LO, COL_VLO, COL_GLO = 3072, 3200, 3328
COL_GQ = 3584
COL_GATE_A, COL_GATE_B = 4096, 5120
COL_GV, COL_GK = 6144, 7168
N_SHIFTED = 3328
N_PROJ = 7680
PROJ_TM = 512
PROJ_SUB = 512
PROMPT_PROJ_DTYPE = BF16
VMEM_LIMIT = 48 * 1024 * 1024
PROMPT_ROWS_PER_STEP = 4
SAMPLE_ROWS_PER_STEP = 2
DECAY_SCALE = 0.6065306597126334


def _cparams(sem):
    return pltpu.CompilerParams(dimension_semantics=sem, vmem_limit_bytes=VMEM_LIMIT)


def _mm(a, b):
    return jnp.dot(a, b, preferred_element_type=F32)


def _mm_nt(a, b):
    return lax.dot_general(a, b, (((1,), (1,)), ((), ())), preferred_element_type=F32)


def _mm_tn(a, b):
    return lax.dot_general(a, b, (((0,), (0,)), ((), ())), preferred_element_type=F32)


def _sigmoid(x):
    return 0.5 * jnp.tanh(0.5 * x) + 0.5


def _softplus(x):
    return jnp.maximum(x, 0.0) + jnp.log(1.0 + jnp.exp(-jnp.abs(x)))


def _cumsum_rows(x):
    n = x.shape[0]
    row = lax.broadcasted_iota(jnp.int32, x.shape, 0)
    s = 1
    while s < n:
        x = x + jnp.where(row >= s, pltpu.roll(x, s, 0), 0.0)
        s *= 2
    return x


def _rms_rows(x, g):
    ms = jnp.mean(x * x, axis=-1, keepdims=True)
    return x * lax.rsqrt(ms + NORM_EPS) * g


def _proj_kernel(*refs, sample, tiles_per_seq, tm):
    if sample:
        x_ref, g_ref, w_ref, mu_ref, prev_ref, o_ref = refs
    else:
        x_ref, g_ref, w_ref, mu_ref, o_ref, last_ref = refs
    xs = _rms_rows(x_ref[...], g_ref[...]).astype(BF16)
    subs = [slice(n * PROJ_SUB, (n + 1) * PROJ_SUB) for n in range(N_PROJ // PROJ_SUB)]
    n_shift = -(-N_SHIFTED // PROJ_SUB)
    if sample:
        ps = prev_ref[...].astype(BF16)
        for n, cs in enumerate(subs):
            y = _mm_nt(xs, w_ref[cs, :])
            if n < n_shift:
                y = y + mu_ref[:, cs] * (_mm_nt(ps, w_ref[cs, :]) - y)
            o_ref[:, cs] = y.astype(o_ref.dtype)
        return

    i = pl.program_id(0)

    @pl.when(i == 0)
    def _():
        last_ref[...] = jnp.zeros_like(last_ref)

    row = lax.broadcasted_iota(jnp.int32, (tm, PROJ_SUB), 0)
    seq_start = i % tiles_per_seq == 0
    for n, cs in enumerate(subs):
        y = _mm_nt(xs, w_ref[cs, :])
        if n < n_shift:
            prev_row = jnp.where(seq_start, 0.0, last_ref[7:8, cs])
            shifted = jnp.where(row == 0, prev_row, pltpu.roll(y, 1, 0))
            last_ref[:, cs] = y[tm - 8:, :]
            y = y + mu_ref[:, cs] * (shifted - y)
        o_ref[:, cs] = y.astype(o_ref.dtype)


def _proj_call(x, g, w, layer, mu, prev, seq_len):
    m = x.shape[0]
    sample = prev is not None
    tm = min(PROJ_TM, m if sample else seq_len)
    row = lambda i: (i, 0)
    in_specs = [
        pl.BlockSpec((tm, D_MODEL), row),
        pl.BlockSpec((1, D_MODEL), lambda i: (0, 0)),
        pl.BlockSpec((None, N_PROJ, D_MODEL), lambda i: (layer, 0, 0), pipeline_mode=pl.Buffered(1)),
        pl.BlockSpec((1, N_PROJ), lambda i: (0, 0)),
    ]
    args = [x, g, w, mu]
    scratch = []
    if sample:
        in_specs.append(pl.BlockSpec((tm, D_MODEL), row))
        args.append(prev)
    else:
        scratch.append(pltpu.VMEM((8, -(-N_SHIFTED // PROJ_SUB) * PROJ_SUB), F32))
    return pl.pallas_call(
        functools.partial(_proj_kernel, sample=sample, tiles_per_seq=max(seq_len // tm, 1), tm=tm),
        grid=(m // tm,),
        in_specs=in_specs,
        out_specs=pl.BlockSpec((tm, N_PROJ), row),
        out_shape=jax.ShapeDtypeStruct((m, N_PROJ), F32 if sample else PROMPT_PROJ_DTYPE),
        scratch_shapes=scratch,
        compiler_params=_cparams(("arbitrary",)),
        name="proj_sample" if sample else "proj_prompt",
    )(*args)


class _Shared:
    pass


def _rwkv_row_stages(bi, R, res, *, C, n_valid, has_vres):
    H = RWKV_HEAD_DIM
    head0, trow = R.head0, R.trow
    sls = [slice(p * LANES, (p + 1) * LANES) for p in range(N_PAIRS)]
    scr = [bi * N_PAIRS + p for p in range(N_PAIRS)]
    n_sq = C.bit_length() - 2
    fused = 2 * C == LANES
    st = {}
    bf = lambda x: x.astype(BF16)
    cat0 = lambda x, y: jnp.concatenate([x, y], axis=0)
    cat1 = lambda x, y: jnp.concatenate([x, y], axis=1)

    def each(f, *lists):
        return [f(*xs) for xs in zip(*lists)]

    def stack(x):
        return jnp.concatenate([jnp.where(head0, x, 0.0), jnp.where(head0, 0.0, x)], axis=0)

    def head_sum(x):
        s0 = jnp.sum(jnp.where(head0, x, 0.0), axis=-1, keepdims=True)
        s1 = jnp.sum(jnp.where(head0, 0.0, x), axis=-1, keepdims=True)
        return jnp.where(head0, s0, s1)

    def lora():
        lo = R.lo_ref[bi].astype(F32)
        w2a = R.w2a_ref[...]
        st['wl'] = _mm(jnp.where(head0, jnp.tanh(lo), 0.0), w2a)
        st['al'] = _mm(jnp.where(head0, 0.0, lo), w2a)
        if has_vres:
            st['vg'] = _mm(R.vlo_ref[bi].astype(F32), R.vw2_ref[...])

    def operands():
        vec = lambda row: [R.vec_ref[row:row + 1, sl] for sl in sls]
        w0, a0, k_k, k_a, r_k, lnx_g, lnx_b, v0 = (vec(i) for i in range(8))
        r = [R.r_ref[bi, :, sl].astype(F32) for sl in sls]
        k = [R.k_ref[bi, :, sl].astype(F32) for sl in sls]
        v = [R.v_ref[bi, :, sl].astype(F32) for sl in sls]
        if has_vres:
            v = [vp + (R.vf_ref[bi, :, sl].astype(F32) - vp) * _sigmoid(z + st['vg'][:, sl])
                 for vp, sl, z in zip(v, sls, v0)]

        def decay_log(w0p, sl):
            ld = -DECAY_SCALE * _sigmoid(w0p + st['wl'][:, sl])
            if n_valid < C:
                ld = jnp.where(trow < n_valid, ld, 0.0)
            return ld

        ld = each(decay_log, w0, sls)
        a = each(lambda z, sl: _sigmoid(z + st['al'][:, sl]), a0, sls)

        def unit_kk(kp, kkp):
            kk = kp * kkp
            return kk * lax.rsqrt(jnp.maximum(head_sum(kk * kk), 1e-24))

        kk = each(unit_kk, k, k_k)
        k = each(lambda kp, ap, kap: kp * (1.0 + (ap - 1.0) * kap), k, a, k_a)
        b = each(_cumsum_rows, ld)
        e_b = each(jnp.exp, b)
        e_nb = each(lambda x: jnp.exp(-x), b)
        st['al_s'] = each(lambda kkp, bp, ldp: bf(stack(-kkp * jnp.exp(bp - ldp))), kk, b, ld)
        st['be'] = each(lambda ap, kkp, e: bf(stack(ap * kkp * e)), a, kk, e_nb)
        st['kt'] = each(lambda kp, e: bf(stack(kp * e)), k, e_nb)
        st['rt'] = each(lambda rp, e: bf(stack(rp * e)), r, e_b)
        st['vs'] = each(lambda x: bf(stack(x)), v)
        st['decay'] = [e[C - 1:C, :] for e in e_b]
        st['S32'] = [R.s_scr[i] for i in scr]
        st['epi'] = (r, k, v, r_k, lnx_g, lnx_b)

    def finish(o2, s_new):
        for i, s in zip(scr, s_new):
            R.s_scr[i] = s
        r, k, v, r_k, lnx_g, lnx_b = st['epi']
        outs = []
        for p in range(N_PAIRS):
            o = o2[p][:C] + o2[p][C:]
            mean = head_sum(o) * (1.0 / H)
            d = o - mean
            var = head_sum(d * d) * (1.0 / H)
            o = d * lax.rsqrt(var + RWKV_GN_EPS) * lnx_g[p] + lnx_b[p]
            outs.append(o + head_sum(r[p] * k[p] * r_k[p]) * v[p])
        res[bi] = outs

    if fused:
        def amat():
            operands()
            st['ar'] = each(cat0, st['al_s'], st['rt'])
            st['bk'] = each(cat0, st['be'], st['kt'])
            st['amat'] = each(lambda x, y: jnp.where(R.mask4, _mm_nt(x, y), 0.0), st['ar'], st['bk'])

        def xq():
            sv = each(lambda s, y: cat0(bf(s.T), y), st['S32'], st['vs'])
            st['x'] = each(lambda l, m, rhs: _mm(cat1(l, bf(m[:, LANES:])), rhs), st['ar'], st['amat'], sv)
            st['tinv'] = each(lambda m: R.eye + m[:LANES, :LANES], st['amat'])
            q = each(lambda m: bf(m[:LANES, :LANES]), st['amat'])
            st['q'] = each(lambda z: bf(_mm(z, z)), q)

        def double():
            y = each(lambda z, t: _mm(z, cat1(bf(t), z)), st['q'], st['tinv'])
            st['tinv'] = each(lambda t, z: t + z[:, :LANES], st['tinv'], y)
            st['q'] = each(lambda z: bf(z[:, LANES:]), y)

        def double_last():
            st['tinv'] = each(lambda t, z: t + _mm(z, bf(t)), st['tinv'], st['q'])

        def solve():
            st['cm'] = each(lambda t, z: bf(_mm(bf(t), bf(z[:LANES]))), st['tinv'], st['x'])

        def out_state():
            o2 = each(lambda z, m, cc: z[LANES:] + _mm(bf(m[LANES:, :LANES]), cc),
                      st['x'], st['amat'], st['cm'])
            s_new = each(lambda s, cc, y, rhs, e: (s + _mm_tn(cat0(cc, y), rhs)) * e,
                         st['S32'], st['cm'], st['vs'], st['bk'], st['decay'])
            finish(o2, s_new)

        return [lora, amat, xq] + [double] * (n_sq - 1) + [double_last, solve, out_state]

    def small_a():
        operands()
        S = each(bf, st['S32'])
        st['S'] = S
        a_ak = each(lambda x, y: bf(jnp.where(R.strict, _mm_nt(x, y), 0.0)), st['al_s'], st['kt'])
        st['a_rb'] = each(lambda x, y: bf(jnp.where(R.incl, _mm_nt(x, y), 0.0)), st['rt'], st['be'])
        st['a_rk'] = each(lambda x, y: bf(jnp.where(R.incl, _mm_nt(x, y), 0.0)), st['rt'], st['kt'])
        st['cm'] = each(lambda x, s, m, y: _mm_nt(x, s) + _mm(m, y), st['al_s'], S, a_ak, st['vs'])

    def small_solve():
        pw = each(lambda x, y: jnp.where(R.strict, _mm_nt(x, y), 0.0), st['al_s'], st['be'])
        tinv = each(lambda x: R.eye + x, pw)
        pwb = each(bf, pw)
        for _ in range(n_sq):
            pwb = each(lambda x: bf(_mm(x, x)), pwb)
            tinv = each(lambda t, x: t + _mm(bf(t), x), tinv, pwb)
        st['cm'] = each(lambda t, x: _mm(bf(t), bf(x)), tinv, st['cm'])

    def small_out():
        cm = each(bf, st['cm'])
        o2 = each(lambda x, s, m1, c1, m2, y: _mm_nt(x, s) + _mm(m1, c1) + _mm(m2, y),
                  st['rt'], st['S'], st['a_rb'], cm, st['a_rk'], st['vs'])
        s_new = each(lambda s, c1, x, y, z, e: (s + _mm_tn(c1, x) + _mm_tn(y, z)) * e,
                     st['S32'], cm, st['be'], st['vs'], st['kt'], st['decay'])
        finish(o2, s_new)

    return [lora, small_a] + ([small_solve] if n_valid > 1 else []) + [small_out]


def _gla_row_stages(bi, R, res, *, C, n_valid):
    heads = range(GLA_HEADS)
    ks = [slice(h * GLA_DK, (h + 1) * GLA_DK) for h in heads]
    vsl = [slice(h * GLA_DV, (h + 1) * GLA_DV) for h in heads]
    scr = [bi * GLA_HEADS + h for h in heads]
    st = {}

    def gate():
        st['gate'] = _mm(R.glo_ref[bi].astype(F32), R.ga2_ref[...])

    def scores():
        def cum_log_decay(s):
            la = -_softplus(-(st['gate'][:, s] + R.gab_ref[:, s])) * (1.0 / GLA_GATE_NORMALIZER)
            if n_valid < C:
                la = jnp.where(R.trow < n_valid, la, 0.0)
            return _cumsum_rows(la)

        b = [cum_log_decay(s) for s in ks]
        k = [R.gk_ref[bi, :, s].astype(F32) for s in ks]
        st['v'] = [R.gv_ref[bi, :, s].astype(F32) for s in vsl]
        qd = [R.gq_ref[bi, :, s].astype(F32) * (GLA_DK ** -0.5) * jnp.exp(bh) for s, bh in zip(ks, b)]
        kd = [kh * jnp.exp(-bh) for kh, bh in zip(k, b)]
        kl = [kh * jnp.exp(bh[C - 1:C, :] - bh) for kh, bh in zip(k, b)]
        St = [R.g_scr[i] for i in scr]
        st['att'] = [jnp.where(R.causal, _mm_nt(x, y), 0.0) for x, y in zip(qd, kd)]
        st['o_s'] = [_mm_nt(x, s) for x, s in zip(qd, St)]
        upd = [_mm_tn(x, y) for x, y in zip(st['v'], kl)]
        for i, s_old, bh, u in zip(scr, St, b, upd):
            R.g_scr[i] = s_old * jnp.exp(bh[C - 1:C, :]) + u

    def outputs():
        o = [_mm(x, y) + z for x, y, z in zip(st['att'], st['v'], st['o_s'])]
        res[bi] = [oh * lax.rsqrt(jnp.mean(oh * oh, axis=-1, keepdims=True) + GLA_NORM_EPS) * R.gg_ref[...]
                   for oh in o]

    return [gate, scores, outputs]


def _mixer_kernel(*refs, C, BB, n_valid, has_s0, has_vres, layer, copy_other_layers):
    R = _Shared()
    it = iter(refs)
    R.r_ref, R.k_ref, R.v_ref, R.lo_ref = next(it), next(it), next(it), next(it)
    if has_vres:
        R.vlo_ref, R.vf_ref = next(it), next(it)
    R.gq_ref, R.gk_ref, R.gv_ref, R.glo_ref = next(it), next(it), next(it), next(it)
    ga_ref, gb_ref, h_ref = next(it), next(it), next(it)
    if has_s0:
        s0w_ref, s0g_ref = next(it), next(it)
    R.w2a_ref = next(it)
    if has_vres:
        R.vw2_ref = next(it)
    R.vec_ref, R.ga2_ref, R.gab_ref, R.gg_ref, wo_ref = next(it), next(it), next(it), next(it), next(it)
    o_ref, sow_ref, sog_ref, R.s_scr, R.g_scr = next(it), next(it), next(it), next(it), next(it)

    c = pl.program_id(1)
    H = RWKV_HEAD_DIM
    rows = range(BB)
    if has_s0 and copy_other_layers:
        s0w_all, s0g_all, sow_all, sog_all = s0w_ref, s0g_ref, sow_ref, sog_ref
        s0w_ref, s0g_ref = s0w_all.at[layer], s0g_all.at[layer]
        sow_ref, sog_ref = sow_all.at[layer], sog_all.at[layer]

    @pl.when(c == 0)
    def _():
        R.s_scr[...] = jnp.zeros_like(R.s_scr)
        if has_s0:
            for bi in rows:
                for p in range(N_PAIRS):
                    R.s_scr[bi * N_PAIRS + p, 0:H, 0:H] = s0w_ref[bi, 2 * p]
                    R.s_scr[bi * N_PAIRS + p, H:2 * H, H:2 * H] = s0w_ref[bi, 2 * p + 1]
                for h in range(GLA_HEADS):
                    R.g_scr[bi * GLA_HEADS + h] = s0g_ref[bi, h].T
        else:
            R.g_scr[...] = jnp.zeros_like(R.g_scr)

    R.head0 = lax.broadcasted_iota(jnp.int32, (C, LANES), 1) < H
    R.trow = lax.broadcasted_iota(jnp.int32, (C, LANES), 0)
    ri = lax.broadcasted_iota(jnp.int32, (C, C), 0)
    ci = lax.broadcasted_iota(jnp.int32, (C, C), 1)
    R.causal = ri >= ci
    if 2 * C == LANES:
        ri = lax.broadcasted_iota(jnp.int32, (2 * LANES, 2 * LANES), 0)
        ci = lax.broadcasted_iota(jnp.int32, (2 * LANES, 2 * LANES), 1)
        rr, cc = ri % LANES, ci % LANES
        R.mask4 = ((rr // C) == (cc // C)) & ((rr % C) >= (cc % C) + (ri < LANES).astype(jnp.int32))
        r1 = lax.broadcasted_iota(jnp.int32, (LANES, LANES), 0)
        c1 = lax.broadcasted_iota(jnp.int32, (LANES, LANES), 1)
        R.eye = (r1 == c1).astype(F32)
    else:
        ri = lax.broadcasted_iota(jnp.int32, (2 * C, 2 * C), 0)
        ci = lax.broadcasted_iota(jnp.int32, (2 * C, 2 * C), 1)
        same_head = (ri // C) == (ci // C)
        R.strict = same_head & ((ri % C) > (ci % C))
        R.incl = same_head & ((ri % C) >= (ci % C))
        R.eye = (ri == ci).astype(F32)

    ro, go = {}, {}
    plans = []
    for bi in rows:
        plans.append(_rwkv_row_stages(bi, R, ro, C=C, n_valid=n_valid, has_vres=has_vres))
        plans.append(_gla_row_stages(bi, R, go, C=C, n_valid=n_valid))
    for d in range(max(len(stages) for stages in plans)):
        for stages in plans:
            if d < len(stages):
                stages[d]()

    merged = [_sigmoid(ga_ref[bi].astype(F32)) * jnp.concatenate(ro[bi], axis=1)
              + _sigmoid(gb_ref[bi].astype(F32)) * jnp.concatenate(go[bi], axis=1) for bi in rows]
    mix = _mm(jnp.concatenate(merged, axis=0).astype(BF16), wo_ref[...])
    for bi in rows:
        o_ref[bi] = h_ref[bi] + mix[bi * C:(bi + 1) * C]

    @pl.when(c == pl.num_programs(1) - 1)
    def _():
        for bi in rows:
            for p in range(N_PAIRS):
                sow_ref[bi, 2 * p] = R.s_scr[bi * N_PAIRS + p, 0:H, 0:H]
                sow_ref[bi, 2 * p + 1] = R.s_scr[bi * N_PAIRS + p, H:2 * H, H:2 * H]
            for h in range(GLA_HEADS):
                sog_ref[bi, h] = R.g_scr[bi * GLA_HEADS + h].T
        if has_s0 and copy_other_layers:
            for other in range(s0w_all.shape[0]):
                if other != layer:
                    sow_all[other] = s0w_all[other]
                    sog_all[other] = s0g_all[other]


def _mixer_call(P, Pfirst, h, s_wkv, s_gla, layer, lw, w_o, *, C, n_valid, BB):
    B, T, _ = P.shape
    has_vres = Pfirst is not None
    has_s0 = s_wkv is not None
    kw = GLA_HEADS * GLA_DK

    def cols(off, width):
        idx = off // width
        return pl.BlockSpec((BB, C, width), lambda b, c: (b, c, idx))

    whole = lambda shape: pl.BlockSpec(shape, lambda b, c: (0,) * len(shape))
    wkv_block = (BB, RWKV_HEADS, RWKV_HEAD_DIM, RWKV_HEAD_DIM)
    gla_block = (BB, GLA_HEADS, GLA_DK, GLA_DV)
    copy_other_layers = has_s0 and layer == 0
    if copy_other_layers:
        depth = s_wkv.shape[0]
        state_spec = lambda blk: pl.BlockSpec((depth,) + blk, lambda b, c: (0, b, 0, 0, 0))
        wkv_shape, gla_shape = s_wkv.shape, s_gla.shape
    elif has_s0:
        state_spec = lambda blk: pl.BlockSpec((None,) + blk, lambda b, c: (layer, b, 0, 0, 0))
        wkv_shape, gla_shape = s_wkv.shape, s_gla.shape
    else:
        state_spec = lambda blk: pl.BlockSpec(blk, lambda b, c: (b, 0, 0, 0))
        wkv_shape, gla_shape = (B,) + wkv_block[1:], (B,) + gla_block[1:]

    in_specs = [cols(COL_R, D_MODEL), cols(COL_K, D_MODEL), cols(COL_V, D_MODEL), cols(COL_LO, LANES)]
    args = [P, P, P, P]
    if has_vres:
        in_specs += [cols(COL_VLO, LANES), cols(COL_V, D_MODEL)]
        args += [P, Pfirst]
    in_specs += [cols(COL_GQ, kw), cols(COL_GK, kw), cols(COL_GV, D_MODEL), cols(COL_GLO, LANES),
                 cols(COL_GATE_A, D_MODEL), cols(COL_GATE_B, D_MODEL), cols(0, D_MODEL)]
    args += [P, P, P, P, P, P, h]
    aliases = {}
    if has_s0:
        if not copy_other_layers:
            aliases = {len(args): 1, len(args) + 1: 2}
        in_specs += [state_spec(wkv_block), state_spec(gla_block)]
        args += [s_wkv, s_gla]
    in_specs.append(whole((LANES, D_MODEL)))
    args.append(lw['w2a'])
    if has_vres:
        in_specs.append(whole((LANES, D_MODEL)))
        args.append(lw['vw2'])
    in_specs += [whole((8, D_MODEL)), whole((LANES, kw)), whole((1, kw)), whole((1, GLA_DV)),
                 pl.BlockSpec((None, D_MODEL, D_MODEL), lambda b, c: (layer, 0, 0))]
    args += [lw['vecs'], lw['ga2'], lw['gab'], lw['gla_g'], w_o]
    return pl.pallas_call(
        functools.partial(_mixer_kernel, C=C, BB=BB, n_valid=n_valid, has_s0=has_s0, has_vres=has_vres,
                          layer=layer, copy_other_layers=copy_other_layers),
        grid=(B // BB, T // C),
        in_specs=in_specs,
        out_specs=[pl.BlockSpec((BB, C, D_MODEL), lambda b, c: (b, c, 0)),
                   state_spec(wkv_block), state_spec(gla_block)],
        out_shape=[jax.ShapeDtypeStruct((B, T, D_MODEL), F32),
                   jax.ShapeDtypeStruct(wkv_shape, F32),
                   jax.ShapeDtypeStruct(gla_shape, F32)],
        input_output_aliases=aliases,
        scratch_shapes=[pltpu.VMEM((BB * N_PAIRS, LANES, LANES), F32),
                        pltpu.VMEM((BB * GLA_HEADS, GLA_DV, GLA_DK), F32)],
        compiler_params=_cparams(("parallel", "arbitrary")),
        name="mixer",
    )(*args)


FFN_SUB = 256
FFN_TM = 1024


def _ffn_kernel(h_ref, g_ref, wg_ref, wu_ref, wd_ref, gf_ref, o_ref, *, final_norm):
    h = h_ref[...]
    xs = _rms_rows(h, g_ref[...]).astype(BF16)
    out = h
    for n in range(FFN_HIDDEN // FFN_SUB):
        cs = slice(n * FFN_SUB, (n + 1) * FFN_SUB)
        gate = _mm(xs, wg_ref[:, cs])
        up = _mm(xs, wu_ref[:, cs])
        act = (gate * _sigmoid(gate) * up).astype(BF16)
        out = out + _mm(act, wd_ref[cs, :])
    if final_norm:
        out = _rms_rows(out, gf_ref[...])
    o_ref[...] = out


def _ffn_call(h, g, w_gu, w_down, layer, gf, final_norm):
    m = h.shape[0]
    tm = min(FFN_TM, m)
    row = lambda i: (i, 0)
    resident = pl.Buffered(1)
    return pl.pallas_call(
        functools.partial(_ffn_kernel, final_norm=final_norm),
        grid=(m // tm,),
        in_specs=[pl.BlockSpec((tm, D_MODEL), row),
                  pl.BlockSpec((1, D_MODEL), lambda i: (0, 0)),
                  pl.BlockSpec((None, D_MODEL, FFN_HIDDEN), lambda i: (layer, 0, 0), pipeline_mode=resident),
                  pl.BlockSpec((None, D_MODEL, FFN_HIDDEN), lambda i: (layer, 0, 1), pipeline_mode=resident),
                  pl.BlockSpec((None, FFN_HIDDEN, D_MODEL), lambda i: (layer, 0, 0), pipeline_mode=resident),
                  pl.BlockSpec((1, D_MODEL), lambda i: (0, 0))],
        out_specs=pl.BlockSpec((tm, D_MODEL), row),
        out_shape=jax.ShapeDtypeStruct((m, D_MODEL), F32),
        compiler_params=_cparams(("parallel",)),
        name="ffn",
    )(h, g, w_gu, w_gu, w_down, gf)


def _rms_kernel(x_ref, g_ref, o_ref):
    o_ref[...] = _rms_rows(x_ref[...], g_ref[...])


def _rms_call(x, g):
    return pl.pallas_call(
        _rms_kernel,
        out_shape=jax.ShapeDtypeStruct(x.shape, F32),
        name="rmsnorm_rows",
    )(x, g)


SRC_GQ, SRC_GK, SRC_GV, SRC_GLO, SRC_GATE_A, SRC_GATE_B = 3200, 3712, 4224, 5248, 5264, 6288
WCAT_COLS = 256


def _wcat_kernel(w_ref, v1_ref, o_ref):
    def put(dst, src, width):
        o_ref[dst:dst + width, :] = w_ref[src:src + width, :].astype(BF16)

    def zero(lo, hi):
        o_ref[lo:hi, :] = jnp.zeros((hi - lo, o_ref.shape[1]), BF16)

    put(COL_R, 0, COL_LO + 2 * LORA_DECAY)
    o_ref[COL_VLO:COL_VLO + LORA_VRES, :] = v1_ref[...].astype(BF16)
    zero(COL_VLO + LORA_VRES, COL_GLO)
    put(COL_GLO, SRC_GLO, GLA_GATE_RANK)
    zero(COL_GLO + GLA_GATE_RANK, COL_GQ)
    put(COL_GQ, SRC_GQ, GLA_HEADS * GLA_DK)
    put(COL_GATE_A, SRC_GATE_A, D_MODEL)
    put(COL_GATE_B, SRC_GATE_B, D_MODEL)
    put(COL_GV, SRC_GV, GLA_HEADS * GLA_DV)
    put(COL_GK, SRC_GK, GLA_HEADS * GLA_DK)


def _wcat_call(w_in_t, vres_w1_t):
    depth, n_in, _ = w_in_t.shape
    return pl.pallas_call(
        _wcat_kernel,
        grid=(depth, D_MODEL // WCAT_COLS),
        in_specs=[pl.BlockSpec((None, n_in, WCAT_COLS), lambda l, i: (l, 0, i)),
                  pl.BlockSpec((None, LORA_VRES, WCAT_COLS), lambda l, i: (l, 0, i))],
        out_specs=pl.BlockSpec((None, N_PROJ, WCAT_COLS), lambda l, i: (l, 0, i)),
        out_shape=jax.ShapeDtypeStruct((depth, N_PROJ, D_MODEL), BF16),
        compiler_params=_cparams(("parallel", "parallel")),
        name="w_in_reorder",
    )(w_in_t, vres_w1_t)


def _pad_rows(x, n):
    return jnp.pad(x, ((0, n - x.shape[0]), (0, 0)))


def _layer_weights(l, mu_shift, vres_mu, p):
    mu = mu_shift[l]
    zeros_mu = lambda n: jnp.zeros((n,), F32)
    vlo_mu = jnp.pad(vres_mu[l - 1], (0, LANES - LORA_VRES)) if l > 0 else zeros_mu(LANES)
    mu_cat = jnp.concatenate([mu[:COL_VLO], vlo_mu, zeros_mu(N_PROJ - N_SHIFTED)])[None, :]
    lw = dict(mu_cat=mu_cat)
    lw['w2a'] = jnp.concatenate([p['rwkv_w2'][l], p['rwkv_a2'][l]], axis=0)
    lw['vecs'] = jnp.stack([p['rwkv_w0'][l], p['rwkv_a0'][l], p['rwkv_k_k'][l], p['rwkv_k_a'][l],
                            p['rwkv_r_k'][l].reshape(-1), p['rwkv_lnx_g'][l], p['rwkv_lnx_b'][l],
                            p['vres_v0'][l - 1] if l > 0 else zeros_mu(D_MODEL)], axis=0)
    lw['vw2'] = _pad_rows(p['vres_w2'][l - 1], LANES) if l > 0 else None
    lw['ga2'] = _pad_rows(p['gla_a2'][l], LANES)
    lw['gab'] = p['gla_ab'][l][None, :]
    lw['gla_g'] = p['gla_norm_g'][l][None, :]
    lw['g_mix'] = p['norm_mix'][l][None, :]
    lw['g_ffn'] = p['norm_ffn'][l][None, :]
    return lw


def _rows_per_step(batch, want):
    while batch % want:
        want //= 2
    return want


def _run_group(x, shift_state, wkv_state, gla_state, big, weights, g_final):
    w_cat, w_o, w_gu, w_down = big
    B, T, _ = x.shape
    fresh = shift_state is None
    C = CHUNK if fresh else SAMPLE_CHUNK
    Tp = -(-T // C) * C
    BB = _rows_per_step(B, PROMPT_ROWS_PER_STEP if fresh else SAMPLE_ROWS_PER_STEP)
    h = x.reshape(B * T, D_MODEL)
    new_shift, new_wkv, new_gla = [], [], []
    P_first = None
    depth = len(weights)
    for l, lw in enumerate(weights):
        new_shift.append(_rms_call(h.reshape(B, T, D_MODEL)[:, -1], lw['g_mix']))
        P = _proj_call(h, lw['g_mix'], w_cat, l, lw['mu_cat'], None if fresh else shift_state[l], T)
        P = jnp.pad(P.reshape(B, T, N_PROJ), ((0, 0), (0, Tp - T), (0, 0)))
        h3 = jnp.pad(h.reshape(B, T, D_MODEL), ((0, 0), (0, Tp - T), (0, 0)))
        h3, s_wkv, s_gla = _mixer_call(P, P_first if l > 0 else None, h3, wkv_state, gla_state, l, lw, w_o,
                                       C=C, n_valid=min(T, C), BB=BB)
        if l == 0:
            P_first = P
        if fresh:
            new_wkv.append(s_wkv)
            new_gla.append(s_gla)
        else:
            wkv_state, gla_state = s_wkv, s_gla
        h = h3[:, :T].reshape(B * T, D_MODEL)
        h = _ffn_call(h, lw['g_ffn'], w_gu, w_down, l, g_final, final_norm=(l == depth - 1))
    if fresh:
        wkv_state, gla_state = jnp.stack(new_wkv), jnp.stack(new_gla)
    return (h.reshape(B, T, D_MODEL), jnp.stack(new_shift), wkv_state, gla_state)


def kernel(x_prompt, x_sample, state_shift, state_wkv, state_gla, norm_mix, w_in, mu_shift, rwkv_w0, rwkv_w2, rwkv_a0, rwkv_a2, rwkv_k_k, rwkv_k_a, rwkv_r_k, rwkv_lnx_g, rwkv_lnx_b, vres_w1, vres_mu, vres_w2, vres_v0, gla_a2, gla_ab, gla_norm_g, w_o, norm_ffn, ffn_w_gu, ffn_w_down, norm_final):
    p = dict(norm_mix=norm_mix, rwkv_w0=rwkv_w0, rwkv_w2=rwkv_w2, rwkv_a0=rwkv_a0, rwkv_a2=rwkv_a2,
             rwkv_k_k=rwkv_k_k, rwkv_k_a=rwkv_k_a, rwkv_r_k=rwkv_r_k, rwkv_lnx_g=rwkv_lnx_g,
             rwkv_lnx_b=rwkv_lnx_b, vres_w2=vres_w2, vres_v0=vres_v0, gla_a2=gla_a2, gla_ab=gla_ab,
             gla_norm_g=gla_norm_g, w_o=w_o, norm_ffn=norm_ffn, ffn_w_gu=ffn_w_gu, ffn_w_down=ffn_w_down)
    depth = w_in.shape[0]
    weights = [_layer_weights(l, mu_shift, vres_mu, p) for l in range(depth)]
    vres_w1_t = jnp.swapaxes(vres_w1, 1, 2)
    w_cat = _wcat_call(jnp.swapaxes(w_in, 1, 2),
                       jnp.concatenate([jnp.zeros_like(vres_w1_t[:1]), vres_w1_t], axis=0))
    big = (w_cat, w_o.astype(BF16), ffn_w_gu.astype(BF16), ffn_w_down.astype(BF16))
    g_final = norm_final[None, :]
    y_p, shift_p, wkv_p, gla_p = _run_group(x_prompt, None, None, None, big, weights, g_final)
    y_s, shift_s, wkv_s, gla_s = _run_group(x_sample, state_shift, state_wkv, state_gla, big, weights, g_final)
    return (y_p, y_s, shift_p, wkv_p, gla_p, shift_s, wkv_s, gla_s)
```

```python
import functools

import jax
import jax.numpy as jnp
from jax import lax
from jax.experimental import pallas as pl
from jax.experimental.pallas import tpu as pltpu

F32 = jnp.float32
BF16 = jnp.bfloat16

LANES = 128
D_MODEL = 1024
RWKV_HEAD_DIM = 64
RWKV_HEADS = D_MODEL // RWKV_HEAD_DIM
HEADS_PER_PAIR = LANES // RWKV_HEAD_DIM
N_PAIRS = RWKV_HEADS // HEADS_PER_PAIR
LORA_DECAY = 64
LORA_ICLR = 64
LORA_VRES = 32
RWKV_GN_EPS = 64e-5
GLA_HEADS = 4
GLA_DK = 128
GLA_DV = 256
GLA_GATE_RANK = 16
GLA_GATE_NORMALIZER = 16.0
GLA_NORM_EPS = 1e-5
FFN_HIDDEN = 2816
NORM_EPS = 1e-6
CHUNK = 64
SAMPLE_CHUNK = 8

COL_R, COL_K, COL_V = 0, 1024, 2048
COL_LO, COL_VLO, COL_GLO = 3072, 3200, 3328
COL_GQ = 3584
COL_GATE_A, COL_GATE_B = 4096, 5120
COL_GV, COL_GK = 6144, 7168
N_SHIFTED = 3328
N_PROJ = 7680
PROJ_TM = 512
PROJ_SUB = 512
PROMPT_PROJ_DTYPE = BF16
VMEM_LIMIT = 48 * 1024 * 1024
PROMPT_ROWS_PER_STEP = 4
SAMPLE_ROWS_PER_STEP = 2
DECAY_SCALE = 0.6065306597126334


def _cparams(sem):
    return pltpu.CompilerParams(dimension_semantics=sem, vmem_limit_bytes=VMEM_LIMIT)


def _mm(a, b):
    return jnp.dot(a, b, preferred_element_type=F32)


def _mm_nt(a, b):
    return lax.dot_general(a, b, (((1,), (1,)), ((), ())), preferred_element_type=F32)


def _mm_tn(a, b):
    return lax.dot_general(a, b, (((0,), (0,)), ((), ())), preferred_element_type=F32)


def _sigmoid(x):
    return 0.5 * jnp.tanh(0.5 * x) + 0.5


def _softplus(x):
    return jnp.maximum(x, 0.0) + jnp.log(1.0 + jnp.exp(-jnp.abs(x)))


def _cumsum_rows(x):
    n = x.shape[0]
    row = lax.broadcasted_iota(jnp.int32, x.shape, 0)
    s = 1
    while s < n:
        x = x + jnp.where(row >= s, pltpu.roll(x, s, 0), 0.0)
        s *= 2
    return x


def _rms_rows(x, g):
    ms = jnp.mean(x * x, axis=-1, keepdims=True)
    return x * lax.rsqrt(ms + NORM_EPS) * g


def _proj_kernel(*refs, sample, tiles_per_seq, tm):
    if sample:
        x_ref, g_ref, w_ref, mu_ref, prev_ref, o_ref = refs
    else:
        x_ref, g_ref, w_ref, mu_ref, o_ref, last_ref = refs
    xs = _rms_rows(x_ref[...], g_ref[...]).astype(BF16)
    subs = [slice(n * PROJ_SUB, (n + 1) * PROJ_SUB) for n in range(N_PROJ // PROJ_SUB)]
    n_shift = -(-N_SHIFTED // PROJ_SUB)
    if sample:
        ps = prev_ref[...].astype(BF16)
        for n, cs in enumerate(subs):
            y = _mm(xs, w_ref[:, cs])
            if n < n_shift:
                y = y + mu_ref[:, cs] * (_mm(ps, w_ref[:, cs]) - y)
            o_ref[:, cs] = y.astype(o_ref.dtype)
        return

    i = pl.program_id(0)

    @pl.when(i == 0)
    def _():
        last_ref[...] = jnp.zeros_like(last_ref)

    row = lax.broadcasted_iota(jnp.int32, (tm, PROJ_SUB), 0)
    seq_start = i % tiles_per_seq == 0
    for n, cs in enumerate(subs):
        y = _mm(xs, w_ref[:, cs])
        if n < n_shift:
            prev_row = jnp.where(seq_start, 0.0, last_ref[7:8, cs])
            shifted = jnp.where(row == 0, prev_row, pltpu.roll(y, 1, 0))
            last_ref[:, cs] = y[tm - 8:, :]
            y = y + mu_ref[:, cs] * (shifted - y)
        o_ref[:, cs] = y.astype(o_ref.dtype)


def _proj_call(x, g, w, layer, mu, prev, seq_len):
    m = x.shape[0]
    sample = prev is not None
    tm = min(PROJ_TM, m if sample else seq_len)
    row = lambda i: (i, 0)
    in_specs = [
        pl.BlockSpec((tm, D_MODEL), row),
        pl.BlockSpec((1, D_MODEL), lambda i: (0, 0)),
        pl.BlockSpec((None, D_MODEL, N_PROJ), lambda i: (layer, 0, 0), pipeline_mode=pl.Buffered(1)),
        pl.BlockSpec((1, N_PROJ), lambda i: (0, 0)),
    ]
    args = [x, g, w, mu]
    scratch = []
    if sample:
        in_specs.append(pl.BlockSpec((tm, D_MODEL), row))
        args.append(prev)
    else:
        scratch.append(pltpu.VMEM((8, -(-N_SHIFTED // PROJ_SUB) * PROJ_SUB), F32))
    return pl.pallas_call(
        functools.partial(_proj_kernel, sample=sample, tiles_per_seq=max(seq_len // tm, 1), tm=tm),
        grid=(m // tm,),
        in_specs=in_specs,
        out_specs=pl.BlockSpec((tm, N_PROJ), row),
        out_shape=jax.ShapeDtypeStruct((m, N_PROJ), F32 if sample else PROMPT_PROJ_DTYPE),
        scratch_shapes=scratch,
        compiler_params=_cparams(("arbitrary",)),
        name="proj_sample" if sample else "proj_prompt",
    )(*args)


class _Shared:
    pass


def _rwkv_row_stages(bi, R, res, *, C, n_valid, has_vres):
    H = RWKV_HEAD_DIM
    head0, trow = R.head0, R.trow
    sls = [slice(p * LANES, (p + 1) * LANES) for p in range(N_PAIRS)]
    scr = [bi * N_PAIRS + p for p in range(N_PAIRS)]
    n_sq = C.bit_length() - 2
    fused = 2 * C == LANES
    st = {}
    bf = lambda x: x.astype(BF16)
    cat0 = lambda x, y: jnp.concatenate([x, y], axis=0)
    cat1 = lambda x, y: jnp.concatenate([x, y], axis=1)

    def each(f, *lists):
        return [f(*xs) for xs in zip(*lists)]

    def stack(x):
        return jnp.concatenate([jnp.where(head0, x, 0.0), jnp.where(head0, 0.0, x)], axis=0)

    def head_sum(x):
        s0 = jnp.sum(jnp.where(head0, x, 0.0), axis=-1, keepdims=True)
        s1 = jnp.sum(jnp.where(head0, 0.0, x), axis=-1, keepdims=True)
        return jnp.where(head0, s0, s1)

    def lora():
        lo = R.lo_ref[bi].astype(F32)
        w2a = R.w2a_ref[...]
        st['wl'] = _mm(jnp.where(head0, jnp.tanh(lo), 0.0), w2a)
        st['al'] = _mm(jnp.where(head0, 0.0, lo), w2a)
        if has_vres:
            st['vg'] = _mm(R.vlo_ref[bi].astype(F32), R.vw2_ref[...])

    def operands():
        vec = lambda row: [R.vec_ref[row:row + 1, sl] for sl in sls]
        w0, a0, k_k, k_a, r_k, lnx_g, lnx_b, v0 = (vec(i) for i in range(8))
        r = [R.r_ref[bi, :, sl].astype(F32) for sl in sls]
        k = [R.k_ref[bi, :, sl].astype(F32) for sl in sls]
        v = [R.v_ref[bi, :, sl].astype(F32) for sl in sls]
        if has_vres:
            v = [vp + (R.vf_ref[bi, :, sl].astype(F32) - vp) * _sigmoid(z + st['vg'][:, sl])
                 for vp, sl, z in zip(v, sls, v0)]

        def decay_log(w0p, sl):
            ld = -DECAY_SCALE * _sigmoid(w0p + st['wl'][:, sl])
            if n_valid < C:
                ld = jnp.where(trow < n_valid, ld, 0.0)
            return ld

        ld = each(decay_log, w0, sls)
        a = each(lambda z, sl: _sigmoid(z + st['al'][:, sl]), a0, sls)

        def unit_kk(kp, kkp):
            kk = kp * kkp
            return kk * lax.rsqrt(jnp.maximum(head_sum(kk * kk), 1e-24))

        kk = each(unit_kk, k, k_k)
        k = each(lambda kp, ap, kap: kp * (1.0 + (ap - 1.0) * kap), k, a, k_a)
        b = each(_cumsum_rows, ld)
        e_b = each(jnp.exp, b)
        e_nb = each(lambda x: jnp.exp(-x), b)
        st['al_s'] = each(lambda kkp, bp, ldp: bf(stack(-kkp * jnp.exp(bp - ldp))), kk, b, ld)
        st['be'] = each(lambda ap, kkp, e: bf(stack(ap * kkp * e)), a, kk, e_nb)
        st['kt'] = each(lambda kp, e: bf(stack(kp * e)), k, e_nb)
        st['rt'] = each(lambda rp, e: bf(stack(rp * e)), r, e_b)
        st['vs'] = each(lambda x: bf(stack(x)), v)
        st['decay'] = [e[C - 1:C, :] for e in e_b]
        st['S32'] = [R.s_scr[i] for i in scr]
        st['epi'] = (r, k, v, r_k, lnx_g, lnx_b)

    def finish(o2, s_new):
        for i, s in zip(scr, s_new):
            R.s_scr[i] = s
        r, k, v, r_k, lnx_g, lnx_b = st['epi']
        outs = []
        for p in range(N_PAIRS):
            o = o2[p][:C] + o2[p][C:]
            mean = head_sum(o) * (1.0 / H)
            d = o - mean
            var = head_sum(d * d) * (1.0 / H)
            o = d * lax.rsqrt(var + RWKV_GN_EPS) * lnx_g[p] + lnx_b[p]
            outs.append(o + head_sum(r[p] * k[p] * r_k[p]) * v[p])
        res[bi] = outs

    if fused:
        def amat():
            operands()
            st['ar'] = each(cat0, st['al_s'], st['rt'])
            st['bk'] = each(cat0, st['be'], st['kt'])
            st['amat'] = each(lambda x, y: jnp.where(R.mask4, _mm_nt(x, y), 0.0), st['ar'], st['bk'])

        def xq():
            sv = each(lambda s, y: cat0(bf(s.T), y), st['S32'], st['vs'])
            st['x'] = each(lambda l, m, rhs: _mm(cat1(l, bf(m[:, LANES:])), rhs), st['ar'], st['amat'], sv)
            st['tinv'] = each(lambda m: R.eye + m[:LANES, :LANES], st['amat'])
            q = each(lambda m: bf(m[:LANES, :LANES]), st['amat'])
            st['q'] = each(lambda z: bf(_mm(z, z)), q)

        def double():
            y = each(lambda z, t: _mm(z, cat1(bf(t), z)), st['q'], st['tinv'])
            st['tinv'] = each(lambda t, z: t + z[:, :LANES], st['tinv'], y)
            st['q'] = each(lambda z: bf(z[:, LANES:]), y)

        def double_last():
            st['tinv'] = each(lambda t, z: t + _mm(z, bf(t)), st['tinv'], st['q'])

        def solve():
            st['cm'] = each(lambda t, z: bf(_mm(bf(t), bf(z[:LANES]))), st['tinv'], st['x'])

        def out_state():
            o2 = each(lambda z, m, cc: z[LANES:] + _mm(bf(m[LANES:, :LANES]), cc),
                      st['x'], st['amat'], st['cm'])
            s_new = each(lambda s, cc, y, rhs, e: (s + _mm_tn(cat0(cc, y), rhs)) * e,
                         st['S32'], st['cm'], st['vs'], st['bk'], st['decay'])
            finish(o2, s_new)

        return [lora, amat, xq] + [double] * (n_sq - 1) + [double_last, solve, out_state]

    def small_a():
        operands()
        S = each(bf, st['S32'])
        st['S'] = S
        a_ak = each(lambda x, y: bf(jnp.where(R.strict, _mm_nt(x, y), 0.0)), st['al_s'], st['kt'])
        st['a_rb'] = each(lambda x, y: bf(jnp.where(R.incl, _mm_nt(x, y), 0.0)), st['rt'], st['be'])
        st['a_rk'] = each(lambda x, y: bf(jnp.where(R.incl, _mm_nt(x, y), 0.0)), st['rt'], st['kt'])
        st['cm'] = each(lambda x, s, m, y: _mm_nt(x, s) + _mm(m, y), st['al_s'], S, a_ak, st['vs'])

    def small_solve():
        pw = each(lambda x, y: jnp.where(R.strict, _mm_nt(x, y), 0.0), st['al_s'], st['be'])
        tinv = each(lambda x: R.eye + x, pw)
        pwb = each(bf, pw)
        for _ in range(n_sq):
            pwb = each(lambda x: bf(_mm(x, x)), pwb)
            tinv = each(lambda t, x: t + _mm(bf(t), x), tinv, pwb)
        st['cm'] = each(lambda t, x: _mm(bf(t), bf(x)), tinv, st['cm'])

    def small_out():
        cm = each(bf, st['cm'])
        o2 = each(lambda x, s, m1, c1, m2, y: _mm_nt(x, s) + _mm(m1, c1) + _mm(m2, y),
                  st['rt'], st['S'], st['a_rb'], cm, st['a_rk'], st['vs'])
        s_new = each(lambda s, c1, x, y, z, e: (s + _mm_tn(c1, x) + _mm_tn(y, z)) * e,
                     st['S32'], cm, st['be'], st['vs'], st['kt'], st['decay'])
        finish(o2, s_new)

    return [lora, small_a] + ([small_solve] if n_valid > 1 else []) + [small_out]


def _gla_row_stages(bi, R, res, *, C, n_valid):
    heads = range(GLA_HEADS)
    ks = [slice(h * GLA_DK, (h + 1) * GLA_DK) for h in heads]
    vsl = [slice(h * GLA_DV, (h + 1) * GLA_DV) for h in heads]
    scr = [bi * GLA_HEADS + h for h in heads]
    st = {}

    def gate():
        st['gate'] = _mm(R.glo_ref[bi].astype(F32), R.ga2_ref[...])

    def scores():
        def cum_log_decay(s):
            la = -_softplus(-(st['gate'][:, s] + R.gab_ref[:, s])) * (1.0 / GLA_GATE_NORMALIZER)
            if n_valid < C:
                la = jnp.where(R.trow < n_valid, la, 0.0)
            return _cumsum_rows(la)

        b = [cum_log_decay(s) for s in ks]
        k = [R.gk_ref[bi, :, s].astype(F32) for s in ks]
        st['v'] = [R.gv_ref[bi, :, s].astype(F32) for s in vsl]
        qd = [R.gq_ref[bi, :, s].astype(F32) * (GLA_DK ** -0.5) * jnp.exp(bh) for s, bh in zip(ks, b)]
        kd = [kh * jnp.exp(-bh) for kh, bh in zip(k, b)]
        kl = [kh * jnp.exp(bh[C - 1:C, :] - bh) for kh, bh in zip(k, b)]
        St = [R.g_scr[i] for i in scr]
        st['att'] = [jnp.where(R.causal, _mm_nt(x, y), 0.0) for x, y in zip(qd, kd)]
        st['o_s'] = [_mm_nt(x, s) for x, s in zip(qd, St)]
        upd = [_mm_tn(x, y) for x, y in zip(st['v'], kl)]
        for i, s_old, bh, u in zip(scr, St, b, upd):
            R.g_scr[i] = s_old * jnp.exp(bh[C - 1:C, :]) + u

    def outputs():
        o = [_mm(x, y) + z for x, y, z in zip(st['att'], st['v'], st['o_s'])]
        res[bi] = [oh * lax.rsqrt(jnp.mean(oh * oh, axis=-1, keepdims=True) + GLA_NORM_EPS) * R.gg_ref[...]
                   for oh in o]

    return [gate, scores, outputs]


def _mixer_kernel(*refs, C, BB, n_valid, has_s0, has_vres, layer, copy_other_layers):
    R = _Shared()
    it = iter(refs)
    R.r_ref, R.k_ref, R.v_ref, R.lo_ref = next(it), next(it), next(it), next(it)
    if has_vres:
        R.vlo_ref, R.vf_ref = next(it), next(it)
    R.gq_ref, R.gk_ref, R.gv_ref, R.glo_ref = next(it), next(it), next(it), next(it)
    ga_ref, gb_ref, h_ref = next(it), next(it), next(it)
    if has_s0:
        s0w_ref, s0g_ref = next(it), next(it)
    R.w2a_ref = next(it)
    if has_vres:
        R.vw2_ref = next(it)
    R.vec_ref, R.ga2_ref, R.gab_ref, R.gg_ref, wo_ref = next(it), next(it), next(it), next(it), next(it)
    o_ref, sow_ref, sog_ref, R.s_scr, R.g_scr = next(it), next(it), next(it), next(it), next(it)

    c = pl.program_id(1)
    H = RWKV_HEAD_DIM
    rows = range(BB)
    if has_s0 and copy_other_layers:
        s0w_all, s0g_all, sow_all, sog_all = s0w_ref, s0g_ref, sow_ref, sog_ref
        s0w_ref, s0g_ref = s0w_all.at[layer], s0g_all.at[layer]
        sow_ref, sog_ref = sow_all.at[layer], sog_all.at[layer]

    @pl.when(c == 0)
    def _():
        R.s_scr[...] = jnp.zeros_like(R.s_scr)
        if has_s0:
            for bi in rows:
                for p in range(N_PAIRS):
                    R.s_scr[bi * N_PAIRS + p, 0:H, 0:H] = s0w_ref[bi, 2 * p]
                    R.s_scr[bi * N_PAIRS + p, H:2 * H, H:2 * H] = s0w_ref[bi, 2 * p + 1]
                for h in range(GLA_HEADS):
                    R.g_scr[bi * GLA_HEADS + h] = s0g_ref[bi, h].T
        else:
            R.g_scr[...] = jnp.zeros_like(R.g_scr)

    R.head0 = lax.broadcasted_iota(jnp.int32, (C, LANES), 1) < H
    R.trow = lax.broadcasted_iota(jnp.int32, (C, LANES), 0)
    ri = lax.broadcasted_iota(jnp.int32, (C, C), 0)
    ci = lax.broadcasted_iota(jnp.int32, (C, C), 1)
    R.causal = ri >= ci
    if 2 * C == LANES:
        ri = lax.broadcasted_iota(jnp.int32, (2 * LANES, 2 * LANES), 0)
        ci = lax.broadcasted_iota(jnp.int32, (2 * LANES, 2 * LANES), 1)
        rr, cc = ri % LANES, ci % LANES
        R.mask4 = ((rr // C) == (cc // C)) & ((rr % C) >= (cc % C) + (ri < LANES).astype(jnp.int32))
        r1 = lax.broadcasted_iota(jnp.int32, (LANES, LANES), 0)
        c1 = lax.broadcasted_iota(jnp.int32, (LANES, LANES), 1)
        R.eye = (r1 == c1).astype(F32)
    else:
        ri = lax.broadcasted_iota(jnp.int32, (2 * C, 2 * C), 0)
        ci = lax.broadcasted_iota(jnp.int32, (2 * C, 2 * C), 1)
        same_head = (ri // C) == (ci // C)
        R.strict = same_head & ((ri % C) > (ci % C))
        R.incl = same_head & ((ri % C) >= (ci % C))
        R.eye = (ri == ci).astype(F32)

    ro, go = {}, {}
    plans = []
    for bi in rows:
        plans.append(_rwkv_row_stages(bi, R, ro, C=C, n_valid=n_valid, has_vres=has_vres))
        plans.append(_gla_row_stages(bi, R, go, C=C, n_valid=n_valid))
    for d in range(max(len(stages) for stages in plans)):
        for stages in plans:
            if d < len(stages):
                stages[d]()

    merged = [_sigmoid(ga_ref[bi].astype(F32)) * jnp.concatenate(ro[bi], axis=1)
              + _sigmoid(gb_ref[bi].astype(F32)) * jnp.concatenate(go[bi], axis=1) for bi in rows]
    mix = _mm(jnp.concatenate(merged, axis=0).astype(BF16), wo_ref[...])
    for bi in rows:
        o_ref[bi] = h_ref[bi] + mix[bi * C:(bi + 1) * C]

    @pl.when(c == pl.num_programs(1) - 1)
    def _():
        for bi in rows:
            for p in range(N_PAIRS):
                sow_ref[bi, 2 * p] = R.s_scr[bi * N_PAIRS + p, 0:H, 0:H]
                sow_ref[bi, 2 * p + 1] = R.s_scr[bi * N_PAIRS + p, H:2 * H, H:2 * H]
            for h in range(GLA_HEADS):
                sog_ref[bi, h] = R.g_scr[bi * GLA_HEADS + h].T
        if has_s0 and copy_other_layers:
            for other in range(s0w_all.shape[0]):
                if other != layer:
                    sow_all[other] = s0w_all[other]
                    sog_all[other] = s0g_all[other]


def _mixer_call(P, Pfirst, h, s_wkv, s_gla, layer, lw, w_o, *, C, n_valid, BB):
    B, T, _ = P.shape
    has_vres = Pfirst is not None
    has_s0 = s_wkv is not None
    kw = GLA_HEADS * GLA_DK

    def cols(off, width):
        idx = off // width
        return pl.BlockSpec((BB, C, width), lambda b, c: (b, c, idx))

    whole = lambda shape: pl.BlockSpec(shape, lambda b, c: (0,) * len(shape))
    wkv_block = (BB, RWKV_HEADS, RWKV_HEAD_DIM, RWKV_HEAD_DIM)
    gla_block = (BB, GLA_HEADS, GLA_DK, GLA_DV)
    copy_other_layers = has_s0 and layer == 0
    if copy_other_layers:
        depth = s_wkv.shape[0]
        state_spec = lambda blk: pl.BlockSpec((depth,) + blk, lambda b, c: (0, b, 0, 0, 0))
        wkv_shape, gla_shape = s_wkv.shape, s_gla.shape
    elif has_s0:
        state_spec = lambda blk: pl.BlockSpec((None,) + blk, lambda b, c: (layer, b, 0, 0, 0))
        wkv_shape, gla_shape = s_wkv.shape, s_gla.shape
    else:
        state_spec = lambda blk: pl.BlockSpec(blk, lambda b, c: (b, 0, 0, 0))
        wkv_shape, gla_shape = (B,) + wkv_block[1:], (B,) + gla_block[1:]

    in_specs = [cols(COL_R, D_MODEL), cols(COL_K, D_MODEL), cols(COL_V, D_MODEL), cols(COL_LO, LANES)]
    args = [P, P, P, P]
    if has_vres:
        in_specs += [cols(COL_VLO, LANES), cols(COL_V, D_MODEL)]
        args += [P, Pfirst]
    in_specs += [cols(COL_GQ, kw), cols(COL_GK, kw), cols(COL_GV, D_MODEL), cols(COL_GLO, LANES),
                 cols(COL_GATE_A, D_MODEL), cols(COL_GATE_B, D_MODEL), cols(0, D_MODEL)]
    args += [P, P, P, P, P, P, h]
    aliases = {}
    if has_s0:
        if not copy_other_layers:
            aliases = {len(args): 1, len(args) + 1: 2}
        in_specs += [state_spec(wkv_block), state_spec(gla_block)]
        args += [s_wkv, s_gla]
    in_specs.append(whole((LANES, D_MODEL)))
    args.append(lw['w2a'])
    if has_vres:
        in_specs.append(whole((LANES, D_MODEL)))
        args.append(lw['vw2'])
    in_specs += [whole((8, D_MODEL)), whole((LANES, kw)), whole((1, kw)), whole((1, GLA_DV)),
                 pl.BlockSpec((None, D_MODEL, D_MODEL), lambda b, c: (layer, 0, 0))]
    args += [lw['vecs'], lw['ga2'], lw['gab'], lw['gla_g'], w_o]
    return pl.pallas_call(
        functools.partial(_mixer_kernel, C=C, BB=BB, n_valid=n_valid, has_s0=has_s0, has_vres=has_vres,
                          layer=layer, copy_other_layers=copy_other_layers),
        grid=(B // BB, T // C),
        in_specs=in_specs,
        out_specs=[pl.BlockSpec((BB, C, D_MODEL), lambda b, c: (b, c, 0)),
                   state_spec(wkv_block), state_spec(gla_block)],
        out_shape=[jax.ShapeDtypeStruct((B, T, D_MODEL), F32),
                   jax.ShapeDtypeStruct(wkv_shape, F32),
                   jax.ShapeDtypeStruct(gla_shape, F32)],
        input_output_aliases=aliases,
        scratch_shapes=[pltpu.VMEM((BB * N_PAIRS, LANES, LANES), F32),
                        pltpu.VMEM((BB * GLA_HEADS, GLA_DV, GLA_DK), F32)],
        compiler_params=_cparams(("parallel", "arbitrary")),
        name="mixer",
    )(*args)


FFN_SUB = 256
FFN_TM = 1024


def _ffn_kernel(h_ref, g_ref, wg_ref, wu_ref, wd_ref, gf_ref, o_ref, *, final_norm):
    h = h_ref[...]
    xs = _rms_rows(h, g_ref[...]).astype(BF16)
    out = h
    for n in range(FFN_HIDDEN // FFN_SUB):
        cs = slice(n * FFN_SUB, (n + 1) * FFN_SUB)
        gate = _mm(xs, wg_ref[:, cs])
        up = _mm(xs, wu_ref[:, cs])
        act = (gate * _sigmoid(gate) * up).astype(BF16)
        out = out + _mm(act, wd_ref[cs, :])
    if final_norm:
        out = _rms_rows(out, gf_ref[...])
    o_ref[...] = out


def _ffn_call(h, g, w_gu, w_down, layer, gf, final_norm):
    m = h.shape[0]
    tm = min(FFN_TM, m)
    row = lambda i: (i, 0)
    resident = pl.Buffered(1)
    return pl.pallas_call(
        functools.partial(_ffn_kernel, final_norm=final_norm),
        grid=(m // tm,),
        in_specs=[pl.BlockSpec((tm, D_MODEL), row),
                  pl.BlockSpec((1, D_MODEL), lambda i: (0, 0)),
                  pl.BlockSpec((None, D_MODEL, FFN_HIDDEN), lambda i: (layer, 0, 0), pipeline_mode=resident),
                  pl.BlockSpec((None, D_MODEL, FFN_HIDDEN), lambda i: (layer, 0, 1), pipeline_mode=resident),
                  pl.BlockSpec((None, FFN_HIDDEN, D_MODEL), lambda i: (layer, 0, 0), pipeline_mode=resident),
                  pl.BlockSpec((1, D_MODEL), lambda i: (0, 0))],
        out_specs=pl.BlockSpec((tm, D_MODEL), row),
        out_shape=jax.ShapeDtypeStruct((m, D_MODEL), F32),
        compiler_params=_cparams(("parallel",)),
        name="ffn",
    )(h, g, w_gu, w_gu, w_down, gf)


def _rms_kernel(x_ref, g_ref, o_ref):
    o_ref[...] = _rms_rows(x_ref[...], g_ref[...])


def _rms_call(x, g):
    return pl.pallas_call(
        _rms_kernel,
        out_shape=jax.ShapeDtypeStruct(x.shape, F32),
        name="rmsnorm_rows",
    )(x, g)


SRC_GQ, SRC_GK, SRC_GV, SRC_GLO, SRC_GATE_A, SRC_GATE_B = 3200, 3712, 4224, 5248, 5264, 6288
WCAT_ROWS = 256


def _wcat_kernel(w_ref, v1_ref, o_ref):
    def put(dst, src, width):
        o_ref[:, dst:dst + width] = w_ref[src:src + width, :].T.astype(BF16)

    put(COL_R, 0, COL_LO + 2 * LORA_DECAY)
    o_ref[:, COL_VLO:COL_GLO] = v1_ref[...].T.astype(BF16)
    glo = w_ref[SRC_GLO:SRC_GLO + LANES, :].T
    lane = lax.broadcasted_iota(jnp.int32, glo.shape, 1)
    o_ref[:, COL_GLO:COL_GLO + LANES] = jnp.where(lane < GLA_GATE_RANK, glo, 0.0).astype(BF16)
    o_ref[:, COL_GLO + LANES:COL_GQ] = jnp.zeros((o_ref.shape[0], COL_GQ - COL_GLO - LANES), BF16)
    put(COL_GQ, SRC_GQ, GLA_HEADS * GLA_DK)
    put(COL_GATE_A, SRC_GATE_A, D_MODEL)
    put(COL_GATE_B, SRC_GATE_B, D_MODEL)
    put(COL_GV, SRC_GV, GLA_HEADS * GLA_DV)
    put(COL_GK, SRC_GK, GLA_HEADS * GLA_DK)


def _wcat_call(w_in_t, vres_w1_t):
    depth, n_in, _ = w_in_t.shape
    return pl.pallas_call(
        _wcat_kernel,
        grid=(depth, D_MODEL // WCAT_ROWS),
        in_specs=[pl.BlockSpec((None, n_in, WCAT_ROWS), lambda l, i: (l, 0, i)),
                  pl.BlockSpec((None, LANES, WCAT_ROWS), lambda l, i: (l, 0, i))],
        out_specs=pl.BlockSpec((None, WCAT_ROWS, N_PROJ), lambda l, i: (l, i, 0)),
        out_shape=jax.ShapeDtypeStruct((depth, D_MODEL, N_PROJ), BF16),
        compiler_params=_cparams(("parallel", "parallel")),
        name="w_in_reorder",
    )(w_in_t, vres_w1_t)


def _pad_rows(x, n):
    return jnp.pad(x, ((0, n - x.shape[0]), (0, 0)))


def _layer_weights(l, mu_shift, vres_mu, p):
    mu = mu_shift[l]
    zeros_mu = lambda n: jnp.zeros((n,), F32)
    vlo_mu = jnp.pad(vres_mu[l - 1], (0, LANES - LORA_VRES)) if l > 0 else zeros_mu(LANES)
    mu_cat = jnp.concatenate([mu[:COL_VLO], vlo_mu, zeros_mu(N_PROJ - N_SHIFTED)])[None, :]
    lw = dict(mu_cat=mu_cat)
    lw['w2a'] = jnp.concatenate([p['rwkv_w2'][l], p['rwkv_a2'][l]], axis=0)
    lw['vecs'] = jnp.stack([p['rwkv_w0'][l], p['rwkv_a0'][l], p['rwkv_k_k'][l], p['rwkv_k_a'][l],
                            p['rwkv_r_k'][l].reshape(-1), p['rwkv_lnx_g'][l], p['rwkv_lnx_b'][l],
                            p['vres_v0'][l - 1] if l > 0 else zeros_mu(D_MODEL)], axis=0)
    lw['vw2'] = _pad_rows(p['vres_w2'][l - 1], LANES) if l > 0 else None
    lw['ga2'] = _pad_rows(p['gla_a2'][l], LANES)
    lw['gab'] = p['gla_ab'][l][None, :]
    lw['gla_g'] = p['gla_norm_g'][l][None, :]
    lw['g_mix'] = p['norm_mix'][l][None, :]
    lw['g_ffn'] = p['norm_ffn'][l][None, :]
    return lw


def _rows_per_step(batch, want):
    while batch % want:
        want //= 2
    return want


def _run_group(x, shift_state, wkv_state, gla_state, big, weights, g_final):
    w_cat, w_o, w_gu, w_down = big
    B, T, _ = x.shape
    fresh = shift_state is None
    C = CHUNK if fresh else SAMPLE_CHUNK
    Tp = -(-T // C) * C
    BB = _rows_per_step(B, PROMPT_ROWS_PER_STEP if fresh else SAMPLE_ROWS_PER_STEP)
    h = x.reshape(B * T, D_MODEL)
    new_shift, new_wkv, new_gla = [], [], []
    P_first = None
    depth = len(weights)
    for l, lw in enumerate(weights):
        new_shift.append(_rms_call(h.reshape(B, T, D_MODEL)[:, -1], lw['g_mix']))
        P = _proj_call(h, lw['g_mix'], w_cat, l, lw['mu_cat'], None if fresh else shift_state[l], T)
        P = jnp.pad(P.reshape(B, T, N_PROJ), ((0, 0), (0, Tp - T), (0, 0)))
        h3 = jnp.pad(h.reshape(B, T, D_MODEL), ((0, 0), (0, Tp - T), (0, 0)))
        h3, s_wkv, s_gla = _mixer_call(P, P_first if l > 0 else None, h3, wkv_state, gla_state, l, lw, w_o,
                                       C=C, n_valid=min(T, C), BB=BB)
        if l == 0:
            P_first = P
        if fresh:
            new_wkv.append(s_wkv)
            new_gla.append(s_gla)
        else:
            wkv_state, gla_state = s_wkv, s_gla
        h = h3[:, :T].reshape(B * T, D_MODEL)
        h = _ffn_call(h, lw['g_ffn'], w_gu, w_down, l, g_final, final_norm=(l == depth - 1))
    if fresh:
        wkv_state, gla_state = jnp.stack(new_wkv), jnp.stack(new_gla)
    return (h.reshape(B, T, D_MODEL), jnp.stack(new_shift), wkv_state, gla_state)


def kernel(x_prompt, x_sample, state_shift, state_wkv, state_gla, norm_mix, w_in, mu_shift, rwkv_w0, rwkv_w2, rwkv_a0, rwkv_a2, rwkv_k_k, rwkv_k_a, rwkv_r_k, rwkv_lnx_g, rwkv_lnx_b, vres_w1, vres_mu, vres_w2, vres_v0, gla_a2, gla_ab, gla_norm_g, w_o, norm_ffn, ffn_w_gu, ffn_w_down, norm_final):
    p = dict(norm_mix=norm_mix, rwkv_w0=rwkv_w0, rwkv_w2=rwkv_w2, rwkv_a0=rwkv_a0, rwkv_a2=rwkv_a2,
             rwkv_k_k=rwkv_k_k, rwkv_k_a=rwkv_k_a, rwkv_r_k=rwkv_r_k, rwkv_lnx_g=rwkv_lnx_g,
             rwkv_lnx_b=rwkv_lnx_b, vres_w2=vres_w2, vres_v0=vres_v0, gla_a2=gla_a2, gla_ab=gla_ab,
             gla_norm_g=gla_norm_g, w_o=w_o, norm_ffn=norm_ffn, ffn_w_gu=ffn_w_gu, ffn_w_down=ffn_w_down)
    depth = w_in.shape[0]
    weights = [_layer_weights(l, mu_shift, vres_mu, p) for l in range(depth)]
    vres_w1_t = jnp.pad(jnp.swapaxes(vres_w1, 1, 2), ((1, 0), (0, LANES - LORA_VRES), (0, 0)))
    w_cat = _wcat_call(jnp.swapaxes(w_in, 1, 2), vres_w1_t)
    big = (w_cat, w_o.astype(BF16), ffn_w_gu.astype(BF16), ffn_w_down.astype(BF16))
    g_final = norm_final[None, :]
    y_p, shift_p, wkv_p, gla_p = _run_group(x_prompt, None, None, None, big, weights, g_final)
    y_s, shift_s, wkv_s, gla_s = _run_group(x_sample, state_shift, state_wkv, state_gla, big, weights, g_final)
    return (y_p, y_s, shift_p, wkv_p, gla_p, shift_s, wkv_s, gla_s)
```

```python
import functools

import jax
import jax.numpy as jnp
from jax import lax
from jax.experimental import pallas as pl
from jax.experimental.pallas import tpu as pltpu

F32 = jnp.float32
BF16 = jnp.bfloat16

LANES = 128
SUBLANES = 8
D_MODEL = 1024
RWKV_HEAD_DIM = 64
RWKV_HEADS = D_MODEL // RWKV_HEAD_DIM
HEADS_PER_PAIR = LANES // RWKV_HEAD_DIM
N_PAIRS = RWKV_HEADS // HEADS_PER_PAIR
LORA_DECAY = 64
LORA_ICLR = 64
LORA_VRES = 32
RWKV_GN_EPS = 64e-5
GLA_HEADS = 4
GLA_DK = 128
GLA_DV = 256
GLA_GATE_RANK = 16
GLA_GATE_NORMALIZER = 16.0
GLA_NORM_EPS = 1e-5
FFN_HIDDEN = 2816
NORM_EPS = 1e-6
CHUNK = 64
SAMPLE_CHUNK = 8

COL_R, COL_K, COL_V = 0, 1024, 2048
COL_LO, COL_VLO, COL_GLO = 3072, 3200, 3328
COL_GQ = 3584
COL_GATE_A, COL_GATE_B = 4096, 5120
COL_GV, COL_GK = 6144, 7168
N_SHIFTED = 3328
N_PROJ = 7680
PROJ_TM = 512
PROJ_SUB = 512
PROMPT_PROJ_DTYPE = BF16
VMEM_LIMIT = 48 * 1024 * 1024
PROMPT_ROWS_PER_STEP = 4
SAMPLE_ROWS_PER_STEP = 2
DECAY_SCALE = 0.6065306597126334
N_RWKV_VECS = 8


def _cparams(sem):
    return pltpu.CompilerParams(dimension_semantics=sem, vmem_limit_bytes=VMEM_LIMIT)


def _mm(a, b):
    return jnp.dot(a, b, preferred_element_type=F32)


def _mm_nt(a, b):
    return lax.dot_general(a, b, (((1,), (1,)), ((), ())), preferred_element_type=F32)


def _mm_tn(a, b):
    return lax.dot_general(a, b, (((0,), (0,)), ((), ())), preferred_element_type=F32)


def _sigmoid(x):
    return 0.5 * jnp.tanh(0.5 * x) + 0.5


def _softplus(x):
    return jnp.maximum(x, 0.0) + jnp.log(1.0 + jnp.exp(-jnp.abs(x)))


def _cumsum_rows(x):
    n = x.shape[0]
    row = lax.broadcasted_iota(jnp.int32, x.shape, 0)
    s = 1
    while s < n:
        x = x + jnp.where(row >= s, pltpu.roll(x, s, 0), 0.0)
        s *= 2
    return x


def _rms_rows(x, g):
    ms = jnp.mean(x * x, axis=-1, keepdims=True)
    return x * lax.rsqrt(ms + NORM_EPS) * g


def _proj_kernel(*refs, sample, tiles_per_seq, tm):
    if sample:
        x_ref, g_ref, w_ref, mu_ref, prev_ref, o_ref = refs
    else:
        x_ref, g_ref, w_ref, mu_ref, o_ref, last_ref = refs
    xs = _rms_rows(x_ref[...], g_ref[...]).astype(BF16)
    subs = [slice(n * PROJ_SUB, (n + 1) * PROJ_SUB) for n in range(N_PROJ // PROJ_SUB)]
    n_shift = -(-N_SHIFTED // PROJ_SUB)
    if sample:
        ps = prev_ref[...].astype(BF16)
        for n, cs in enumerate(subs):
            y = _mm(xs, w_ref[:, cs])
            if n < n_shift:
                y = y + mu_ref[:, cs] * (_mm(ps, w_ref[:, cs]) - y)
            o_ref[:, cs] = y.astype(o_ref.dtype)
        return

    i = pl.program_id(0)

    @pl.when(i == 0)
    def _():
        last_ref[...] = jnp.zeros_like(last_ref)

    row = lax.broadcasted_iota(jnp.int32, (tm, PROJ_SUB), 0)
    seq_start = i % tiles_per_seq == 0
    for n, cs in enumerate(subs):
        y = _mm(xs, w_ref[:, cs])
        if n < n_shift:
            prev_row = jnp.where(seq_start, 0.0, last_ref[SUBLANES - 1:SUBLANES, cs])
            shifted = jnp.where(row == 0, prev_row, pltpu.roll(y, 1, 0))
            last_ref[:, cs] = y[tm - SUBLANES:, :]
            y = y + mu_ref[:, cs] * (shifted - y)
        o_ref[:, cs] = y.astype(o_ref.dtype)


def _proj_call(x, g, w, layer, mu, prev, seq_len):
    m = x.shape[0]
    sample = prev is not None
    tm = min(PROJ_TM, m if sample else seq_len)
    row = lambda i: (i, 0)
    in_specs = [
        pl.BlockSpec((tm, D_MODEL), row),
        pl.BlockSpec((1, D_MODEL), lambda i: (0, 0)),
        pl.BlockSpec((None, D_MODEL, N_PROJ), lambda i: (layer, 0, 0), pipeline_mode=pl.Buffered(1)),
        pl.BlockSpec((1, N_PROJ), lambda i: (0, 0)),
    ]
    args = [x, g, w, mu]
    scratch = []
    if sample:
        in_specs.append(pl.BlockSpec((tm, D_MODEL), row))
        args.append(prev)
    else:
        scratch.append(pltpu.VMEM((SUBLANES, -(-N_SHIFTED // PROJ_SUB) * PROJ_SUB), F32))
    return pl.pallas_call(
        functools.partial(_proj_kernel, sample=sample, tiles_per_seq=max(seq_len // tm, 1), tm=tm),
        grid=(m // tm,),
        in_specs=in_specs,
        out_specs=pl.BlockSpec((tm, N_PROJ), row),
        out_shape=jax.ShapeDtypeStruct((m, N_PROJ), F32 if sample else PROMPT_PROJ_DTYPE),
        scratch_shapes=scratch,
        compiler_params=_cparams(("arbitrary",)),
        name="proj_sample" if sample else "proj_prompt",
    )(*args)


class _Shared:
    pass


def _rwkv_row_stages(bi, R, res, *, C, n_valid, has_vres):
    H = RWKV_HEAD_DIM
    head0, trow = R.head0, R.trow
    sls = [slice(p * LANES, (p + 1) * LANES) for p in range(N_PAIRS)]
    scr = [bi * N_PAIRS + p for p in range(N_PAIRS)]
    n_sq = C.bit_length() - 2
    fused = 2 * C == LANES
    st = {}
    bf = lambda x: x.astype(BF16)
    cat0 = lambda x, y: jnp.concatenate([x, y], axis=0)
    cat1 = lambda x, y: jnp.concatenate([x, y], axis=1)

    def each(f, *lists):
        return [f(*xs) for xs in zip(*lists)]

    def stack(x):
        return jnp.concatenate([jnp.where(head0, x, 0.0), jnp.where(head0, 0.0, x)], axis=0)

    def head_sum(x):
        s0 = jnp.sum(jnp.where(head0, x, 0.0), axis=-1, keepdims=True)
        s1 = jnp.sum(jnp.where(head0, 0.0, x), axis=-1, keepdims=True)
        return jnp.where(head0, s0, s1)

    def lora():
        lo = R.lo_ref[bi].astype(F32)
        w2a = R.w2a_ref[...]
        st['wl'] = _mm(jnp.where(head0, jnp.tanh(lo), 0.0), w2a)
        st['al'] = _mm(jnp.where(head0, 0.0, lo), w2a)
        if has_vres:
            st['vg'] = _mm(R.vlo_ref[bi].astype(F32), R.vw2_ref[...])

    def operands():
        vec = lambda row: [R.vec_ref[row:row + 1, sl] for sl in sls]
        w0, a0, k_k, k_a, r_k, lnx_g, lnx_b, v0 = (vec(i) for i in range(N_RWKV_VECS))
        r = [R.r_ref[bi, :, sl].astype(F32) for sl in sls]
        k = [R.k_ref[bi, :, sl].astype(F32) for sl in sls]
        v = [R.v_ref[bi, :, sl].astype(F32) for sl in sls]
        if has_vres:
            v = [vp + (R.vf_ref[bi, :, sl].astype(F32) - vp) * _sigmoid(z + st['vg'][:, sl])
                 for vp, sl, z in zip(v, sls, v0)]

        def decay_log(w0p, sl):
            ld = -DECAY_SCALE * _sigmoid(w0p + st['wl'][:, sl])
            if n_valid < C:
                ld = jnp.where(trow < n_valid, ld, 0.0)
            return ld

        ld = each(decay_log, w0, sls)
        a = each(lambda z, sl: _sigmoid(z + st['al'][:, sl]), a0, sls)

        def unit_kk(kp, kkp):
            kk = kp * kkp
            return kk * lax.rsqrt(jnp.maximum(head_sum(kk * kk), 1e-24))

        kk = each(unit_kk, k, k_k)
        k = each(lambda kp, ap, kap: kp * (1.0 + (ap - 1.0) * kap), k, a, k_a)
        b = each(_cumsum_rows, ld)
        e_b = each(jnp.exp, b)
        e_nb = each(lambda x: jnp.exp(-x), b)
        st['al_s'] = each(lambda kkp, bp, ldp: bf(stack(-kkp * jnp.exp(bp - ldp))), kk, b, ld)
        st['be'] = each(lambda ap, kkp, e: bf(stack(ap * kkp * e)), a, kk, e_nb)
        st['kt'] = each(lambda kp, e: bf(stack(kp * e)), k, e_nb)
        st['rt'] = each(lambda rp, e: bf(stack(rp * e)), r, e_b)
        st['vs'] = each(lambda x: bf(stack(x)), v)
        st['decay'] = [e[C - 1:C, :] for e in e_b]
        st['S32'] = [R.s_scr[i] for i in scr]
        st['epi'] = (r, k, v, r_k, lnx_g, lnx_b)

    def finish(o2, s_new):
        for i, s in zip(scr, s_new):
            R.s_scr[i] = s
        r, k, v, r_k, lnx_g, lnx_b = st['epi']
        outs = []
        for p in range(N_PAIRS):
            o = o2[p][:C] + o2[p][C:]
            mean = head_sum(o) * (1.0 / H)
            d = o - mean
            var = head_sum(d * d) * (1.0 / H)
            o = d * lax.rsqrt(var + RWKV_GN_EPS) * lnx_g[p] + lnx_b[p]
            outs.append(o + head_sum(r[p] * k[p] * r_k[p]) * v[p])
        res[bi] = outs

    if fused:
        def amat():
            operands()
            st['ar'] = each(cat0, st['al_s'], st['rt'])
            st['bk'] = each(cat0, st['be'], st['kt'])
            st['amat'] = each(lambda x, y: jnp.where(R.mask4, _mm_nt(x, y), 0.0), st['ar'], st['bk'])

        def xq():
            sv = each(lambda s, y: cat0(bf(s.T), y), st['S32'], st['vs'])
            st['x'] = each(lambda l, m, rhs: _mm(cat1(l, bf(m[:, LANES:])), rhs), st['ar'], st['amat'], sv)
            st['tinv'] = each(lambda m: R.eye + m[:LANES, :LANES], st['amat'])
            q = each(lambda m: bf(m[:LANES, :LANES]), st['amat'])
            st['q'] = each(lambda z: bf(_mm(z, z)), q)

        def double():
            y = each(lambda z, t: _mm(z, cat1(bf(t), z)), st['q'], st['tinv'])
            st['tinv'] = each(lambda t, z: t + z[:, :LANES], st['tinv'], y)
            st['q'] = each(lambda z: bf(z[:, LANES:]), y)

        def double_last():
            st['tinv'] = each(lambda t, z: t + _mm(z, bf(t)), st['tinv'], st['q'])

        def solve():
            st['cm'] = each(lambda t, z: bf(_mm(bf(t), bf(z[:LANES]))), st['tinv'], st['x'])

        def out_state():
            o2 = each(lambda z, m, cc: z[LANES:] + _mm(bf(m[LANES:, :LANES]), cc),
                      st['x'], st['amat'], st['cm'])
            s_new = each(lambda s, cc, y, rhs, e: (s + _mm_tn(cat0(cc, y), rhs)) * e,
                         st['S32'], st['cm'], st['vs'], st['bk'], st['decay'])
            finish(o2, s_new)

        return [lora, amat, xq] + [double] * (n_sq - 1) + [double_last, solve, out_state]

    def small_a():
        operands()
        S = each(bf, st['S32'])
        st['S'] = S
        a_ak = each(lambda x, y: bf(jnp.where(R.strict, _mm_nt(x, y), 0.0)), st['al_s'], st['kt'])
        st['a_rb'] = each(lambda x, y: bf(jnp.where(R.incl, _mm_nt(x, y), 0.0)), st['rt'], st['be'])
        st['a_rk'] = each(lambda x, y: bf(jnp.where(R.incl, _mm_nt(x, y), 0.0)), st['rt'], st['kt'])
        st['cm'] = each(lambda x, s, m, y: _mm_nt(x, s) + _mm(m, y), st['al_s'], S, a_ak, st['vs'])

    def small_solve():
        pw = each(lambda x, y: jnp.where(R.strict, _mm_nt(x, y), 0.0), st['al_s'], st['be'])
        tinv = each(lambda x: R.eye + x, pw)
        pwb = each(bf, pw)
        for _ in range(n_sq):
            pwb = each(lambda x: bf(_mm(x, x)), pwb)
            tinv = each(lambda t, x: t + _mm(bf(t), x), tinv, pwb)
        st['cm'] = each(lambda t, x: _mm(bf(t), bf(x)), tinv, st['cm'])

    def small_out():
        cm = each(bf, st['cm'])
        o2 = each(lambda x, s, m1, c1, m2, y: _mm_nt(x, s) + _mm(m1, c1) + _mm(m2, y),
                  st['rt'], st['S'], st['a_rb'], cm, st['a_rk'], st['vs'])
        s_new = each(lambda s, c1, x, y, z, e: (s + _mm_tn(c1, x) + _mm_tn(y, z)) * e,
                     st['S32'], cm, st['be'], st['vs'], st['kt'], st['decay'])
        finish(o2, s_new)

    return [lora, small_a] + ([small_solve] if n_valid > 1 else []) + [small_out]


def _gla_row_stages(bi, R, res, *, C, n_valid):
    heads = range(GLA_HEADS)
    ks = [slice(h * GLA_DK, (h + 1) * GLA_DK) for h in heads]
    vsl = [slice(h * GLA_DV, (h + 1) * GLA_DV) for h in heads]
    scr = [bi * GLA_HEADS + h for h in heads]
    st = {}

    def gate():
        st['gate'] = _mm(R.glo_ref[bi].astype(F32), R.ga2_ref[...])

    def scores():
        def cum_log_decay(s):
            la = -_softplus(-(st['gate'][:, s] + R.gab_ref[:, s])) * (1.0 / GLA_GATE_NORMALIZER)
            if n_valid < C:
                la = jnp.where(R.trow < n_valid, la, 0.0)
            return _cumsum_rows(la)

        b = [cum_log_decay(s) for s in ks]
        k = [R.gk_ref[bi, :, s].astype(F32) for s in ks]
        st['v'] = [R.gv_ref[bi, :, s].astype(F32) for s in vsl]
        qd = [R.gq_ref[bi, :, s].astype(F32) * (GLA_DK ** -0.5) * jnp.exp(bh) for s, bh in zip(ks, b)]
        kd = [kh * jnp.exp(-bh) for kh, bh in zip(k, b)]
        kl = [kh * jnp.exp(bh[C - 1:C, :] - bh) for kh, bh in zip(k, b)]
        St = [R.g_scr[i] for i in scr]
        st['att'] = [jnp.where(R.causal, _mm_nt(x, y), 0.0) for x, y in zip(qd, kd)]
        st['o_s'] = [_mm_nt(x, s) for x, s in zip(qd, St)]
        upd = [_mm_tn(x, y) for x, y in zip(st['v'], kl)]
        for i, s_old, bh, u in zip(scr, St, b, upd):
            R.g_scr[i] = s_old * jnp.exp(bh[C - 1:C, :]) + u

    def outputs():
        o = [_mm(x, y) + z for x, y, z in zip(st['att'], st['v'], st['o_s'])]
        res[bi] = [oh * lax.rsqrt(jnp.mean(oh * oh, axis=-1, keepdims=True) + GLA_NORM_EPS) * R.gg_ref[...]
                   for oh in o]

    return [gate, scores, outputs]


def _mixer_kernel(*refs, C, BB, n_valid, has_s0, has_vres, layer, copy_other_layers):
    R = _Shared()
    it = iter(refs)
    R.r_ref, R.k_ref, R.v_ref, R.lo_ref = next(it), next(it), next(it), next(it)
    if has_vres:
        R.vlo_ref, R.vf_ref = next(it), next(it)
    R.gq_ref, R.gk_ref, R.gv_ref, R.glo_ref = next(it), next(it), next(it), next(it)
    ga_ref, gb_ref, h_ref = next(it), next(it), next(it)
    if has_s0:
        s0w_ref, s0g_ref = next(it), next(it)
    R.w2a_ref = next(it)
    if has_vres:
        R.vw2_ref = next(it)
    R.vec_ref, R.ga2_ref, R.gab_ref, R.gg_ref, wo_ref = next(it), next(it), next(it), next(it), next(it)
    o_ref, sow_ref, sog_ref, R.s_scr, R.g_scr = next(it), next(it), next(it), next(it), next(it)

    c = pl.program_id(1)
    H = RWKV_HEAD_DIM
    rows = range(BB)
    if has_s0 and copy_other_layers:
        s0w_all, s0g_all, sow_all, sog_all = s0w_ref, s0g_ref, sow_ref, sog_ref
        s0w_ref, s0g_ref = s0w_all.at[layer], s0g_all.at[layer]
        sow_ref, sog_ref = sow_all.at[layer], sog_all.at[layer]

    @pl.when(c == 0)
    def _():
        R.s_scr[...] = jnp.zeros_like(R.s_scr)
        if has_s0:
            for bi in rows:
                for p in range(N_PAIRS):
                    R.s_scr[bi * N_PAIRS + p, 0:H, 0:H] = s0w_ref[bi, 2 * p]
                    R.s_scr[bi * N_PAIRS + p, H:2 * H, H:2 * H] = s0w_ref[bi, 2 * p + 1]
                for h in range(GLA_HEADS):
                    R.g_scr[bi * GLA_HEADS + h] = s0g_ref[bi, h].T
        else:
            R.g_scr[...] = jnp.zeros_like(R.g_scr)

    R.head0 = lax.broadcasted_iota(jnp.int32, (C, LANES), 1) < H
    R.trow = lax.broadcasted_iota(jnp.int32, (C, LANES), 0)
    ri = lax.broadcasted_iota(jnp.int32, (C, C), 0)
    ci = lax.broadcasted_iota(jnp.int32, (C, C), 1)
    R.causal = ri >= ci
    if 2 * C == LANES:
        ri = lax.broadcasted_iota(jnp.int32, (2 * LANES, 2 * LANES), 0)
        ci = lax.broadcasted_iota(jnp.int32, (2 * LANES, 2 * LANES), 1)
        rr, cc = ri % LANES, ci % LANES
        R.mask4 = ((rr // C) == (cc // C)) & ((rr % C) >= (cc % C) + (ri < LANES).astype(jnp.int32))
        r1 = lax.broadcasted_iota(jnp.int32, (LANES, LANES), 0)
        c1 = lax.broadcasted_iota(jnp.int32, (LANES, LANES), 1)
        R.eye = (r1 == c1).astype(F32)
    else:
        ri = lax.broadcasted_iota(jnp.int32, (2 * C, 2 * C), 0)
        ci = lax.broadcasted_iota(jnp.int32, (2 * C, 2 * C), 1)
        same_head = (ri // C) == (ci // C)
        R.strict = same_head & ((ri % C) > (ci % C))
        R.incl = same_head & ((ri % C) >= (ci % C))
        R.eye = (ri == ci).astype(F32)

    ro, go = {}, {}
    plans = []
    for bi in rows:
        plans.append(_rwkv_row_stages(bi, R, ro, C=C, n_valid=n_valid, has_vres=has_vres))
        plans.append(_gla_row_stages(bi, R, go, C=C, n_valid=n_valid))
    for d in range(max(len(stages) for stages in plans)):
        for stages in plans:
            if d < len(stages):
                stages[d]()

    merged = [_sigmoid(ga_ref[bi].astype(F32)) * jnp.concatenate(ro[bi], axis=1)
              + _sigmoid(gb_ref[bi].astype(F32)) * jnp.concatenate(go[bi], axis=1) for bi in rows]
    mix = _mm(jnp.concatenate(merged, axis=0).astype(BF16), wo_ref[...])
    for bi in rows:
        o_ref[bi] = h_ref[bi] + mix[bi * C:(bi + 1) * C]

    @pl.when(c == pl.num_programs(1) - 1)
    def _():
        for bi in rows:
            for p in range(N_PAIRS):
                sow_ref[bi, 2 * p] = R.s_scr[bi * N_PAIRS + p, 0:H, 0:H]
                sow_ref[bi, 2 * p + 1] = R.s_scr[bi * N_PAIRS + p, H:2 * H, H:2 * H]
            for h in range(GLA_HEADS):
                sog_ref[bi, h] = R.g_scr[bi * GLA_HEADS + h].T
        if has_s0 and copy_other_layers:
            for other in range(s0w_all.shape[0]):
                if other != layer:
                    sow_all[other] = s0w_all[other]
                    sog_all[other] = s0g_all[other]


def _mixer_call(P, Pfirst, h, s_wkv, s_gla, layer, lw, w_o, *, C, n_valid, BB):
    B, T, _ = P.shape
    has_vres = Pfirst is not None
    has_s0 = s_wkv is not None
    kw = GLA_HEADS * GLA_DK

    def cols(off, width):
        idx = off // width
        return pl.BlockSpec((BB, C, width), lambda b, c: (b, c, idx))

    whole = lambda shape: pl.BlockSpec(shape, lambda b, c: (0,) * len(shape))
    wkv_block = (BB, RWKV_HEADS, RWKV_HEAD_DIM, RWKV_HEAD_DIM)
    gla_block = (BB, GLA_HEADS, GLA_DK, GLA_DV)
    copy_other_layers = has_s0 and layer == 0
    if copy_other_layers:
        depth = s_wkv.shape[0]
        state_spec = lambda blk: pl.BlockSpec((depth,) + blk, lambda b, c: (0, b, 0, 0, 0))
        wkv_shape, gla_shape = s_wkv.shape, s_gla.shape
    elif has_s0:
        state_spec = lambda blk: pl.BlockSpec((None,) + blk, lambda b, c: (layer, b, 0, 0, 0))
        wkv_shape, gla_shape = s_wkv.shape, s_gla.shape
    else:
        state_spec = lambda blk: pl.BlockSpec(blk, lambda b, c: (b, 0, 0, 0))
        wkv_shape, gla_shape = (B,) + wkv_block[1:], (B,) + gla_block[1:]

    in_specs = [cols(COL_R, D_MODEL), cols(COL_K, D_MODEL), cols(COL_V, D_MODEL), cols(COL_LO, LANES)]
    args = [P, P, P, P]
    if has_vres:
        in_specs += [cols(COL_VLO, LANES), cols(COL_V, D_MODEL)]
        args += [P, Pfirst]
    in_specs += [cols(COL_GQ, kw), cols(COL_GK, kw), cols(COL_GV, D_MODEL), cols(COL_GLO, LANES),
                 cols(COL_GATE_A, D_MODEL), cols(COL_GATE_B, D_MODEL), cols(0, D_MODEL)]
    args += [P, P, P, P, P, P, h]
    aliases = {}
    if has_s0:
        if not copy_other_layers:
            aliases = {len(args): 1, len(args) + 1: 2}
        in_specs += [state_spec(wkv_block), state_spec(gla_block)]
        args += [s_wkv, s_gla]
    in_specs.append(whole((LANES, D_MODEL)))
    args.append(lw['w2a'])
    if has_vres:
        in_specs.append(whole((LANES, D_MODEL)))
        args.append(lw['vw2'])
    in_specs += [whole((N_RWKV_VECS, D_MODEL)), whole((LANES, kw)), whole((1, kw)), whole((1, GLA_DV)),
                 pl.BlockSpec((None, D_MODEL, D_MODEL), lambda b, c: (layer, 0, 0))]
    args += [lw['vecs'], lw['ga2'], lw['gab'], lw['gla_g'], w_o]
    return pl.pallas_call(
        functools.partial(_mixer_kernel, C=C, BB=BB, n_valid=n_valid, has_s0=has_s0, has_vres=has_vres,
                          layer=layer, copy_other_layers=copy_other_layers),
        grid=(B // BB, T // C),
        in_specs=in_specs,
        out_specs=[pl.BlockSpec((BB, C, D_MODEL), lambda b, c: (b, c, 0)),
                   state_spec(wkv_block), state_spec(gla_block)],
        out_shape=[jax.ShapeDtypeStruct((B, T, D_MODEL), F32),
                   jax.ShapeDtypeStruct(wkv_shape, F32),
                   jax.ShapeDtypeStruct(gla_shape, F32)],
        input_output_aliases=aliases,
        scratch_shapes=[pltpu.VMEM((BB * N_PAIRS, LANES, LANES), F32),
                        pltpu.VMEM((BB * GLA_HEADS, GLA_DV, GLA_DK), F32)],
        compiler_params=_cparams(("parallel", "arbitrary")),
        name="mixer",
    )(*args)


FFN_SUB = 256
FFN_TM = 1024
assert FFN_HIDDEN % FFN_SUB == 0 and N_PROJ % PROJ_SUB == 0


def _ffn_kernel(h_ref, g_ref, wg_ref, wu_ref, wd_ref, gf_ref, o_ref, *, final_norm):
    h = h_ref[...]
    xs = _rms_rows(h, g_ref[...]).astype(BF16)
    out = h
    for n in range(FFN_HIDDEN // FFN_SUB):
        cs = slice(n * FFN_SUB, (n + 1) * FFN_SUB)
        gate = _mm(xs, wg_ref[:, cs])
        up = _mm(xs, wu_ref[:, cs])
        act = (gate * _sigmoid(gate) * up).astype(BF16)
        out = out + _mm(act, wd_ref[cs, :])
    if final_norm:
        out = _rms_rows(out, gf_ref[...])
    o_ref[...] = out


def _ffn_call(h, g, w_gu, w_down, layer, gf, final_norm):
    m = h.shape[0]
    tm = min(FFN_TM, m)
    row = lambda i: (i, 0)
    resident = pl.Buffered(1)
    return pl.pallas_call(
        functools.partial(_ffn_kernel, final_norm=final_norm),
        grid=(m // tm,),
        in_specs=[pl.BlockSpec((tm, D_MODEL), row),
                  pl.BlockSpec((1, D_MODEL), lambda i: (0, 0)),
                  pl.BlockSpec((None, D_MODEL, FFN_HIDDEN), lambda i: (layer, 0, 0), pipeline_mode=resident),
                  pl.BlockSpec((None, D_MODEL, FFN_HIDDEN), lambda i: (layer, 0, 1), pipeline_mode=resident),
                  pl.BlockSpec((None, FFN_HIDDEN, D_MODEL), lambda i: (layer, 0, 0), pipeline_mode=resident),
                  pl.BlockSpec((1, D_MODEL), lambda i: (0, 0))],
        out_specs=pl.BlockSpec((tm, D_MODEL), row),
        out_shape=jax.ShapeDtypeStruct((m, D_MODEL), F32),
        compiler_params=_cparams(("parallel",)),
        name="ffn",
    )(h, g, w_gu, w_gu, w_down, gf)


def _rms_kernel(x_ref, g_ref, o_ref):
    o_ref[...] = _rms_rows(x_ref[...], g_ref[...])


def _rms_call(x, g):
    return pl.pallas_call(
        _rms_kernel,
        out_shape=jax.ShapeDtypeStruct(x.shape, F32),
        name="rmsnorm_rows",
    )(x, g)


SRC_GQ, SRC_GK, SRC_GV, SRC_GLO, SRC_GATE_A, SRC_GATE_B = 3200, 3712, 4224, 5248, 5264, 6288
WCAT_ROWS = 256


def _wcat_kernel(w_ref, v1_ref, o_ref):
    def put(dst, src, width):
        o_ref[:, dst:dst + width] = w_ref[src:src + width, :].T.astype(BF16)

    put(COL_R, 0, COL_LO + 2 * LORA_DECAY)
    o_ref[:, COL_VLO:COL_GLO] = v1_ref[...].T.astype(BF16)
    glo = w_ref[SRC_GLO:SRC_GLO + LANES, :].T
    lane = lax.broadcasted_iota(jnp.int32, glo.shape, 1)
    o_ref[:, COL_GLO:COL_GLO + LANES] = jnp.where(lane < GLA_GATE_RANK, glo, 0.0).astype(BF16)
    o_ref[:, COL_GLO + LANES:COL_GQ] = jnp.zeros((o_ref.shape[0], COL_GQ - COL_GLO - LANES), BF16)
    put(COL_GQ, SRC_GQ, GLA_HEADS * GLA_DK)
    put(COL_GATE_A, SRC_GATE_A, D_MODEL)
    put(COL_GATE_B, SRC_GATE_B, D_MODEL)
    put(COL_GV, SRC_GV, GLA_HEADS * GLA_DV)
    put(COL_GK, SRC_GK, GLA_HEADS * GLA_DK)


def _wcat_call(w_in_t, vres_w1_t):
    depth, n_in, _ = w_in_t.shape
    return pl.pallas_call(
        _wcat_kernel,
        grid=(depth, D_MODEL // WCAT_ROWS),
        in_specs=[pl.BlockSpec((None, n_in, WCAT_ROWS), lambda l, i: (l, 0, i)),
                  pl.BlockSpec((None, LANES, WCAT_ROWS), lambda l, i: (l, 0, i))],
        out_specs=pl.BlockSpec((None, WCAT_ROWS, N_PROJ), lambda l, i: (l, i, 0)),
        out_shape=jax.ShapeDtypeStruct((depth, D_MODEL, N_PROJ), BF16),
        compiler_params=_cparams(("parallel", "parallel")),
        name="w_in_reorder",
    )(w_in_t, vres_w1_t)


def _pad_rows(x, n):
    return jnp.pad(x, ((0, n - x.shape[0]), (0, 0)))


def _layer_weights(l, mu_shift, vres_mu, p):
    mu = mu_shift[l]
    zeros_mu = lambda n: jnp.zeros((n,), F32)
    vlo_mu = jnp.pad(vres_mu[l - 1], (0, LANES - LORA_VRES)) if l > 0 else zeros_mu(LANES)
    mu_cat = jnp.concatenate([mu[:COL_VLO], vlo_mu, zeros_mu(N_PROJ - N_SHIFTED)])[None, :]
    lw = dict(mu_cat=mu_cat)
    lw['w2a'] = jnp.concatenate([p['rwkv_w2'][l], p['rwkv_a2'][l]], axis=0)
    lw['vecs'] = jnp.stack([p['rwkv_w0'][l], p['rwkv_a0'][l], p['rwkv_k_k'][l], p['rwkv_k_a'][l],
                            p['rwkv_r_k'][l].reshape(-1), p['rwkv_lnx_g'][l], p['rwkv_lnx_b'][l],
                            p['vres_v0'][l - 1] if l > 0 else zeros_mu(D_MODEL)], axis=0)
    lw['vw2'] = _pad_rows(p['vres_w2'][l - 1], LANES) if l > 0 else None
    lw['ga2'] = _pad_rows(p['gla_a2'][l], LANES)
    lw['gab'] = p['gla_ab'][l][None, :]
    lw['gla_g'] = p['gla_norm_g'][l][None, :]
    lw['g_mix'] = p['norm_mix'][l][None, :]
    lw['g_ffn'] = p['norm_ffn'][l][None, :]
    return lw


def _rows_per_step(batch, want):
    while batch % want:
        want //= 2
    return want


def _run_group(x, shift_state, wkv_state, gla_state, big, weights, g_final):
    w_cat, w_o, w_gu, w_down = big
    B, T, _ = x.shape
    fresh = shift_state is None
    C = CHUNK if fresh else SAMPLE_CHUNK
    Tp = -(-T // C) * C
    BB = _rows_per_step(B, PROMPT_ROWS_PER_STEP if fresh else SAMPLE_ROWS_PER_STEP)
    h = x.reshape(B * T, D_MODEL)
    new_shift, new_wkv, new_gla = [], [], []
    P_first = None
    depth = len(weights)
    for l, lw in enumerate(weights):
        new_shift.append(_rms_call(h.reshape(B, T, D_MODEL)[:, -1], lw['g_mix']))
        P = _proj_call(h, lw['g_mix'], w_cat, l, lw['mu_cat'], None if fresh else shift_state[l], T)
        P = jnp.pad(P.reshape(B, T, N_PROJ), ((0, 0), (0, Tp - T), (0, 0)))
        h3 = jnp.pad(h.reshape(B, T, D_MODEL), ((0, 0), (0, Tp - T), (0, 0)))
        h3, s_wkv, s_gla = _mixer_call(P, P_first if l > 0 else None, h3, wkv_state, gla_state, l, lw, w_o,
                                       C=C, n_valid=min(T, C), BB=BB)
        if l == 0:
            P_first = P
        if fresh:
            new_wkv.append(s_wkv)
            new_gla.append(s_gla)
        else:
            wkv_state, gla_state = s_wkv, s_gla
        h = h3[:, :T].reshape(B * T, D_MODEL)
        h = _ffn_call(h, lw['g_ffn'], w_gu, w_down, l, g_final, final_norm=(l == depth - 1))
    if fresh:
        wkv_state, gla_state = jnp.stack(new_wkv), jnp.stack(new_gla)
    return (h.reshape(B, T, D_MODEL), jnp.stack(new_shift), wkv_state, gla_state)


def kernel(x_prompt, x_sample, state_shift, state_wkv, state_gla, norm_mix, w_in, mu_shift, rwkv_w0, rwkv_w2, rwkv_a0, rwkv_a2, rwkv_k_k, rwkv_k_a, rwkv_r_k, rwkv_lnx_g, rwkv_lnx_b, vres_w1, vres_mu, vres_w2, vres_v0, gla_a2, gla_ab, gla_norm_g, w_o, norm_ffn, ffn_w_gu, ffn_w_down, norm_final):
    p = dict(norm_mix=norm_mix, rwkv_w0=rwkv_w0, rwkv_w2=rwkv_w2, rwkv_a0=rwkv_a0, rwkv_a2=rwkv_a2,
             rwkv_k_k=rwkv_k_k, rwkv_k_a=rwkv_k_a, rwkv_r_k=rwkv_r_k, rwkv_lnx_g=rwkv_lnx_g,
             rwkv_lnx_b=rwkv_lnx_b, vres_w2=vres_w2, vres_v0=vres_v0, gla_a2=gla_a2, gla_ab=gla_ab,
             gla_norm_g=gla_norm_g, w_o=w_o, norm_ffn=norm_ffn, ffn_w_gu=ffn_w_gu, ffn_w_down=ffn_w_down)
    depth = w_in.shape[0]
    weights = [_layer_weights(l, mu_shift, vres_mu, p) for l in range(depth)]
    vres_w1_t = jnp.pad(jnp.swapaxes(vres_w1, 1, 2), ((1, 0), (0, LANES - LORA_VRES), (0, 0)))
    w_cat = _wcat_call(jnp.swapaxes(w_in, 1, 2), vres_w1_t)
    big = (w_cat, w_o.astype(BF16), ffn_w_gu.astype(BF16), ffn_w_down.astype(BF16))
    g_final = norm_final[None, :]
    y_p, shift_p, wkv_p, gla_p = _run_group(x_prompt, None, None, None, big, weights, g_final)
    y_s, shift_s, wkv_s, gla_s = _run_group(x_sample, state_shift, state_wkv, state_gla, big, weights, g_final)
    return (y_p, y_s, shift_p, wkv_p, gla_p, shift_s, wkv_s, gla_s)
```

```python
import functools

import jax
import jax.numpy as jnp
from jax import lax
from jax.experimental import pallas as pl
from jax.experimental.pallas import tpu as pltpu

F32 = jnp.float32
BF16 = jnp.bfloat16

LANES = 128
SUBLANES = 8
D_MODEL = 1024
RWKV_HEAD_DIM = 64
RWKV_HEADS = D_MODEL // RWKV_HEAD_DIM
HEADS_PER_PAIR = LANES // RWKV_HEAD_DIM
N_PAIRS = RWKV_HEADS // HEADS_PER_PAIR
LORA_DECAY = 64
LORA_ICLR = 64
LORA_VRES = 32
RWKV_GN_EPS = 64e-5
GLA_HEADS = 4
GLA_DK = 128
GLA_DV = 256
GLA_GATE_RANK = 16
GLA_GATE_NORMALIZER = 16.0
GLA_NORM_EPS = 1e-5
FFN_HIDDEN = 2816
NORM_EPS = 1e-6
CHUNK = 64
SAMPLE_CHUNK = 8

COL_R, COL_K, COL_V = 0, 1024, 2048
COL_LO, COL_VLO, COL_GLO = 3072, 3200, 3328
COL_GQ = 3584
COL_GATE_A, COL_GATE_B = 4096, 5120
COL_GV, COL_GK = 6144, 7168
N_SHIFTED = 3328
N_PROJ = 7680
PROJ_TM = 512
PROJ_SUB = 512
PROMPT_PROJ_DTYPE = BF16
VMEM_LIMIT = 48 * 1024 * 1024
PROMPT_ROWS_PER_STEP = 4
SAMPLE_ROWS_PER_STEP = 2
DECAY_SCALE = 0.6065306597126334
ROW_GROUP = 2
N_RWKV_VECS = 8


def _cparams(sem):
    return pltpu.CompilerParams(dimension_semantics=sem, vmem_limit_bytes=VMEM_LIMIT)


def _mm(a, b):
    return jnp.dot(a, b, preferred_element_type=F32)


def _mm_nt(a, b):
    return lax.dot_general(a, b, (((1,), (1,)), ((), ())), preferred_element_type=F32)


def _mm_tn(a, b):
    return lax.dot_general(a, b, (((0,), (0,)), ((), ())), preferred_element_type=F32)


def _sigmoid(x):
    return 0.5 * jnp.tanh(0.5 * x) + 0.5


def _softplus(x):
    return jnp.maximum(x, 0.0) + jnp.log(1.0 + jnp.exp(-jnp.abs(x)))


def _cumsum_rows(x):
    n = x.shape[0]
    row = lax.broadcasted_iota(jnp.int32, x.shape, 0)
    s = 1
    while s < n:
        x = x + jnp.where(row >= s, pltpu.roll(x, s, 0), 0.0)
        s *= 2
    return x


def _rms_rows(x, g):
    ms = jnp.mean(x * x, axis=-1, keepdims=True)
    return x * lax.rsqrt(ms + NORM_EPS) * g


def _proj_kernel(*refs, sample, tiles_per_seq, tm):
    if sample:
        x_ref, g_ref, w_ref, mu_ref, prev_ref, o_ref = refs
    else:
        x_ref, g_ref, w_ref, mu_ref, o_ref, last_ref = refs
    xs = _rms_rows(x_ref[...], g_ref[...]).astype(BF16)
    subs = [slice(n * PROJ_SUB, (n + 1) * PROJ_SUB) for n in range(N_PROJ // PROJ_SUB)]
    n_shift = -(-N_SHIFTED // PROJ_SUB)
    if sample:
        ps = prev_ref[...].astype(BF16)
        for n, cs in enumerate(subs):
            y = _mm(xs, w_ref[:, cs])
            if n < n_shift:
                y = y + mu_ref[:, cs] * (_mm(ps, w_ref[:, cs]) - y)
            o_ref[:, cs] = y.astype(o_ref.dtype)
        return

    i = pl.program_id(0)

    @pl.when(i == 0)
    def _():
        last_ref[...] = jnp.zeros_like(last_ref)

    row = lax.broadcasted_iota(jnp.int32, (tm, PROJ_SUB), 0)
    seq_start = i % tiles_per_seq == 0
    for n, cs in enumerate(subs):
        y = _mm(xs, w_ref[:, cs])
        if n < n_shift:
            prev_row = jnp.where(seq_start, 0.0, last_ref[SUBLANES - 1:SUBLANES, cs])
            shifted = jnp.where(row == 0, prev_row, pltpu.roll(y, 1, 0))
            last_ref[:, cs] = y[tm - SUBLANES:, :]
            y = y + mu_ref[:, cs] * (shifted - y)
        o_ref[:, cs] = y.astype(o_ref.dtype)


def _proj_call(x, g, w, layer, mu, prev, seq_len):
    m = x.shape[0]
    sample = prev is not None
    tm = min(PROJ_TM, m if sample else seq_len)
    row = lambda i: (i, 0)
    in_specs = [
        pl.BlockSpec((tm, D_MODEL), row),
        pl.BlockSpec((1, D_MODEL), lambda i: (0, 0)),
        pl.BlockSpec((None, D_MODEL, N_PROJ), lambda i: (layer, 0, 0), pipeline_mode=pl.Buffered(1)),
        pl.BlockSpec((1, N_PROJ), lambda i: (0, 0)),
    ]
    args = [x, g, w, mu]
    scratch = []
    if sample:
        in_specs.append(pl.BlockSpec((tm, D_MODEL), row))
        args.append(prev)
    else:
        scratch.append(pltpu.VMEM((SUBLANES, -(-N_SHIFTED // PROJ_SUB) * PROJ_SUB), F32))
    return pl.pallas_call(
        functools.partial(_proj_kernel, sample=sample, tiles_per_seq=max(seq_len // tm, 1), tm=tm),
        grid=(m // tm,),
        in_specs=in_specs,
        out_specs=pl.BlockSpec((tm, N_PROJ), row),
        out_shape=jax.ShapeDtypeStruct((m, N_PROJ), F32 if sample else PROMPT_PROJ_DTYPE),
        scratch_shapes=scratch,
        compiler_params=_cparams(("arbitrary",)),
        name="proj_sample" if sample else "proj_prompt",
    )(*args)


class _Shared:
    pass


def _rwkv_row_stages(bi, R, res, *, C, n_valid, has_vres):
    H = RWKV_HEAD_DIM
    head0, trow = R.head0, R.trow
    sls = [slice(p * LANES, (p + 1) * LANES) for p in range(N_PAIRS)]
    scr = [bi * N_PAIRS + p for p in range(N_PAIRS)]
    n_sq = C.bit_length() - 2
    fused = 2 * C == LANES
    st = {}
    bf = lambda x: x.astype(BF16)
    cat0 = lambda x, y: jnp.concatenate([x, y], axis=0)
    cat1 = lambda x, y: jnp.concatenate([x, y], axis=1)

    def each(f, *lists):
        return [f(*xs) for xs in zip(*lists)]

    def stack(x):
        return jnp.concatenate([jnp.where(head0, x, 0.0), jnp.where(head0, 0.0, x)], axis=0)

    def head_sum(x):
        s0 = jnp.sum(jnp.where(head0, x, 0.0), axis=-1, keepdims=True)
        s1 = jnp.sum(jnp.where(head0, 0.0, x), axis=-1, keepdims=True)
        return jnp.where(head0, s0, s1)

    def lora():
        lo = R.lo_ref[bi].astype(F32)
        w2a = R.w2a_ref[...]
        st['wl'] = _mm(jnp.where(head0, jnp.tanh(lo), 0.0), w2a)
        st['al'] = _mm(jnp.where(head0, 0.0, lo), w2a)
        if has_vres:
            st['vg'] = _mm(R.vlo_ref[bi].astype(F32), R.vw2_ref[...])

    def operands():
        vec = lambda row: [R.vec_ref[row:row + 1, sl] for sl in sls]
        w0, a0, k_k, k_a, r_k, lnx_g, lnx_b, v0 = (vec(i) for i in range(N_RWKV_VECS))
        r = [R.r_ref[bi, :, sl].astype(F32) for sl in sls]
        k = [R.k_ref[bi, :, sl].astype(F32) for sl in sls]
        v = [R.v_ref[bi, :, sl].astype(F32) for sl in sls]
        if has_vres:
            v = [vp + (R.vf_ref[bi, :, sl].astype(F32) - vp) * _sigmoid(z + st['vg'][:, sl])
                 for vp, sl, z in zip(v, sls, v0)]

        def decay_log(w0p, sl):
            ld = -DECAY_SCALE * _sigmoid(w0p + st['wl'][:, sl])
            if n_valid < C:
                ld = jnp.where(trow < n_valid, ld, 0.0)
            return ld

        ld = each(decay_log, w0, sls)
        a = each(lambda z, sl: _sigmoid(z + st['al'][:, sl]), a0, sls)

        def unit_kk(kp, kkp):
            kk = kp * kkp
            return kk * lax.rsqrt(jnp.maximum(head_sum(kk * kk), 1e-24))

        kk = each(unit_kk, k, k_k)
        k = each(lambda kp, ap, kap: kp * (1.0 + (ap - 1.0) * kap), k, a, k_a)
        b = each(_cumsum_rows, ld)
        e_b = each(jnp.exp, b)
        e_nb = each(lambda x: jnp.exp(-x), b)
        st['al_s'] = each(lambda kkp, bp, ldp: bf(stack(-kkp * jnp.exp(bp - ldp))), kk, b, ld)
        st['be'] = each(lambda ap, kkp, e: bf(stack(ap * kkp * e)), a, kk, e_nb)
        st['kt'] = each(lambda kp, e: bf(stack(kp * e)), k, e_nb)
        st['rt'] = each(lambda rp, e: bf(stack(rp * e)), r, e_b)
        st['vs'] = each(lambda x: bf(stack(x)), v)
        st['decay'] = [e[C - 1:C, :] for e in e_b]
        st['S32'] = [R.s_scr[i] for i in scr]
        st['epi'] = (r, k, v, r_k, lnx_g, lnx_b)

    def finish(o2, s_new):
        for i, s in zip(scr, s_new):
            R.s_scr[i] = s
        r, k, v, r_k, lnx_g, lnx_b = st['epi']
        outs = []
        for p in range(N_PAIRS):
            o = o2[p][:C] + o2[p][C:]
            mean = head_sum(o) * (1.0 / H)
            d = o - mean
            var = head_sum(d * d) * (1.0 / H)
            o = d * lax.rsqrt(var + RWKV_GN_EPS) * lnx_g[p] + lnx_b[p]
            outs.append(o + head_sum(r[p] * k[p] * r_k[p]) * v[p])
        res[bi] = outs

    if fused:
        def amat():
            operands()
            st['ar'] = each(cat0, st['al_s'], st['rt'])
            st['bk'] = each(cat0, st['be'], st['kt'])
            st['amat'] = each(lambda x, y: jnp.where(R.mask4, _mm_nt(x, y), 0.0), st['ar'], st['bk'])

        def xq():
            sv = each(lambda s, y: cat0(bf(s.T), y), st['S32'], st['vs'])
            st['x'] = each(lambda l, m, rhs: _mm(cat1(l, bf(m[:, LANES:])), rhs), st['ar'], st['amat'], sv)
            st['tinv'] = each(lambda m: R.eye + m[:LANES, :LANES], st['amat'])
            q = each(lambda m: bf(m[:LANES, :LANES]), st['amat'])
            st['q'] = each(lambda z: bf(_mm(z, z)), q)

        def double():
            y = each(lambda z, t: _mm(z, cat1(bf(t), z)), st['q'], st['tinv'])
            st['tinv'] = each(lambda t, z: t + z[:, :LANES], st['tinv'], y)
            st['q'] = each(lambda z: bf(z[:, LANES:]), y)

        def double_last():
            st['tinv'] = each(lambda t, z: t + _mm(z, bf(t)), st['tinv'], st['q'])

        def solve():
            st['cm'] = each(lambda t, z: bf(_mm(bf(t), bf(z[:LANES]))), st['tinv'], st['x'])

        def out_state():
            o2 = each(lambda z, m, cc: z[LANES:] + _mm(bf(m[LANES:, :LANES]), cc),
                      st['x'], st['amat'], st['cm'])
            s_new = each(lambda s, cc, y, rhs, e: (s + _mm_tn(cat0(cc, y), rhs)) * e,
                         st['S32'], st['cm'], st['vs'], st['bk'], st['decay'])
            finish(o2, s_new)

        return [lora, amat, xq] + [double] * (n_sq - 1) + [double_last, solve, out_state]

    def small_a():
        operands()
        S = each(bf, st['S32'])
        st['S'] = S
        a_ak = each(lambda x, y: bf(jnp.where(R.strict, _mm_nt(x, y), 0.0)), st['al_s'], st['kt'])
        st['a_rb'] = each(lambda x, y: bf(jnp.where(R.incl, _mm_nt(x, y), 0.0)), st['rt'], st['be'])
        st['a_rk'] = each(lambda x, y: bf(jnp.where(R.incl, _mm_nt(x, y), 0.0)), st['rt'], st['kt'])
        st['cm'] = each(lambda x, s, m, y: _mm_nt(x, s) + _mm(m, y), st['al_s'], S, a_ak, st['vs'])

    def small_solve():
        pw = each(lambda x, y: jnp.where(R.strict, _mm_nt(x, y), 0.0), st['al_s'], st['be'])
        tinv = each(lambda x: R.eye + x, pw)
        pwb = each(bf, pw)
        for _ in range(n_sq):
            pwb = each(lambda x: bf(_mm(x, x)), pwb)
            tinv = each(lambda t, x: t + _mm(bf(t), x), tinv, pwb)
        st['cm'] = each(lambda t, x: _mm(bf(t), bf(x)), tinv, st['cm'])

    def small_out():
        cm = each(bf, st['cm'])
        o2 = each(lambda x, s, m1, c1, m2, y: _mm_nt(x, s) + _mm(m1, c1) + _mm(m2, y),
                  st['rt'], st['S'], st['a_rb'], cm, st['a_rk'], st['vs'])
        s_new = each(lambda s, c1, x, y, z, e: (s + _mm_tn(c1, x) + _mm_tn(y, z)) * e,
                     st['S32'], cm, st['be'], st['vs'], st['kt'], st['decay'])
        finish(o2, s_new)

    return [lora, small_a] + ([small_solve] if n_valid > 1 else []) + [small_out]


def _gla_row_stages(bi, R, res, *, C, n_valid):
    heads = range(GLA_HEADS)
    ks = [slice(h * GLA_DK, (h + 1) * GLA_DK) for h in heads]
    vsl = [slice(h * GLA_DV, (h + 1) * GLA_DV) for h in heads]
    scr = [bi * GLA_HEADS + h for h in heads]
    st = {}

    def gate():
        st['gate'] = _mm(R.glo_ref[bi].astype(F32), R.ga2_ref[...])

    def scores():
        def cum_log_decay(s):
            la = -_softplus(-(st['gate'][:, s] + R.gab_ref[:, s])) * (1.0 / GLA_GATE_NORMALIZER)
            if n_valid < C:
                la = jnp.where(R.trow < n_valid, la, 0.0)
            return _cumsum_rows(la)

        b = [cum_log_decay(s) for s in ks]
        k = [R.gk_ref[bi, :, s].astype(F32) for s in ks]
        st['v'] = [R.gv_ref[bi, :, s].astype(F32) for s in vsl]
        qd = [R.gq_ref[bi, :, s].astype(F32) * (GLA_DK ** -0.5) * jnp.exp(bh) for s, bh in zip(ks, b)]
        kd = [kh * jnp.exp(-bh) for kh, bh in zip(k, b)]
        kl = [kh * jnp.exp(bh[C - 1:C, :] - bh) for kh, bh in zip(k, b)]
        St = [R.g_scr[i] for i in scr]
        st['att'] = [jnp.where(R.causal, _mm_nt(x, y), 0.0) for x, y in zip(qd, kd)]
        st['o_s'] = [_mm_nt(x, s) for x, s in zip(qd, St)]
        upd = [_mm_tn(x, y) for x, y in zip(st['v'], kl)]
        for i, s_old, bh, u in zip(scr, St, b, upd):
            R.g_scr[i] = s_old * jnp.exp(bh[C - 1:C, :]) + u

    def outputs():
        o = [_mm(x, y) + z for x, y, z in zip(st['att'], st['v'], st['o_s'])]
        res[bi] = [oh * lax.rsqrt(jnp.mean(oh * oh, axis=-1, keepdims=True) + GLA_NORM_EPS) * R.gg_ref[...]
                   for oh in o]

    return [gate, scores, outputs]


def _mixer_kernel(*refs, C, BB, n_valid, has_s0, has_vres, layer, copy_other_layers):
    R = _Shared()
    it = iter(refs)
    R.r_ref, R.k_ref, R.v_ref, R.lo_ref = next(it), next(it), next(it), next(it)
    if has_vres:
        R.vlo_ref, R.vf_ref = next(it), next(it)
    R.gq_ref, R.gk_ref, R.gv_ref, R.glo_ref = next(it), next(it), next(it), next(it)
    ga_ref, gb_ref, h_ref = next(it), next(it), next(it)
    if has_s0:
        s0w_ref, s0g_ref = next(it), next(it)
    R.w2a_ref = next(it)
    if has_vres:
        R.vw2_ref = next(it)
    R.vec_ref, R.ga2_ref, R.gab_ref, R.gg_ref, wo_ref = next(it), next(it), next(it), next(it), next(it)
    o_ref, sow_ref, sog_ref, R.s_scr, R.g_scr = next(it), next(it), next(it), next(it), next(it)

    c = pl.program_id(1)
    H = RWKV_HEAD_DIM
    rows = range(BB)
    if has_s0 and copy_other_layers:
        s0w_all, s0g_all, sow_all, sog_all = s0w_ref, s0g_ref, sow_ref, sog_ref
        s0w_ref, s0g_ref = s0w_all.at[layer], s0g_all.at[layer]
        sow_ref, sog_ref = sow_all.at[layer], sog_all.at[layer]

    @pl.when(c == 0)
    def _():
        R.s_scr[...] = jnp.zeros_like(R.s_scr)
        if has_s0:
            for bi in rows:
                for p in range(N_PAIRS):
                    R.s_scr[bi * N_PAIRS + p, 0:H, 0:H] = s0w_ref[bi, 2 * p]
                    R.s_scr[bi * N_PAIRS + p, H:2 * H, H:2 * H] = s0w_ref[bi, 2 * p + 1]
                for h in range(GLA_HEADS):
                    R.g_scr[bi * GLA_HEADS + h] = s0g_ref[bi, h].T
        else:
            R.g_scr[...] = jnp.zeros_like(R.g_scr)

    R.head0 = lax.broadcasted_iota(jnp.int32, (C, LANES), 1) < H
    R.trow = lax.broadcasted_iota(jnp.int32, (C, LANES), 0)
    ri = lax.broadcasted_iota(jnp.int32, (C, C), 0)
    ci = lax.broadcasted_iota(jnp.int32, (C, C), 1)
    R.causal = ri >= ci
    if 2 * C == LANES:
        ri = lax.broadcasted_iota(jnp.int32, (2 * LANES, 2 * LANES), 0)
        ci = lax.broadcasted_iota(jnp.int32, (2 * LANES, 2 * LANES), 1)
        rr, cc = ri % LANES, ci % LANES
        R.mask4 = ((rr // C) == (cc // C)) & ((rr % C) >= (cc % C) + (ri < LANES).astype(jnp.int32))
        r1 = lax.broadcasted_iota(jnp.int32, (LANES, LANES), 0)
        c1 = lax.broadcasted_iota(jnp.int32, (LANES, LANES), 1)
        R.eye = (r1 == c1).astype(F32)
    else:
        ri = lax.broadcasted_iota(jnp.int32, (2 * C, 2 * C), 0)
        ci = lax.broadcasted_iota(jnp.int32, (2 * C, 2 * C), 1)
        same_head = (ri // C) == (ci // C)
        R.strict = same_head & ((ri % C) > (ci % C))
        R.incl = same_head & ((ri % C) >= (ci % C))
        R.eye = (ri == ci).astype(F32)

    ro, go = {}, {}
    for g0 in range(0, BB, ROW_GROUP):
        plans = []
        for bi in range(g0, min(g0 + ROW_GROUP, BB)):
            plans.append(_rwkv_row_stages(bi, R, ro, C=C, n_valid=n_valid, has_vres=has_vres))
            plans.append(_gla_row_stages(bi, R, go, C=C, n_valid=n_valid))
        for d in range(max(len(stages) for stages in plans)):
            for stages in plans:
                if d < len(stages):
                    stages[d]()

    merged = [_sigmoid(ga_ref[bi].astype(F32)) * jnp.concatenate(ro[bi], axis=1)
              + _sigmoid(gb_ref[bi].astype(F32)) * jnp.concatenate(go[bi], axis=1) for bi in rows]
    mix = _mm(jnp.concatenate(merged, axis=0).astype(BF16), wo_ref[...])
    for bi in rows:
        o_ref[bi] = h_ref[bi] + mix[bi * C:(bi + 1) * C]

    @pl.when(c == pl.num_programs(1) - 1)
    def _():
        for bi in rows:
            for p in range(N_PAIRS):
                sow_ref[bi, 2 * p] = R.s_scr[bi * N_PAIRS + p, 0:H, 0:H]
                sow_ref[bi, 2 * p + 1] = R.s_scr[bi * N_PAIRS + p, H:2 * H, H:2 * H]
            for h in range(GLA_HEADS):
                sog_ref[bi, h] = R.g_scr[bi * GLA_HEADS + h].T
        if has_s0 and copy_other_layers:
            for other in range(s0w_all.shape[0]):
                if other != layer:
                    sow_all[other] = s0w_all[other]
                    sog_all[other] = s0g_all[other]


def _mixer_call(P, Pfirst, h, s_wkv, s_gla, layer, lw, w_o, *, C, n_valid, BB):
    B, T, _ = P.shape
    has_vres = Pfirst is not None
    has_s0 = s_wkv is not None
    kw = GLA_HEADS * GLA_DK

    def cols(off, width):
        idx = off // width
        return pl.BlockSpec((BB, C, width), lambda b, c: (b, c, idx))

    whole = lambda shape: pl.BlockSpec(shape, lambda b, c: (0,) * len(shape))
    wkv_block = (BB, RWKV_HEADS, RWKV_HEAD_DIM, RWKV_HEAD_DIM)
    gla_block = (BB, GLA_HEADS, GLA_DK, GLA_DV)
    copy_other_layers = has_s0 and layer == 0
    if copy_other_layers:
        depth = s_wkv.shape[0]
        state_spec = lambda blk: pl.BlockSpec((depth,) + blk, lambda b, c: (0, b, 0, 0, 0))
        wkv_shape, gla_shape = s_wkv.shape, s_gla.shape
    elif has_s0:
        state_spec = lambda blk: pl.BlockSpec((None,) + blk, lambda b, c: (layer, b, 0, 0, 0))
        wkv_shape, gla_shape = s_wkv.shape, s_gla.shape
    else:
        state_spec = lambda blk: pl.BlockSpec(blk, lambda b, c: (b, 0, 0, 0))
        wkv_shape, gla_shape = (B,) + wkv_block[1:], (B,) + gla_block[1:]

    in_specs = [cols(COL_R, D_MODEL), cols(COL_K, D_MODEL), cols(COL_V, D_MODEL), cols(COL_LO, LANES)]
    args = [P, P, P, P]
    if has_vres:
        in_specs += [cols(COL_VLO, LANES), cols(COL_V, D_MODEL)]
        args += [P, Pfirst]
    in_specs += [cols(COL_GQ, kw), cols(COL_GK, kw), cols(COL_GV, D_MODEL), cols(COL_GLO, LANES),
                 cols(COL_GATE_A, D_MODEL), cols(COL_GATE_B, D_MODEL), cols(0, D_MODEL)]
    args += [P, P, P, P, P, P, h]
    aliases = {}
    if has_s0:
        if not copy_other_layers:
            aliases = {len(args): 1, len(args) + 1: 2}
        in_specs += [state_spec(wkv_block), state_spec(gla_block)]
        args += [s_wkv, s_gla]
    in_specs.append(whole((LANES, D_MODEL)))
    args.append(lw['w2a'])
    if has_vres:
        in_specs.append(whole((LANES, D_MODEL)))
        args.append(lw['vw2'])
    in_specs += [whole((N_RWKV_VECS, D_MODEL)), whole((LANES, kw)), whole((1, kw)), whole((1, GLA_DV)),
                 pl.BlockSpec((None, D_MODEL, D_MODEL), lambda b, c: (layer, 0, 0))]
    args += [lw['vecs'], lw['ga2'], lw['gab'], lw['gla_g'], w_o]
    return pl.pallas_call(
        functools.partial(_mixer_kernel, C=C, BB=BB, n_valid=n_valid, has_s0=has_s0, has_vres=has_vres,
                          layer=layer, copy_other_layers=copy_other_layers),
        grid=(B // BB, T // C),
        in_specs=in_specs,
        out_specs=[pl.BlockSpec((BB, C, D_MODEL), lambda b, c: (b, c, 0)),
                   state_spec(wkv_block), state_spec(gla_block)],
        out_shape=[jax.ShapeDtypeStruct((B, T, D_MODEL), F32),
                   jax.ShapeDtypeStruct(wkv_shape, F32),
                   jax.ShapeDtypeStruct(gla_shape, F32)],
        input_output_aliases=aliases,
        scratch_shapes=[pltpu.VMEM((BB * N_PAIRS, LANES, LANES), F32),
                        pltpu.VMEM((BB * GLA_HEADS, GLA_DV, GLA_DK), F32)],
        compiler_params=_cparams(("parallel", "arbitrary")),
        name="mixer",
    )(*args)


FFN_SUB = 256
FFN_TM = 1024
assert FFN_HIDDEN % FFN_SUB == 0 and N_PROJ % PROJ_SUB == 0


def _ffn_kernel(h_ref, g_ref, wg_ref, wu_ref, wd_ref, gf_ref, o_ref, *, final_norm):
    h = h_ref[...]
    xs = _rms_rows(h, g_ref[...]).astype(BF16)
    out = h
    for n in range(FFN_HIDDEN // FFN_SUB):
        cs = slice(n * FFN_SUB, (n + 1) * FFN_SUB)
        gate = _mm(xs, wg_ref[:, cs])
        up = _mm(xs, wu_ref[:, cs])
        act = (gate * _sigmoid(gate) * up).astype(BF16)
        out = out + _mm(act, wd_ref[cs, :])
    if final_norm:
        out = _rms_rows(out, gf_ref[...])
    o_ref[...] = out


def _ffn_call(h, g, w_gu, w_down, layer, gf, final_norm):
    m = h.shape[0]
    tm = min(FFN_TM, m)
    row = lambda i: (i, 0)
    resident = pl.Buffered(1)
    return pl.pallas_call(
        functools.partial(_ffn_kernel, final_norm=final_norm),
        grid=(m // tm,),
        in_specs=[pl.BlockSpec((tm, D_MODEL), row),
                  pl.BlockSpec((1, D_MODEL), lambda i: (0, 0)),
                  pl.BlockSpec((None, D_MODEL, FFN_HIDDEN), lambda i: (layer, 0, 0), pipeline_mode=resident),
                  pl.BlockSpec((None, D_MODEL, FFN_HIDDEN), lambda i: (layer, 0, 1), pipeline_mode=resident),
                  pl.BlockSpec((None, FFN_HIDDEN, D_MODEL), lambda i: (layer, 0, 0), pipeline_mode=resident),
                  pl.BlockSpec((1, D_MODEL), lambda i: (0, 0))],
        out_specs=pl.BlockSpec((tm, D_MODEL), row),
        out_shape=jax.ShapeDtypeStruct((m, D_MODEL), F32),
        compiler_params=_cparams(("parallel",)),
        name="ffn",
    )(h, g, w_gu, w_gu, w_down, gf)


def _rms_kernel(x_ref, g_ref, o_ref):
    o_ref[...] = _rms_rows(x_ref[...], g_ref[...])


def _rms_call(x, g):
    return pl.pallas_call(
        _rms_kernel,
        out_shape=jax.ShapeDtypeStruct(x.shape, F32),
        name="rmsnorm_rows",
    )(x, g)


SRC_GQ, SRC_GK, SRC_GV, SRC_GLO, SRC_GATE_A, SRC_GATE_B = 3200, 3712, 4224, 5248, 5264, 6288
WCAT_ROWS = 256


def _wcat_kernel(w_ref, v1_ref, o_ref):
    def put(dst, src, width):
        o_ref[:, dst:dst + width] = w_ref[src:src + width, :].T.astype(BF16)

    put(COL_R, 0, COL_LO + 2 * LORA_DECAY)
    o_ref[:, COL_VLO:COL_GLO] = v1_ref[...].T.astype(BF16)
    glo = w_ref[SRC_GLO:SRC_GLO + LANES, :].T
    lane = lax.broadcasted_iota(jnp.int32, glo.shape, 1)
    o_ref[:, COL_GLO:COL_GLO + LANES] = jnp.where(lane < GLA_GATE_RANK, glo, 0.0).astype(BF16)
    o_ref[:, COL_GLO + LANES:COL_GQ] = jnp.zeros((o_ref.shape[0], COL_GQ - COL_GLO - LANES), BF16)
    put(COL_GQ, SRC_GQ, GLA_HEADS * GLA_DK)
    put(COL_GATE_A, SRC_GATE_A, D_MODEL)
    put(COL_GATE_B, SRC_GATE_B, D_MODEL)
    put(COL_GV, SRC_GV, GLA_HEADS * GLA_DV)
    put(COL_GK, SRC_GK, GLA_HEADS * GLA_DK)


def _wcat_call(w_in_t, vres_w1_t):
    depth, n_in, _ = w_in_t.shape
    return pl.pallas_call(
        _wcat_kernel,
        grid=(depth, D_MODEL // WCAT_ROWS),
        in_specs=[pl.BlockSpec((None, n_in, WCAT_ROWS), lambda l, i: (l, 0, i)),
                  pl.BlockSpec((None, LANES, WCAT_ROWS), lambda l, i: (l, 0, i))],
        out_specs=pl.BlockSpec((None, WCAT_ROWS, N_PROJ), lambda l, i: (l, i, 0)),
        out_shape=jax.ShapeDtypeStruct((depth, D_MODEL, N_PROJ), BF16),
        compiler_params=_cparams(("parallel", "parallel")),
        name="w_in_reorder",
    )(w_in_t, vres_w1_t)


def _pad_rows(x, n):
    return jnp.pad(x, ((0, n - x.shape[0]), (0, 0)))


def _layer_weights(l, mu_shift, vres_mu, p):
    mu = mu_shift[l]
    zeros_mu = lambda n: jnp.zeros((n,), F32)
    vlo_mu = jnp.pad(vres_mu[l - 1], (0, LANES - LORA_VRES)) if l > 0 else zeros_mu(LANES)
    mu_cat = jnp.concatenate([mu[:COL_VLO], vlo_mu, zeros_mu(N_PROJ - N_SHIFTED)])[None, :]
    lw = dict(mu_cat=mu_cat)
    lw['w2a'] = jnp.concatenate([p['rwkv_w2'][l], p['rwkv_a2'][l]], axis=0)
    lw['vecs'] = jnp.stack([p['rwkv_w0'][l], p['rwkv_a0'][l], p['rwkv_k_k'][l], p['rwkv_k_a'][l],
                            p['rwkv_r_k'][l].reshape(-1), p['rwkv_lnx_g'][l], p['rwkv_lnx_b'][l],
                            p['vres_v0'][l - 1] if l > 0 else zeros_mu(D_MODEL)], axis=0)
    lw['vw2'] = _pad_rows(p['vres_w2'][l - 1], LANES) if l > 0 else None
    lw['ga2'] = _pad_rows(p['gla_a2'][l], LANES)
    lw['gab'] = p['gla_ab'][l][None, :]
    lw['gla_g'] = p['gla_norm_g'][l][None, :]
    lw['g_mix'] = p['norm_mix'][l][None, :]
    lw['g_ffn'] = p['norm_ffn'][l][None, :]
    return lw


def _rows_per_step(batch, want):
    while batch % want:
        want //= 2
    return want


def _run_group(x, shift_state, wkv_state, gla_state, big, weights, g_final):
    w_cat, w_o, w_gu, w_down = big
    B, T, _ = x.shape
    fresh = shift_state is None
    C = CHUNK if fresh else SAMPLE_CHUNK
    Tp = -(-T // C) * C
    BB = _rows_per_step(B, PROMPT_ROWS_PER_STEP if fresh else SAMPLE_ROWS_PER_STEP)
    h = x.reshape(B * T, D_MODEL)
    new_shift, new_wkv, new_gla = [], [], []
    P_first = None
    depth = len(weights)
    for l, lw in enumerate(weights):
        new_shift.append(_rms_call(h.reshape(B, T, D_MODEL)[:, -1], lw['g_mix']))
        P = _proj_call(h, lw['g_mix'], w_cat, l, lw['mu_cat'], None if fresh else shift_state[l], T)
        P = jnp.pad(P.reshape(B, T, N_PROJ), ((0, 0), (0, Tp - T), (0, 0)))
        h3 = jnp.pad(h.reshape(B, T, D_MODEL), ((0, 0), (0, Tp - T), (0, 0)))
        h3, s_wkv, s_gla = _mixer_call(P, P_first if l > 0 else None, h3, wkv_state, gla_state, l, lw, w_o,
                                       C=C, n_valid=min(T, C), BB=BB)
        if l == 0:
            P_first = P
        if fresh:
            new_wkv.append(s_wkv)
            new_gla.append(s_gla)
        else:
            wkv_state, gla_state = s_wkv, s_gla
        h = h3[:, :T].reshape(B * T, D_MODEL)
        h = _ffn_call(h, lw['g_ffn'], w_gu, w_down, l, g_final, final_norm=(l == depth - 1))
    if fresh:
        wkv_state, gla_state = jnp.stack(new_wkv), jnp.stack(new_gla)
    return (h.reshape(B, T, D_MODEL), jnp.stack(new_shift), wkv_state, gla_state)


def kernel(x_prompt, x_sample, state_shift, state_wkv, state_gla, norm_mix, w_in, mu_shift, rwkv_w0, rwkv_w2, rwkv_a0, rwkv_a2, rwkv_k_k, rwkv_k_a, rwkv_r_k, rwkv_lnx_g, rwkv_lnx_b, vres_w1, vres_mu, vres_w2, vres_v0, gla_a2, gla_ab, gla_norm_g, w_o, norm_ffn, ffn_w_gu, ffn_w_down, norm_final):
    p = dict(norm_mix=norm_mix, rwkv_w0=rwkv_w0, rwkv_w2=rwkv_w2, rwkv_a0=rwkv_a0, rwkv_a2=rwkv_a2,
             rwkv_k_k=rwkv_k_k, rwkv_k_a=rwkv_k_a, rwkv_r_k=rwkv_r_k, rwkv_lnx_g=rwkv_lnx_g,
             rwkv_lnx_b=rwkv_lnx_b, vres_w2=vres_w2, vres_v0=vres_v0, gla_a2=gla_a2, gla_ab=gla_ab,
             gla_norm_g=gla_norm_g, w_o=w_o, norm_ffn=norm_ffn, ffn_w_gu=ffn_w_gu, ffn_w_down=ffn_w_down)
    depth = w_in.shape[0]
    weights = [_layer_weights(l, mu_shift, vres_mu, p) for l in range(depth)]
    vres_w1_t = jnp.pad(jnp.swapaxes(vres_w1, 1, 2), ((1, 0), (0, LANES - LORA_VRES), (0, 0)))
    w_cat = _wcat_call(jnp.swapaxes(w_in, 1, 2), vres_w1_t)
    big = (w_cat, w_o.astype(BF16), ffn_w_gu.astype(BF16), ffn_w_down.astype(BF16))
    g_final = norm_final[None, :]
    y_p, shift_p, wkv_p, gla_p = _run_group(x_prompt, None, None, None, big, weights, g_final)
    y_s, shift_s, wkv_s, gla_s = _run_group(x_sample, state_shift, state_wkv, state_gla, big, weights, g_final)
    return (y_p, y_s, shift_p, wkv_p, gla_p, shift_s, wkv_s, gla_s)
```
